```python
import jax, jax.numpy as jnp
from jax import lax
import numpy as np

D_MODEL = 1024
BATCH = 8
SEQ = 8192
DEPTH = 1
DEC_BATCH = 128
DEC_SEQ = 1
PAST_LEN = 8192
PAGE_SIZE = 128

HEAD_DIM = 64
MIX_W = D_MODEL
RET_W = MIX_W // 2
ATT_W = MIX_W - RET_W
H_RET = RET_W // HEAD_DIM
H_ATT = ATT_W // HEAD_DIM
PROJ_SPLITS = [RET_W, 2 * RET_W, 3 * RET_W, 4 * RET_W, 4 * RET_W + ATT_W, 4 * RET_W + 2 * ATT_W]
N_PROJ = 4 * RET_W + 3 * ATT_W
RET_CHUNK = 128
DIL_BRANCHES = ((128, 1), (512, 4), (2048, 16))
MAX_WINDOW = 2048
N_KEYS = 128
N_EXPERTS = N_KEYS * N_KEYS
PEER_HEADS = 8
PEER_QDIM = 128
PEER_HALF = PEER_QDIM // 2
PEER_TOPK = 16
PEER_BLOCK = 128
EPS = 1e-6
F32 = jnp.float32

kernel_name = "hymba_retention_dilated_peer_step"


def rmsnorm(x, g):
    xf = x.astype(F32)
    y = xf * lax.rsqrt(jnp.mean(xf * xf, -1, keepdims=True) + EPS) * g.astype(F32)
    return y.astype(x.dtype)


def head_rmsnorm(x, g):
    xf = x.astype(F32)
    y = xf * lax.rsqrt(jnp.mean(xf * xf, -1, keepdims=True) + EPS) * g.astype(F32)
    return y.astype(x.dtype)


def head_groupnorm(o, g):
    of = o.astype(F32)
    mu = jnp.mean(of, -1, keepdims=True)
    var = jnp.mean(jnp.square(of - mu), -1, keepdims=True)
    return (of - mu) * lax.rsqrt(var + EPS) * g.astype(F32)


def ret_log_decay():
    return jnp.log1p(-jnp.exp2(-5.0 - jnp.arange(H_RET, dtype=F32)))


def alibi_slopes():
    return jnp.exp2(-8.0 * (jnp.arange(H_ATT, dtype=F32) + 1.0) / H_ATT)


def retention(q, k, v, s0):
    B, L, H, dh = q.shape
    cs = RET_CHUNK if L % RET_CHUNK == 0 else L
    nc = L // cs
    lg = ret_log_decay()
    qc = q.reshape(B, nc, cs, H, dh).astype(F32)
    kc = k.reshape(B, nc, cs, H, dh).astype(F32)
    vc = v.reshape(B, nc, cs, H, dh).astype(F32)
    i = jnp.arange(cs)
    diff = i[:, None] - i[None, :]
    intra = jnp.where(diff >= 0, jnp.exp(jnp.maximum(diff, 0)[None].astype(F32) * lg[:, None, None]), 0.0)
    scores = jnp.einsum('bnihd,bnjhd->bnhij', qc, kc) * intra[None, None]
    o_intra = jnp.einsum('bnhij,bnjhe->bnihe', scores, vc)
    w_end = jnp.exp((cs - 1 - i)[None, :].astype(F32) * lg[:, None])
    kv_chunk = jnp.einsum('bnjhd,hj,bnjhe->bnhde', kc, w_end, vc)
    chunk_decay = jnp.exp(cs * lg)[None, :, None, None]

    def step(s, kv):
        return chunk_decay * s + kv, s

    s_fin, s_starts = lax.scan(step, s0, jnp.moveaxis(kv_chunk, 1, 0))
    s_starts = jnp.moveaxis(s_starts, 0, 1)
    w_start = jnp.exp((i + 1)[None, :].astype(F32) * lg[:, None])
    o_inter = jnp.einsum('bnihd,hi,bnhde->bnihe', qc, w_start, s_starts)
    return (o_intra + o_inter).reshape(B, L, H, dh), s_fin


def dilated_branch_prompt(q, k, v, window, dil):
    B, S, H, hd = q.shape
    nw = window // dil
    M = S // dil
    nb = -(-M // nw)
    Mp = nb * nw

    def to_res(x):
        x = x.reshape(B, M, dil, H, hd).transpose(0, 2, 1, 3, 4).astype(F32)
        x = jnp.pad(x, ((0, 0), (0, 0), (0, Mp - M), (0, 0), (0, 0)))
        return x.reshape(B, dil, nb, nw, H, hd)

    def with_prev(x):
        prev = jnp.pad(x, ((0, 0), (0, 0), (1, 0), (0, 0), (0, 0), (0, 0)))[:, :, :-1]
        return jnp.concatenate([prev, x], axis=3)

    qb = to_res(q)
    kk = with_prev(to_res(k))
    vv = with_prev(to_res(v))
    s = jnp.einsum('brnihd,brnjhd->brnhij', qb, kk) * (hd ** -0.5)
    ii = jnp.arange(nw)[:, None]
    jj = jnp.arange(2 * nw)[None, :]
    off = ii + nw - jj
    band = (off >= 0) & (off <= nw)
    blk = jnp.arange(nb)[:, None, None]
    valid = band[None] & (blk * nw + jj[None] - nw >= 0)
    bias = -alibi_slopes()[:, None, None] * (off * dil).astype(F32)[None]
    s = jnp.where(valid[None, None, :, None], s + bias[None, None, None], -jnp.inf)
    m = jnp.max(s, -1, keepdims=True)
    p = jnp.exp(s - m)
    den = jnp.sum(p, -1)
    o = jnp.einsum('brnhij,brnjhd->brnihd', p, vv) / den.transpose(0, 1, 2, 4, 3)[..., None]
    lse = (m[..., 0] + jnp.log(den)).transpose(0, 1, 2, 4, 3)
    o = o.reshape(B, dil, Mp, H, hd)[:, :, :M].transpose(0, 2, 1, 3, 4).reshape(B, S, H, hd)
    lse = lse.reshape(B, dil, Mp, H)[:, :, :M].transpose(0, 2, 1, 3).reshape(B, S, H)
    return o, lse


def merge_branches(outs, lses):
    w = jax.nn.softmax(jnp.stack(lses), axis=0)
    return jnp.sum(w[..., None] * jnp.stack(outs), axis=0)


def dilated_attention_prompt(q, k, v):
    outs, lses = [], []
    for window, dil in DIL_BRANCHES:
        o, lse = dilated_branch_prompt(q, k, v, window, dil)
        outs.append(o)
        lses.append(lse)
    return merge_branches(outs, lses)


def dilated_attention_sample(q, k_new, v_new, win_k, win_v):
    WIN = win_k.shape[1]
    T = q.shape[1]
    hd = q.shape[-1]
    kk = jnp.concatenate([win_k, k_new.astype(win_k.dtype)], axis=1).astype(F32)
    vv = jnp.concatenate([win_v, v_new.astype(win_v.dtype)], axis=1).astype(F32)
    qf = q.astype(F32)
    t = jnp.arange(T)
    outs, lses = [], []
    for window, dil in DIL_BRANCHES:
        nw = window // dil
        jj = jnp.arange(nw + 1)
        idx = WIN + t[:, None] - jj[None, :] * dil
        ok = idx >= 0
        idx_c = jnp.maximum(idx, 0)
        kg = kk[:, idx_c]
        vg = vv[:, idx_c]
        s = jnp.einsum('bthd,btjhd->bhtj', qf, kg) * (hd ** -0.5)
        s = s - alibi_slopes()[:, None, None] * (jj * dil).astype(F32)[None, None, :]
        s = jnp.where(ok[None, None], s, -jnp.inf)
        lse = jax.nn.logsumexp(s, axis=-1)
        p = jnp.exp(s - lse[..., None])
        outs.append(jnp.einsum('bhtj,btjhd->bthd', p, vg))
        lses.append(lse.transpose(0, 2, 1))
    return merge_branches(outs, lses)


def peer(xn, w_pq, sub_keys, peer_u, peer_v):
    N, D = xn.shape
    Np = -(-N // PEER_BLOCK) * PEER_BLOCK
    xp = jnp.pad(xn, ((0, Np - N), (0, 0)))

    def block(xb):
        qh = (xb @ w_pq).reshape(PEER_BLOCK, PEER_HEADS, 2, PEER_HALF)
        sc = jnp.einsum('thcq,hckq->thck', qh, sub_keys).astype(F32)
        s1, i1 = lax.top_k(sc[:, :, 0], PEER_TOPK)
        s2, i2 = lax.top_k(sc[:, :, 1], PEER_TOPK)
        cand = (s1[..., :, None] + s2[..., None, :]).reshape(PEER_BLOCK, PEER_HEADS, PEER_TOPK * PEER_TOPK)
        cidx = (i1[..., :, None] * N_KEYS + i2[..., None, :]).reshape(PEER_BLOCK, PEER_HEADS, PEER_TOPK * PEER_TOPK)
        top_s, top_pos = lax.top_k(cand, PEER_TOPK)
        eidx = jnp.take_along_axis(cidx, top_pos, axis=-1)
        g = jax.nn.softmax(top_s, axis=-1)
        u = jnp.take(peer_u, eidx, axis=0)
        a = jax.nn.gelu(jnp.einsum('thkd,td->thk', u, xb).astype(F32), approximate=False)
        vsel = jnp.take(peer_v, eidx, axis=0)
        return jnp.einsum('thk,thkd->td', (g * a).astype(xb.dtype), vsel)

    y = lax.map(block, xp.reshape(Np // PEER_BLOCK, PEER_BLOCK, D))
    return y.reshape(Np, D)[:N]


def adaln(c, w_ada, b_ada):
    mod = jax.nn.silu(c) @ w_ada + b_ada
    return jnp.split(mod[:, None, :], 6, axis=-1)


def mixer_projections(h, w_in, g_qn, g_kn):
    Bx, L, _ = h.shape
    parts = jnp.split(h @ w_in, PROJ_SPLITS, axis=-1)
    rq, rk, rv, rg, aq, ak, av = [p.reshape(Bx, L, -1, HEAD_DIM) for p in parts]
    return rq, rk * (HEAD_DIM ** -0.5), rv, rg, head_rmsnorm(aq, g_qn), head_rmsnorm(ak, g_kn), av


def mixer_output(o_ret, rg, o_att, g_ret, w_out):
    Bx, L = rg.shape[:2]
    ret = (head_groupnorm(o_ret, g_ret) * jax.nn.silu(rg.astype(F32))).astype(rg.dtype)
    mix = jnp.concatenate([ret, o_att.astype(rg.dtype)], axis=-1).reshape(Bx, L, MIX_W)
    return mix @ w_out


def layer(x, c, core, lw):
    (w_ada, b_ada, g_norm1, w_in, g_qn, g_kn, g_ret, w_out, g_norm2, w_pq, sub_keys, peer_u, peer_v) = lw
    sh1, sc1, gt1, sh2, sc2, gt2 = adaln(c, w_ada, b_ada)
    h = rmsnorm(x, g_norm1) * (1 + sc1) + sh1
    rq, rk, rv, rg, aq, ak, av = mixer_projections(h, w_in, g_qn, g_kn)
    o_ret, o_att, state = core(rq, rk, rv, aq, ak, av)
    x = x + gt1 * mixer_output(o_ret, rg, o_att, g_ret, w_out)
    h2 = rmsnorm(x, g_norm2) * (1 + sc2) + sh2
    Bx, L, D = x.shape
    x = x + gt2 * peer(h2.reshape(Bx * L, D), w_pq, sub_keys, peer_u, peer_v).reshape(Bx, L, D)
    return x, state


def prompt_core(rq, rk, rv, aq, ak, av):
    Bx, L = rq.shape[:2]
    s0 = jnp.zeros((Bx, H_RET, HEAD_DIM, HEAD_DIM), F32)
    o_ret, s_fin = retention(rq, rk, rv, s0)
    o_att = dilated_attention_prompt(aq, ak, av)
    keep = min(MAX_WINDOW, L)
    return o_ret, o_att, (s_fin, ak[:, L - keep:], av[:, L - keep:])


def make_sample_core(s_ret, win_k, win_v):
    def core(rq, rk, rv, aq, ak, av):
        o_ret, s_new = retention(rq, rk, rv, s_ret.astype(F32))
        o_att = dilated_attention_sample(aq, ak, av, win_k, win_v)
        return o_ret, o_att, (s_new, ak, av)
    return core


def setup_inputs(seed: int = 0) -> dict:
    key = jax.random.key(seed)
    ks = jax.random.split(key, 20)
    win = min(MAX_WINDOW, PAST_LEN)

    def nrm(k, shape, s):
        return jax.random.normal(k, shape, F32) * s

    return {
        "x_prompt": nrm(ks[0], (BATCH, SEQ, D_MODEL), 1.0),
        "x_sample": nrm(ks[1], (DEC_BATCH, DEC_SEQ, D_MODEL), 1.0),
        "c_prompt": nrm(ks[2], (BATCH, D_MODEL), 1.0),
        "c_sample": nrm(ks[3], (DEC_BATCH, D_MODEL), 1.0),
        "state_ret": nrm(ks[4], (DEPTH, DEC_BATCH, H_RET, HEAD_DIM, HEAD_DIM), 0.5),
        "cache_win_k": nrm(ks[5], (DEPTH, DEC_BATCH, win, H_ATT, HEAD_DIM), 1.0),
        "cache_win_v": nrm(ks[6], (DEPTH, DEC_BATCH, win, H_ATT, HEAD_DIM), 1.0),
        "w_ada": nrm(ks[7], (DEPTH, D_MODEL, 6 * D_MODEL), 0.5 * D_MODEL ** -0.5),
        "b_ada": nrm(ks[8], (DEPTH, 6 * D_MODEL), 0.02),
        "g_norm1": 1.0 + nrm(ks[9], (DEPTH, D_MODEL), 0.02),
        "w_in": nrm(ks[10], (DEPTH, D_MODEL, N_PROJ), D_MODEL ** -0.5),
        "g_qn": 1.0 + nrm(ks[11], (DEPTH, H_ATT, HEAD_DIM), 0.02),
        "g_kn": 1.0 + nrm(ks[12], (DEPTH, H_ATT, HEAD_DIM), 0.02),
        "g_ret": 1.0 + nrm(ks[13], (DEPTH, H_RET, HEAD_DIM), 0.02),
        "w_out": nrm(ks[14], (DEPTH, MIX_W, D_MODEL), MIX_W ** -0.5),
        "g_norm2": 1.0 + nrm(ks[15], (DEPTH, D_MODEL), 0.02),
        "w_pq": nrm(ks[16], (DEPTH, D_MODEL, PEER_HEADS * PEER_QDIM), D_MODEL ** -0.5),
        "sub_keys": nrm(ks[17], (DEPTH, PEER_HEADS, 2, N_KEYS, PEER_HALF), PEER_HALF ** -0.5),
        "peer_u": nrm(ks[18], (DEPTH, N_EXPERTS, D_MODEL), D_MODEL ** -0.5),
        "peer_v": nrm(ks[19], (DEPTH, N_EXPERTS, D_MODEL), PEER_HEADS ** -0.5),
    }


def reference(x_prompt, x_sample, c_prompt, c_sample, state_ret, cache_win_k, cache_win_v,
              w_ada, b_ada, g_norm1, w_in, g_qn, g_kn, g_ret, w_out, g_norm2,
              w_pq, sub_keys, peer_u, peer_v):
    y_p = x_prompt
    y_s = x_sample
    ret_p, wk_p, wv_p, ret_s, wk_s, wv_s = [], [], [], [], [], []
    for l in range(DEPTH):
        lw = (w_ada[l], b_ada[l], g_norm1[l], w_in[l], g_qn[l], g_kn[l], g_ret[l], w_out[l],
              g_norm2[l], w_pq[l], sub_keys[l], peer_u[l], peer_v[l])
        y_p, (sp, kp, vp) = layer(y_p, c_prompt, prompt_core, lw)
        y_s, (ss, ks_, vs_) = layer(y_s, c_sample, make_sample_core(state_ret[l], cache_win_k[l], cache_win_v[l]), lw)
        ret_p.append(sp); wk_p.append(kp); wv_p.append(vp)
        ret_s.append(ss); wk_s.append(ks_); wv_s.append(vs_)
    new_ret_p = jnp.stack(ret_p)
    new_wk_p = jnp.stack(wk_p)
    new_wv_p = jnp.stack(wv_p)
    new_ret_s = jnp.stack(ret_s)
    new_wk_s = jnp.stack(wk_s)
    new_wv_s = jnp.stack(wv_s)
    return (y_p, y_s, new_ret_p, new_wk_p, new_wv_p, new_ret_s, new_wk_s, new_wv_s)
```

```python
import functools
import math

import numpy as np
import jax
import jax.numpy as jnp
from jax import lax
from jax.experimental import pallas as pl
from jax.experimental.pallas import tpu as pltpu

F32 = jnp.float32
BF16 = jnp.bfloat16
I32 = jnp.int32

D_MODEL = 1024
HEAD_DIM = 64
RET_W = 512
ATT_W = 512
H_RET = 8
H_ATT = 8
N_PROJ = 4 * RET_W + 3 * ATT_W
RET_CHUNK = 128
DIL_BRANCHES = ((128, 1), (512, 4), (2048, 16))
NW = 128
MAX_WINDOW = 2048
N_KEYS = 128
PEER_HEADS = 8
PEER_TOPK = 16
PEER_SLOTS = PEER_HEADS * PEER_TOPK
EPS = 1e-6

LANES = 128
SUBLANES = 8
ROW_WORDS = D_MODEL // 2
ROW_SUB = ROW_WORDS // LANES
VMEM_LIMIT = 56 * 1024 * 1024

SDS = jax.ShapeDtypeStruct


def _dot(a, b):
    return jnp.dot(a, b, preferred_element_type=F32)


def _dot_nt(a, b):
    return lax.dot_general(a, b, (((1,), (1,)), ((), ())), preferred_element_type=F32)


def _dot_tn(a, b):
    return lax.dot_general(a, b, (((0,), (0,)), ((), ())), preferred_element_type=F32)


def _split(x):
    hi = x.astype(BF16)
    lo = (x - hi.astype(F32)).astype(BF16)
    return hi, lo


def _silu(x):
    return x * jax.nn.sigmoid(x)


def _params(sem, vmem=VMEM_LIMIT):
    return pltpu.CompilerParams(dimension_semantics=sem, vmem_limit_bytes=vmem)


def _adaln_kernel(c_ref, w_ref, b_ref, o_ref):
    s = _silu(c_ref[...])
    o_ref[...] = jnp.dot(s, w_ref[...], preferred_element_type=F32,
                         precision=lax.Precision.HIGHEST) + b_ref[...]


def _adaln(c, w_ada, b_ada):
    rows = c.shape[0]
    tn = 512
    return pl.pallas_call(
        _adaln_kernel,
        out_shape=SDS((rows, 6 * D_MODEL), F32),
        grid=(6 * D_MODEL // tn,),
        in_specs=[pl.BlockSpec((rows, D_MODEL), lambda j: (0, 0)),
                  pl.BlockSpec((D_MODEL, tn), lambda j: (0, j)),
                  pl.BlockSpec((1, tn), lambda j: (0, j))],
        out_specs=pl.BlockSpec((rows, tn), lambda j: (0, j)),
        compiler_params=_params(("arbitrary",)),
        name="adaln",
    )(c, w_ada, b_ada.reshape(1, -1))


def _inproj_kernel(x_ref, sc_ref, sh_ref, g1_ref, w_ref, gq_ref, gk_ref, bd_ref,
                   rq_ref, rk_ref, rv_ref, rg_ref, aq_ref, ak_ref, av_ref):
    x = x_ref[...]
    ms = jnp.mean(x * x, axis=-1, keepdims=True)
    h = x * lax.rsqrt(ms + EPS) * g1_ref[...]
    h = h * (1.0 + sc_ref[...]) + sh_ref[...]
    p = _dot(h.astype(BF16), w_ref[...])
    scale = HEAD_DIM ** -0.5
    rq_ref[...] = p[:, 0:RET_W].astype(BF16)
    rk_ref[...] = (p[:, RET_W:2 * RET_W] * scale).astype(BF16)
    rv_ref[...] = p[:, 2 * RET_W:3 * RET_W].astype(BF16)
    rg_ref[...] = p[:, 3 * RET_W:4 * RET_W].astype(BF16)
    o = 4 * RET_W

    def head_norm(a, g):
        msq = _dot((a * a).astype(BF16), bd_ref[...])
        return a * lax.rsqrt(msq + EPS) * g

    aq_ref[...] = head_norm(p[:, o:o + ATT_W], gq_ref[...]) * scale
    ak_ref[...] = head_norm(p[:, o + ATT_W:o + 2 * ATT_W], gk_ref[...])
    av_ref[...] = p[:, o + 2 * ATT_W:o + 3 * ATT_W]


def _inproj(x2d, sc, sh, g1, w_in_bf, gq, gk, bd, tb):
    n = x2d.shape[0]
    groups, rows, _ = sc.shape
    tiles_per_group = n // tb // groups
    mod_spec = pl.BlockSpec((None, rows, D_MODEL), lambda i: (i // tiles_per_group, 0, 0))
    tok = lambda w: pl.BlockSpec((tb, w), lambda i: (i, 0))
    full = lambda a: pl.BlockSpec(a.shape, lambda i: (0,) * a.ndim)
    outs = [SDS((n, RET_W), BF16)] * 4 + [SDS((n, ATT_W), F32)] * 3
    return pl.pallas_call(
        _inproj_kernel,
        out_shape=outs,
        grid=(n // tb,),
        in_specs=[tok(D_MODEL), mod_spec, mod_spec, full(g1), full(w_in_bf), full(gq), full(gk), full(bd)],
        out_specs=[tok(RET_W)] * 4 + [tok(ATT_W)] * 3,
        compiler_params=_params(("arbitrary",)),
        name="inproj",
    )(x2d, sc, sh, g1, w_in_bf, gq, gk, bd)


def _ret_kernel(lg_ref, q_ref, k_ref, v_ref, rg_ref, gr_ref, avg_ref, o_ref, st_ref, s_scr, *, nchunk):
    hp = pl.program_id(1)
    t = pl.program_id(2)
    C = RET_CHUNK
    lane = lax.broadcasted_iota(I32, (1, LANES), 1)
    m0 = lane < HEAD_DIM
    lg0 = lg_ref[2 * hp]
    lg1 = lg_ref[2 * hp + 1]
    lgv = jnp.where(m0, lg0, lg1)

    @pl.when(t == 0)
    def _():
        s_scr[...] = jnp.zeros_like(s_scr)

    ii = lax.broadcasted_iota(I32, (C, C), 0)
    jj = lax.broadcasted_iota(I32, (C, C), 1)
    diff = ii - jj
    dpos = jnp.maximum(diff, 0).astype(F32)
    dec0 = jnp.where(diff >= 0, jnp.exp(dpos * lg0), 0.0)
    dec1 = jnp.where(diff >= 0, jnp.exp(dpos * lg1), 0.0)
    ri = lax.broadcasted_iota(I32, (C, LANES), 0).astype(F32)
    w_start = jnp.exp((ri + 1.0) * lgv)
    w_end = jnp.exp((float(C) - 1.0 - ri) * lgv)
    chunk_decay = jnp.exp(float(C) * lgv)
    rr = lax.broadcasted_iota(I32, (LANES, LANES), 0)
    cc = lax.broadcasted_iota(I32, (LANES, LANES), 1)
    same_head = (rr < HEAD_DIM) == (cc < HEAD_DIM)
    avg = avg_ref[...]
    gr = gr_ref[...]

    def chunk(c, carry):
        rows = pl.ds(pl.multiple_of(c * C, C), C)
        q = q_ref[rows, :]
        k = k_ref[rows, :]
        v = v_ref[rows, :]
        qf = q.astype(F32)
        kf = k.astype(F32)
        q0 = jnp.where(m0, qf, 0.0).astype(BF16)
        q1 = jnp.where(m0, 0.0, qf).astype(BF16)
        p0 = (_dot_nt(q0, k) * dec0).astype(BF16)
        p1 = (_dot_nt(q1, k) * dec1).astype(BF16)
        o_intra = jnp.where(m0, _dot(p0, v), _dot(p1, v))
        state = s_scr[...]
        o_inter = _dot((qf * w_start).astype(BF16), state.astype(BF16))
        o = o_intra + o_inter
        kv = _dot_tn((kf * w_end).astype(BF16), v)
        s_scr[...] = chunk_decay * state + jnp.where(same_head, kv, 0.0)
        o_hi, o_lo = _split(o)
        mu = _dot(o_hi, avg) + _dot(o_lo, avg)
        xc = o - mu
        sq_hi, sq_lo = _split(xc * xc)
        var = _dot(sq_hi, avg) + _dot(sq_lo, avg)
        y = xc * lax.rsqrt(var + EPS) * gr
        y = y * _silu(rg_ref[rows, :].astype(F32))
        o_ref[rows, :] = y.astype(BF16)
        return carry

    lax.fori_loop(0, nchunk, chunk, 0)

    @pl.when(t == pl.num_programs(2) - 1)
    def _():
        state = s_scr[...]
        st_ref[0] = state[0:HEAD_DIM, 0:HEAD_DIM]
        st_ref[1] = state[HEAD_DIM:, HEAD_DIM:]


def _retention_prompt(lg, rq, rk, rv, rg, g_ret, avg, cb):
    b, s, _ = rq.shape
    nchunk = cb // RET_CHUNK
    blk = pl.BlockSpec((None, cb, LANES), lambda bi, hp, t, lg_: (bi, t, hp))
    grid_spec = pltpu.PrefetchScalarGridSpec(
        num_scalar_prefetch=1,
        grid=(b, H_RET // 2, s // cb),
        in_specs=[blk, blk, blk, blk,
                  pl.BlockSpec((1, LANES), lambda bi, hp, t, lg_: (0, hp)),
                  pl.BlockSpec((LANES, LANES), lambda bi, hp, t, lg_: (0, 0))],
        out_specs=[blk, pl.BlockSpec((None, 2, HEAD_DIM, HEAD_DIM), lambda bi, hp, t, lg_: (bi, hp, 0, 0))],
        scratch_shapes=[pltpu.VMEM((LANES, LANES), F32)],
    )
    return pl.pallas_call(
        functools.partial(_ret_kernel, nchunk=nchunk),
        out_shape=[SDS((b, s, RET_W), BF16), SDS((b, H_RET, HEAD_DIM, HEAD_DIM), F32)],
        grid_spec=grid_spec,
        compiler_params=_params(("arbitrary", "arbitrary", "arbitrary")),
        name="retention_prompt",
    )(lg, rq, rk, rv, rg, g_ret, avg)


ATT_TILE = MAX_WINDOW


def _att_kernel(sl_ref, q_ref, k_ref, v_ref, o_ref, kk, vv, m_s, l_s, acc_s):
    hp = pl.program_id(1)
    t = pl.program_id(2)
    TS = ATT_TILE

    @pl.when(t == 0)
    def _():
        kk[0:TS, :] = jnp.zeros((TS, LANES), F32)
        vv[0:TS, :] = jnp.zeros((TS, LANES), F32)

    kk[TS:2 * TS, :] = k_ref[...]
    vv[TS:2 * TS, :] = v_ref[...]
    m_s[...] = jnp.full((TS, LANES), -jnp.inf, F32)
    l_s[...] = jnp.zeros((TS, LANES), F32)
    acc_s[...] = jnp.zeros((TS, LANES), F32)

    lane = lax.broadcasted_iota(I32, (1, LANES), 1)
    m0 = lane < HEAD_DIM
    ii = lax.broadcasted_iota(I32, (NW, 2 * NW), 0)
    jj = lax.broadcasted_iota(I32, (NW, 2 * NW), 1)
    off = ii + NW - jj
    band = (off >= 0) & (off <= NW)
    off_f = off.astype(F32)
    slopes = (sl_ref[2 * hp], sl_ref[2 * hp + 1])

    for _, dil in DIL_BRANCHES:
        nblk = TS // (NW * dil)
        shift = int(math.log2(nblk))

        def body(bs, carry, dil=dil, nblk=nblk, shift=shift):
            r = lax.shift_right_logical(bs, shift)
            blk = bs & (nblk - 1)
            qs = r + dil * NW * blk
            ks = TS + qs - dil * NW
            q_rows = pl.ds(qs, NW, stride=dil)
            k_rows = pl.ds(ks, 2 * NW, stride=dil)
            qb = q_ref[q_rows, :]
            kb = kk[k_rows, :].astype(BF16)
            vb = vv[k_rows, :].astype(BF16)
            first_key = jnp.where(jnp.logical_or(t > 0, blk > 0), 0, NW)
            valid = band & (jj >= first_key)
            parts = []
            for hh in range(2):
                qh = jnp.where(m0 if hh == 0 else jnp.logical_not(m0), qb, 0.0).astype(BF16)
                s = _dot_nt(qh, kb)
                s = jnp.where(valid, s - (slopes[hh] * float(dil)) * off_f, -jnp.inf)
                mh = jnp.max(s, axis=-1, keepdims=True)
                p = jnp.exp(s - mh)
                lh = jnp.sum(p, axis=-1, keepdims=True)
                parts.append((mh, lh, _dot(p.astype(BF16), vb)))
            m_cur = jnp.where(m0, parts[0][0], parts[1][0])
            l_cur = jnp.where(m0, parts[0][1], parts[1][1])
            pv_cur = jnp.where(m0, parts[0][2], parts[1][2])
            m_old = m_s[q_rows, :]
            m_new = jnp.maximum(m_old, m_cur)
            a_old = jnp.exp(m_old - m_new)
            a_cur = jnp.exp(m_cur - m_new)
            l_s[q_rows, :] = a_old * l_s[q_rows, :] + a_cur * l_cur
            acc_s[q_rows, :] = a_old * acc_s[q_rows, :] + a_cur * pv_cur
            m_s[q_rows, :] = m_new
            return carry

        lax.fori_loop(0, TS // NW, body, 0)

    o_ref[...] = (acc_s[...] / l_s[...]).astype(BF16)
    kk[0:TS, :] = kk[TS:2 * TS, :]
    vv[0:TS, :] = vv[TS:2 * TS, :]


def _attention_prompt(slopes, aq, ak, av):
    b, s, _ = aq.shape
    ts = ATT_TILE
    blk = pl.BlockSpec((None, ts, LANES), lambda bi, hp, t, sl: (bi, t, hp))
    grid_spec = pltpu.PrefetchScalarGridSpec(
        num_scalar_prefetch=1,
        grid=(b, H_ATT // 2, s // ts),
        in_specs=[blk, blk, blk],
        out_specs=blk,
        scratch_shapes=[pltpu.VMEM((2 * ts, LANES), F32), pltpu.VMEM((2 * ts, LANES), F32),
                        pltpu.VMEM((ts, LANES), F32), pltpu.VMEM((ts, LANES), F32),
                        pltpu.VMEM((ts, LANES), F32)],
    )
    return pl.pallas_call(
        _att_kernel,
        out_shape=SDS((b, s, ATT_W), BF16),
        grid_spec=grid_spec,
        compiler_params=_params(("arbitrary", "arbitrary", "arbitrary")),
        name="attention_prompt",
    )(slopes, aq, ak, av)


def _sret_kernel(lg_ref, q_ref, k_ref, v_ref, rg_ref, gr_ref, st_ref, o_ref, sn_ref):
    q = q_ref[...].astype(F32)
    k = k_ref[...].astype(F32)
    v = v_ref[...].astype(F32)
    rg = rg_ref[...].astype(F32)
    gr = gr_ref[...]
    r_i = lax.broadcasted_iota(I32, (HEAD_DIM, HEAD_DIM), 0)
    c_i = lax.broadcasted_iota(I32, (HEAD_DIM, HEAD_DIM), 1)
    eye = r_i == c_i
    outs = []
    for h in range(H_RET):
        sl = slice(h * HEAD_DIM, (h + 1) * HEAD_DIM)
        qh, kh, vh = q[:, sl], k[:, sl], v[:, sl]
        gamma = jnp.exp(jnp.full((1, HEAD_DIM), lg_ref[h], F32))
        q_col = jnp.sum(jnp.where(eye, qh, 0.0), axis=1, keepdims=True)
        k_col = jnp.sum(jnp.where(eye, kh, 0.0), axis=1, keepdims=True)
        state = st_ref[h]
        o_inter = gamma * jnp.sum(q_col * state, axis=0, keepdims=True)
        o_intra = jnp.sum(qh * kh, axis=-1, keepdims=True) * vh
        sn_ref[h] = gamma * state + k_col * vh
        o = o_intra + o_inter
        mu = jnp.mean(o, axis=-1, keepdims=True)
        xc = o - mu
        var = jnp.mean(xc * xc, axis=-1, keepdims=True)
        outs.append(xc * lax.rsqrt(var + EPS) * gr[:, sl] * _silu(rg[:, sl]))
    o_ref[...] = jnp.concatenate(outs, axis=-1).astype(BF16)


def _retention_sample(lg, rq, rk, rv, rg, g_ret, state):
    db = rq.shape[0]
    row = pl.BlockSpec((None, 1, RET_W), lambda i: (i, 0, 0))
    st = pl.BlockSpec((None, H_RET, HEAD_DIM, HEAD_DIM), lambda i: (i, 0, 0, 0))
    r3 = lambda a: a.reshape(db, 1, RET_W)
    return pl.pallas_call(
        _sret_kernel,
        out_shape=[SDS((db, 1, RET_W), BF16), SDS(state.shape, F32)],
        grid=(db,),
        in_specs=[pl.BlockSpec(memory_space=pltpu.SMEM), row, row, row, row,
                  pl.BlockSpec((1, RET_W), lambda i: (0, 0)), st],
        out_specs=[row, st],
        compiler_params=_params(("arbitrary",)),
        name="retention_sample",
    )(lg, r3(rq), r3(rk), r3(rv), r3(rg), g_ret, state)


def _satt_kernel(sl_ref, q_ref, kn_ref, vn_ref, wk_ref, wv_ref, hm_ref, hmt_ref, o_ref):
    hp = pl.program_id(1)
    win = wk_ref.shape[0]
    q = q_ref[...]
    hm = hm_ref[...]
    hmt = hmt_ref[...]
    lane = lax.broadcasted_iota(I32, (1, LANES), 1)
    slope_v = jnp.where(lane == 0, sl_ref[2 * hp], jnp.where(lane == 1, sl_ref[2 * hp + 1], 0.0))

    def head_sums(x):
        hi, lo = _split(x)
        return _dot(hi, hm) + _dot(lo, hm)

    def expand(x):
        hi, lo = _split(x)
        return _dot(hi, hmt) + _dot(lo, hmt)

    s_new = head_sums(q * kn_ref[...])
    steps = float(NW) - lax.broadcasted_iota(I32, (NW, LANES), 0).astype(F32)
    scores, rows = [], []
    m = s_new
    for _, dil in DIL_BRANCHES:
        r = pl.ds(win - NW * dil, NW, stride=dil)
        s = head_sums(wk_ref[r, :] * q) - slope_v * (steps * float(dil))
        scores.append(s)
        rows.append(r)
        m = jnp.maximum(m, jnp.max(s, axis=0, keepdims=True))
    p_new = jnp.exp(s_new - m)
    nb = float(len(DIL_BRANCHES))
    den = nb * p_new
    num = nb * expand(p_new) * vn_ref[...]
    for s, r in zip(scores, rows):
        p = jnp.exp(s - m)
        den = den + jnp.sum(p, axis=0, keepdims=True)
        num = num + jnp.sum(expand(p) * wv_ref[r, :], axis=0, keepdims=True)
    o_ref[...] = (num / expand(den)).astype(BF16)


def _attention_sample(slopes, aq, ak, av, win_k, win_v, hm, hmt):
    db = aq.shape[0]
    win = win_k.shape[1]
    row = pl.BlockSpec((None, 1, LANES), lambda i, hp: (i, 0, hp))
    wblk = pl.BlockSpec((None, win, LANES), lambda i, hp: (i, 0, hp))
    r3 = lambda a: a.reshape(db, 1, ATT_W)
    return pl.pallas_call(
        _satt_kernel,
        out_shape=SDS((db, 1, ATT_W), BF16),
        grid=(db, H_ATT // 2),
        in_specs=[pl.BlockSpec(memory_space=pltpu.SMEM), row, row, row, wblk, wblk,
                  pl.BlockSpec(hm.shape, lambda i, hp: (0, 0)), pl.BlockSpec(hmt.shape, lambda i, hp: (0, 0))],
        out_specs=row,
        compiler_params=_params(("arbitrary", "arbitrary")),
        name="attention_sample",
    )(slopes, r3(aq), r3(ak), r3(av), win_k.reshape(db, win, ATT_W), win_v.reshape(db, win, ATT_W), hm, hmt)


def _outproj_kernel(x_ref, ret_ref, att_ref, wo_ref, gt1_ref, g2_ref, sc2_ref, sh2_ref, wqh_ref, wql_ref,
                    x1_ref, h2_ref, qp_ref):
    mix = _dot(ret_ref[...], wo_ref[0:RET_W, :]) + _dot(att_ref[...], wo_ref[RET_W:, :])
    x1 = x_ref[...] + gt1_ref[...] * mix
    x1_ref[...] = x1
    ms = jnp.mean(x1 * x1, axis=-1, keepdims=True)
    h2 = x1 * lax.rsqrt(ms + EPS) * g2_ref[...]
    h2 = h2 * (1.0 + sc2_ref[...]) + sh2_ref[...]
    h2_ref[...] = h2
    hi, lo = _split(h2)
    wqh = wqh_ref[...]
    qp_ref[...] = _dot(hi, wqh) + (_dot(lo, wqh) + _dot(hi, wql_ref[...]))


def _outproj(x2d, ret, att, wo_bf, gt1, g2, sc2, sh2, wq_hi, wq_lo, tb):
    n = x2d.shape[0]
    groups, rows, _ = gt1.shape
    tiles_per_group = n // tb // groups
    mod_spec = pl.BlockSpec((None, rows, D_MODEL), lambda i: (i // tiles_per_group, 0, 0))
    tok = lambda w: pl.BlockSpec((tb, w), lambda i: (i, 0))
    full = lambda a: pl.BlockSpec(a.shape, lambda i: (0,) * a.ndim)
    return pl.pallas_call(
        _outproj_kernel,
        out_shape=[SDS((n, D_MODEL), F32)] * 3,
        grid=(n // tb,),
        in_specs=[tok(D_MODEL), tok(RET_W), tok(ATT_W), full(wo_bf), mod_spec, full(g2), mod_spec, mod_spec,
                  full(wq_hi), full(wq_lo)],
        out_specs=[tok(D_MODEL)] * 3,
        compiler_params=_params(("arbitrary",)),
        name="outproj",
    )(x2d, ret, att, wo_bf, gt1, g2, sc2, sh2, wq_hi, wq_lo)


def _top16(sc):
    n = sc.shape[0]
    kio = lax.broadcasted_iota(I32, sc.shape, 0)
    vals, idxs = [], []
    for _ in range(PEER_TOPK):
        m = jnp.max(sc, axis=0, keepdims=True)
        idx = jnp.min(jnp.where(sc == m, kio, n), axis=0, keepdims=True)
        vals.append(m)
        idxs.append(idx)
        sc = jnp.where(kio == idx, -jnp.inf, sc)
    return jnp.concatenate(vals, axis=0), jnp.concatenate(idxs, axis=0)


def _topk_kernel(q_ref, skh_ref, skl_ref, e_ref, g_ref):
    qh, ql = _split(q_ref[...])

    def scores(c):
        kh = skh_ref[c]
        return _dot_nt(kh, qh) + (_dot_nt(kh, ql) + _dot_nt(skl_ref[c], qh))

    s1, i1 = _top16(scores(0))
    s2, i2 = _top16(scores(1))
    cand = jnp.concatenate([s1[a:a + 1, :] + s2 for a in range(PEER_TOPK)], axis=0)
    top_s, pos = _top16(cand)
    a_sel = lax.shift_right_logical(pos, 4)
    b_sel = pos & (PEER_TOPK - 1)
    io = lax.broadcasted_iota(I32, i1.shape, 0)
    rows = []
    for r in range(PEER_TOPK):
        ia = jnp.sum(jnp.where(io == a_sel[r:r + 1, :], i1, 0), axis=0, keepdims=True)
        ib = jnp.sum(jnp.where(io == b_sel[r:r + 1, :], i2, 0), axis=0, keepdims=True)
        rows.append(ia * N_KEYS + ib)
    e_ref[...] = jnp.concatenate(rows, axis=0) * ROW_SUB
    ex = jnp.exp(top_s - top_s[0:1, :])
    g_ref[...] = ex / jnp.sum(ex, axis=0, keepdims=True)


def _peer_topk(qp, sk_hi, sk_lo, tt):
    n = qp.shape[0]
    out_blk = pl.BlockSpec((None, PEER_TOPK, tt), lambda i, h: (h, 0, i))
    sk_blk = pl.BlockSpec((None, 2, N_KEYS, LANES), lambda i, h: (h, 0, 0, 0))
    e, g = pl.pallas_call(
        _topk_kernel,
        out_shape=[SDS((PEER_HEADS, PEER_TOPK, n), I32), SDS((PEER_HEADS, PEER_TOPK, n), F32)],
        grid=(n // tt, PEER_HEADS),
        in_specs=[pl.BlockSpec((tt, LANES), lambda i, h: (i, h)), sk_blk, sk_blk],
        out_specs=[out_blk, out_blk],
        compiler_params=_params(("arbitrary", "arbitrary")),
        name="peer_topk",
    )(qp, sk_hi, sk_lo)
    to_tok = lambda a: jnp.transpose(a, (2, 0, 1)).reshape(n, PEER_SLOTS)
    return to_tok(e), to_tok(g)


def _gather_rows(idx_ref, tab_ref, gbuf, t):
    for k in range(PEER_SLOTS):
        e = pl.multiple_of(idx_ref[t, k], ROW_SUB)
        gbuf[ROW_SUB * k:ROW_SUB * (k + 1), :] = tab_ref[pl.ds(e, ROW_SUB), :]


def _diag_mask():
    sub = lax.broadcasted_iota(I32, (SUBLANES, D_MODEL), 0)
    col = lax.broadcasted_iota(I32, (SUBLANES, D_MODEL), 1)
    return (col & (SUBLANES - 1)) == sub


def _peer_u_kernel(idx_ref, tab_ref, h2_ref, g_ref, sel_ref, w_ref, gbuf, xbuf, rall):
    tb = h2_ref.shape[0]
    for j in range(SUBLANES):
        xbuf[pl.ds(j, tb, stride=SUBLANES), :] = h2_ref[:, LANES * j:LANES * (j + 1)]
    diag = _diag_mask()

    def tok(t, carry):
        _gather_rows(idx_ref, tab_ref, gbuf, t)
        rows = pltpu.bitcast(gbuf[...], BF16)
        xt = xbuf[pl.ds(pl.multiple_of(t * SUBLANES, SUBLANES), SUBLANES), :].astype(BF16)
        r = _dot_nt(xt, rows)
        rall[pl.ds(t, 1), :] = jnp.sum(jnp.where(diag, r, 0.0), axis=0, keepdims=True)
        return carry

    lax.fori_loop(0, tb, tok, 0)
    hi, lo = _split(rall[...])
    sel = sel_ref[...]
    a = _dot(hi, sel) + _dot(lo, sel)
    gelu = 0.5 * a * (1.0 + lax.erf(a * (2.0 ** -0.5)))
    w_ref[...] = g_ref[...] * gelu


def _peer_v_kernel(idx_ref, tab_ref, w_ref, x1_ref, gt2_ref, selt_ref, y_ref, gbuf, wrep, ybuf):
    tb = x1_ref.shape[0]
    wrep[...] = _dot(w_ref[...].astype(BF16), selt_ref[...])
    diag = _diag_mask()

    def tok(t, carry):
        _gather_rows(idx_ref, tab_ref, gbuf, t)
        rows = pltpu.bitcast(gbuf[...], BF16)
        wt = jnp.where(diag, wrep[pl.ds(t, 1), :], 0.0).astype(BF16)
        ybuf[pl.ds(pl.multiple_of(t * SUBLANES, SUBLANES), SUBLANES), :] = _dot(wt, rows)
        return carry

    lax.fori_loop(0, tb, tok, 0)
    for j in range(SUBLANES):
        cols = slice(LANES * j, LANES * (j + 1))
        y_ref[:, cols] = x1_ref[:, cols] + gt2_ref[:, cols] * ybuf[pl.ds(j, tb, stride=SUBLANES), :]


def _peer_u(idx, tab, h2, g, sel, tb):
    n = h2.shape[0]
    tok = lambda w: pl.BlockSpec((tb, w), lambda i: (i, 0))
    return pl.pallas_call(
        _peer_u_kernel,
        out_shape=SDS((n, PEER_SLOTS), F32),
        grid=(n // tb,),
        in_specs=[pl.BlockSpec((tb, PEER_SLOTS), lambda i: (i, 0), memory_space=pltpu.SMEM),
                  pl.BlockSpec(memory_space=pltpu.VMEM), tok(D_MODEL), tok(PEER_SLOTS),
                  pl.BlockSpec(sel.shape, lambda i: (0, 0))],
        out_specs=tok(PEER_SLOTS),
        scratch_shapes=[pltpu.VMEM((ROW_SUB * PEER_SLOTS, LANES), I32),
                        pltpu.VMEM((tb * SUBLANES, LANES), F32),
                        pltpu.VMEM((tb, D_MODEL), F32)],
        compiler_params=_params(("arbitrary",)),
        name="peer_u",
    )(idx, tab, h2, g, sel)


def _peer_v(idx, tab, w, x1, gt2, selt, tb):
    n = x1.shape[0]
    groups, rows, _ = gt2.shape
    tiles_per_group = n // tb // groups
    tok = lambda wd: pl.BlockSpec((tb, wd), lambda i: (i, 0))
    return pl.pallas_call(
        _peer_v_kernel,
        out_shape=SDS((n, D_MODEL), F32),
        grid=(n // tb,),
        in_specs=[pl.BlockSpec((tb, PEER_SLOTS), lambda i: (i, 0), memory_space=pltpu.SMEM),
                  pl.BlockSpec(memory_space=pltpu.VMEM), tok(PEER_SLOTS), tok(D_MODEL),
                  pl.BlockSpec((None, rows, D_MODEL), lambda i: (i // tiles_per_group, 0, 0)),
                  pl.BlockSpec(selt.shape, lambda i: (0, 0))],
        out_specs=tok(D_MODEL),
        scratch_shapes=[pltpu.VMEM((ROW_SUB * PEER_SLOTS, LANES), I32),
                        pltpu.VMEM((tb, D_MODEL), F32),
                        pltpu.VMEM((tb * SUBLANES, LANES), F32)],
        compiler_params=_params(("arbitrary",)),
        name="peer_v",
    )(idx, tab, w, x1, gt2, selt)


def _pack_table(tab):
    e = tab.shape[0]
    bits = lax.bitcast_convert_type(tab.astype(BF16), jnp.uint16).astype(jnp.uint32)
    bits = bits.reshape(e, ROW_SUB, 2, LANES)
    word = bits[:, :, 0, :] | (bits[:, :, 1, :] << 16)
    return lax.bitcast_convert_type(word, I32).reshape(e * ROW_SUB, LANES)


def _block_avg(width):
    i = np.arange(width) // HEAD_DIM
    return jnp.asarray((i[:, None] == i[None, :]).astype(np.float32) / HEAD_DIM, BF16)


def _slot_select():
    col = np.arange(D_MODEL) // SUBLANES
    return jnp.asarray((col[:, None] == np.arange(PEER_SLOTS)[None, :]).astype(np.float32), BF16)


def _head_indicator():
    d = np.arange(LANES) // HEAD_DIM
    return jnp.asarray((d[:, None] == np.arange(LANES)[None, :]).astype(np.float32), BF16)


def _ret_log_decay():
    return jnp.log1p(-jnp.exp2(-5.0 - jnp.arange(H_RET, dtype=F32)))


def _alibi_slopes():
    return jnp.exp2(-8.0 * (jnp.arange(H_ATT, dtype=F32) + 1.0) / H_ATT)


def _pad_sub_keys(sub_keys):
    z = jnp.zeros_like(sub_keys[:, 0])
    return jnp.stack([jnp.concatenate([sub_keys[:, 0], z], axis=-1),
                      jnp.concatenate([z, sub_keys[:, 1]], axis=-1)], axis=1)


def _group_w_out(w):
    w4 = w.reshape(H_RET, 2, HEAD_DIM, D_MODEL)
    return jnp.concatenate([w4[:, 0].reshape(RET_W, D_MODEL), w4[:, 1].reshape(ATT_W, D_MODEL)], axis=0)


def _split_w(w):
    hi = w.astype(BF16)
    return hi, (w - hi.astype(F32)).astype(BF16)


def _token_tile(n, pref):
    return pref if n % pref == 0 else n


def _peer(h2, qp, x1, gt2, consts, weights):
    n = h2.shape[0]
    idx, gate = _peer_topk(qp, weights["sk_hi"], weights["sk_lo"], _token_tile(n, 256))
    tb = _token_tile(n, 256)
    w = _peer_u(idx, weights["u_tab"], h2, gate, consts["sel"], tb)
    return _peer_v(idx, weights["v_tab"], w, x1, gt2, consts["selt"], tb)


def _mix_and_peer(x2d, ret, att, mods, consts, weights):
    sh1, sc1, gt1, sh2, sc2, gt2 = mods
    n = x2d.shape[0]
    x1, h2, qp = _outproj(x2d, ret, att, weights["w_out"], gt1, weights["g_norm2"], sc2, sh2,
                          weights["wq_hi"], weights["wq_lo"], _token_tile(n, 512))
    return _peer(h2, qp, x1, gt2, consts, weights)


def _project(x2d, mods, consts, weights):
    sh1, sc1 = mods[0], mods[1]
    n = x2d.shape[0]
    return _inproj(x2d, sc1, sh1, weights["g_norm1"], weights["w_in"], weights["g_qn"], weights["g_kn"],
                   consts["avg_att"], _token_tile(n, 512))


def kernel(x_prompt, x_sample, c_prompt, c_sample, state_ret, cache_win_k, cache_win_v, w_ada, b_ada, g_norm1, w_in, g_qn, g_kn, g_ret, w_out, g_norm2, w_pq, sub_keys, peer_u, peer_v):
    b, s, d = x_prompt.shape
    db, ds_, _ = x_sample.shape
    depth = w_ada.shape[0]
    assert d == D_MODEL and ds_ == 1 and s % ATT_TILE == 0
    win = cache_win_k.shape[2]
    assert win == MAX_WINDOW

    consts = {
        "avg_att": _block_avg(ATT_W),
        "avg_pair": _block_avg(LANES),
        "sel": _slot_select(),
        "selt": _slot_select().T,
        "hm": _head_indicator(),
        "hmt": _head_indicator().T,
        "lg": _ret_log_decay(),
        "slopes": _alibi_slopes(),
    }

    y_p = x_prompt.reshape(b * s, d)
    y_s = x_sample.reshape(db, d)
    outs = {k: [] for k in ("ret_p", "wk_p", "wv_p", "ret_s", "wk_s", "wv_s")}
    for l in range(depth):
        wq_hi, wq_lo = _split_w(w_pq[l])
        sk_hi, sk_lo = _split_w(_pad_sub_keys(sub_keys[l]))
        weights = {
            "g_norm1": g_norm1[l].reshape(1, d), "g_norm2": g_norm2[l].reshape(1, d),
            "w_in": w_in[l].astype(BF16), "w_out": _group_w_out(w_out[l]).astype(BF16),
            "g_qn": g_qn[l].reshape(1, ATT_W), "g_kn": g_kn[l].reshape(1, ATT_W),
            "wq_hi": wq_hi, "wq_lo": wq_lo, "sk_hi": sk_hi, "sk_lo": sk_lo,
            "u_tab": _pack_table(peer_u[l]), "v_tab": _pack_table(peer_v[l]),
        }
        gr = g_ret[l].reshape(1, RET_W)

        mod = _adaln(jnp.concatenate([c_prompt, c_sample], axis=0), w_ada[l], b_ada[l])
        mod6 = jnp.split(mod, 6, axis=-1)
        mods_p = [m[:b].reshape(b, 1, d) for m in mod6]
        mods_s = [m[b:].reshape(1, db, d) for m in mod6]

        rq, rk, rv, rg, aq, ak, av = _project(y_p, mods_p, consts, weights)
        seq = lambda a: a.reshape(b, s, a.shape[-1])
        ret, ret_state = _retention_prompt(consts["lg"], seq(rq), seq(rk), seq(rv), seq(rg), gr,
                                           consts["avg_pair"], 1024)
        att = _attention_prompt(consts["slopes"], seq(aq), seq(ak), seq(av))
        y_p = _mix_and_peer(y_p, ret.reshape(b * s, RET_W), att.reshape(b * s, ATT_W), mods_p, consts, weights)
        keep = min(MAX_WINDOW, s)
        outs["ret_p"].append(ret_state)
        outs["wk_p"].append(seq(ak)[:, s - keep:].reshape(b, keep, H_ATT, HEAD_DIM))
        outs["wv_p"].append(seq(av)[:, s - keep:].reshape(b, keep, H_ATT, HEAD_DIM))

        rq, rk, rv, rg, aq, ak, av = _project(y_s, mods_s, consts, weights)
        ret, new_state = _retention_sample(consts["lg"], rq, rk, rv, rg, gr, state_ret[l])
        att = _attention_sample(consts["slopes"], aq, ak, av, cache_win_k[l], cache_win_v[l],
                                consts["hm"], consts["hmt"])
        y_s = _mix_and_peer(y_s, ret.reshape(db, RET_W), att.reshape(db, ATT_W), mods_s, consts, weights)
        outs["ret_s"].append(new_state)
        outs["wk_s"].append(ak.reshape(db, 1, H_ATT, HEAD_DIM))
        outs["wv_s"].append(av.reshape(db, 1, H_ATT, HEAD_DIM))

    st = lambda k: jnp.stack(outs[k])
    return (y_p.reshape(b, s, d), y_s.reshape(db, 1, d), st("ret_p"), st("wk_p"), st("wv_p"),
            st("ret_s"), st("wk_s"), st("wv_s"))
```

```python
import functools
import math

import numpy as np
import jax
import jax.numpy as jnp
from jax import lax
from jax.experimental import pallas as pl
from jax.experimental.pallas import tpu as pltpu

F32 = jnp.float32
BF16 = jnp.bfloat16
I32 = jnp.int32

D_MODEL = 1024
HEAD_DIM = 64
RET_W = 512
ATT_W = 512
H_RET = 8
H_ATT = 8
N_PROJ = 4 * RET_W + 3 * ATT_W
RET_CHUNK = 128
DIL_BRANCHES = ((128, 1), (512, 4), (2048, 16))
NW = 128
MAX_WINDOW = 2048
N_KEYS = 128
PEER_HEADS = 8
PEER_TOPK = 16
PEER_SLOTS = PEER_HEADS * PEER_TOPK
EPS = 1e-6

LANES = 128
SUBLANES = 8
ROW_WORDS = D_MODEL // 2
ROW_SUB = ROW_WORDS // LANES
VMEM_LIMIT = 56 * 1024 * 1024
TOK_UNROLL = 8

SDS = jax.ShapeDtypeStruct


def _dot(a, b):
    return jnp.dot(a, b, preferred_element_type=F32)


def _dot_nt(a, b):
    return lax.dot_general(a, b, (((1,), (1,)), ((), ())), preferred_element_type=F32)


def _dot_tn(a, b):
    return lax.dot_general(a, b, (((0,), (0,)), ((), ())), preferred_element_type=F32)


def _split(x):
    hi = x.astype(BF16)
    lo = (x - hi.astype(F32)).astype(BF16)
    return hi, lo


def _silu(x):
    return x * jax.nn.sigmoid(x)


def _params(sem, vmem=VMEM_LIMIT):
    return pltpu.CompilerParams(dimension_semantics=sem, vmem_limit_bytes=vmem)


def _adaln_kernel(c_ref, w_ref, b_ref, o_ref):
    s = _silu(c_ref[...])
    o_ref[...] = jnp.dot(s, w_ref[...], preferred_element_type=F32,
                         precision=lax.Precision.HIGHEST) + b_ref[...]


def _adaln(c, w_ada, b_ada):
    rows = c.shape[0]
    tn = 512
    return pl.pallas_call(
        _adaln_kernel,
        out_shape=SDS((rows, 6 * D_MODEL), F32),
        grid=(6 * D_MODEL // tn,),
        in_specs=[pl.BlockSpec((rows, D_MODEL), lambda j: (0, 0)),
                  pl.BlockSpec((D_MODEL, tn), lambda j: (0, j)),
                  pl.BlockSpec((1, tn), lambda j: (0, j))],
        out_specs=pl.BlockSpec((rows, tn), lambda j: (0, j)),
        compiler_params=_params(("arbitrary",)),
        name="adaln",
    )(c, w_ada, b_ada.reshape(1, -1))


def _inproj_kernel(x_ref, sc_ref, sh_ref, g1_ref, w_ref, gq_ref, gk_ref, bd_ref,
                   rq_ref, rk_ref, rv_ref, rg_ref, aq_ref, ak_ref, av_ref):
    x = x_ref[...]
    ms = jnp.mean(x * x, axis=-1, keepdims=True)
    h = x * lax.rsqrt(ms + EPS) * g1_ref[...]
    h = h * (1.0 + sc_ref[...]) + sh_ref[...]
    p = _dot(h.astype(BF16), w_ref[...])
    scale = HEAD_DIM ** -0.5
    rq_ref[...] = p[:, 0:RET_W].astype(BF16)
    rk_ref[...] = (p[:, RET_W:2 * RET_W] * scale).astype(BF16)
    rv_ref[...] = p[:, 2 * RET_W:3 * RET_W].astype(BF16)
    rg_ref[...] = p[:, 3 * RET_W:4 * RET_W].astype(BF16)
    o = 4 * RET_W

    def head_norm(a, g):
        msq = _dot((a * a).astype(BF16), bd_ref[...])
        return a * lax.rsqrt(msq + EPS) * g

    aq_ref[...] = head_norm(p[:, o:o + ATT_W], gq_ref[...]) * scale
    ak_ref[...] = head_norm(p[:, o + ATT_W:o + 2 * ATT_W], gk_ref[...])
    av_ref[...] = p[:, o + 2 * ATT_W:o + 3 * ATT_W]


def _inproj(x2d, sc, sh, g1, w_in_bf, gq, gk, bd, tb):
    n = x2d.shape[0]
    groups, rows, _ = sc.shape
    tiles_per_group = n // tb // groups
    mod_spec = pl.BlockSpec((None, rows, D_MODEL), lambda i: (i // tiles_per_group, 0, 0))
    tok = lambda w: pl.BlockSpec((tb, w), lambda i: (i, 0))
    full = lambda a: pl.BlockSpec(a.shape, lambda i: (0,) * a.ndim)
    outs = [SDS((n, RET_W), BF16)] * 4 + [SDS((n, ATT_W), F32)] * 3
    return pl.pallas_call(
        _inproj_kernel,
        out_shape=outs,
        grid=(n // tb,),
        in_specs=[tok(D_MODEL), mod_spec, mod_spec, full(g1), full(w_in_bf), full(gq), full(gk), full(bd)],
        out_specs=[tok(RET_W)] * 4 + [tok(ATT_W)] * 3,
        compiler_params=_params(("arbitrary",)),
        name="inproj",
    )(x2d, sc, sh, g1, w_in_bf, gq, gk, bd)


def _ret_kernel(lg_ref, q_ref, k_ref, v_ref, rg_ref, gr_ref, avg_ref, o_ref, st_ref, s_scr, *, nchunk):
    hp = pl.program_id(1)
    t = pl.program_id(2)
    C = RET_CHUNK
    lane = lax.broadcasted_iota(I32, (1, LANES), 1)
    m0 = lane < HEAD_DIM
    lg0 = lg_ref[2 * hp]
    lg1 = lg_ref[2 * hp + 1]
    lgv = jnp.where(m0, lg0, lg1)

    @pl.when(t == 0)
    def _():
        s_scr[...] = jnp.zeros_like(s_scr)

    ii = lax.broadcasted_iota(I32, (C, C), 0)
    jj = lax.broadcasted_iota(I32, (C, C), 1)
    diff = ii - jj
    dpos = jnp.maximum(diff, 0).astype(F32)
    dec0 = jnp.where(diff >= 0, jnp.exp(dpos * lg0), 0.0)
    dec1 = jnp.where(diff >= 0, jnp.exp(dpos * lg1), 0.0)
    ri = lax.broadcasted_iota(I32, (C, LANES), 0).astype(F32)
    w_start = jnp.exp((ri + 1.0) * lgv)
    w_end = jnp.exp((float(C) - 1.0 - ri) * lgv)
    chunk_decay = jnp.exp(float(C) * lgv)
    rr = lax.broadcasted_iota(I32, (LANES, LANES), 0)
    cc = lax.broadcasted_iota(I32, (LANES, LANES), 1)
    same_head = (rr < HEAD_DIM) == (cc < HEAD_DIM)
    avg = avg_ref[...]
    gr = gr_ref[...]

    def chunk(c, carry):
        rows = pl.ds(pl.multiple_of(c * C, C), C)
        q = q_ref[rows, :]
        k = k_ref[rows, :]
        v = v_ref[rows, :]
        qf = q.astype(F32)
        kf = k.astype(F32)
        q0 = jnp.where(m0, qf, 0.0).astype(BF16)
        q1 = jnp.where(m0, 0.0, qf).astype(BF16)
        p0 = (_dot_nt(q0, k) * dec0).astype(BF16)
        p1 = (_dot_nt(q1, k) * dec1).astype(BF16)
        o_intra = jnp.where(m0, _dot(p0, v), _dot(p1, v))
        state = s_scr[...]
        o_inter = _dot((qf * w_start).astype(BF16), state.astype(BF16))
        o = o_intra + o_inter
        kv = _dot_tn((kf * w_end).astype(BF16), v)
        s_scr[...] = chunk_decay * state + jnp.where(same_head, kv, 0.0)
        o_hi, o_lo = _split(o)
        mu = _dot(o_hi, avg) + _dot(o_lo, avg)
        xc = o - mu
        sq_hi, sq_lo = _split(xc * xc)
        var = _dot(sq_hi, avg) + _dot(sq_lo, avg)
        y = xc * lax.rsqrt(var + EPS) * gr
        y = y * _silu(rg_ref[rows, :].astype(F32))
        o_ref[rows, :] = y.astype(BF16)
        return carry

    lax.fori_loop(0, nchunk, chunk, 0)

    @pl.when(t == pl.num_programs(2) - 1)
    def _():
        state = s_scr[...]
        st_ref[0] = state[0:HEAD_DIM, 0:HEAD_DIM]
        st_ref[1] = state[HEAD_DIM:, HEAD_DIM:]


def _retention_prompt(lg, rq, rk, rv, rg, g_ret, avg, cb):
    b, s, _ = rq.shape
    nchunk = cb // RET_CHUNK
    blk = pl.BlockSpec((None, cb, LANES), lambda bi, hp, t, lg_: (bi, t, hp))
    grid_spec = pltpu.PrefetchScalarGridSpec(
        num_scalar_prefetch=1,
        grid=(b, H_RET // 2, s // cb),
        in_specs=[blk, blk, blk, blk,
                  pl.BlockSpec((1, LANES), lambda bi, hp, t, lg_: (0, hp)),
                  pl.BlockSpec((LANES, LANES), lambda bi, hp, t, lg_: (0, 0))],
        out_specs=[blk, pl.BlockSpec((None, 2, HEAD_DIM, HEAD_DIM), lambda bi, hp, t, lg_: (bi, hp, 0, 0))],
        scratch_shapes=[pltpu.VMEM((LANES, LANES), F32)],
    )
    return pl.pallas_call(
        functools.partial(_ret_kernel, nchunk=nchunk),
        out_shape=[SDS((b, s, RET_W), BF16), SDS((b, H_RET, HEAD_DIM, HEAD_DIM), F32)],
        grid_spec=grid_spec,
        compiler_params=_params(("arbitrary", "arbitrary", "arbitrary")),
        name="retention_prompt",
    )(lg, rq, rk, rv, rg, g_ret, avg)


ATT_TILE = MAX_WINDOW


ATT_UNROLL = 4


def _att_kernel(sl_ref, q_ref, k_ref, v_ref, o_ref, kk, vv, *stats):
    hp = pl.program_id(1)
    t = pl.program_id(2)
    TS = ATT_TILE
    nbr = len(DIL_BRANCHES)
    m_br, l_br, acc_br = stats[:nbr], stats[nbr:2 * nbr], stats[2 * nbr:]
    cur = pl.multiple_of((t & 1) * TS, TS)

    @pl.when(t == 0)
    def _():
        kk[TS:2 * TS, :] = jnp.zeros((TS, LANES), F32)
        vv[TS:2 * TS, :] = jnp.zeros((TS, LANES), F32)

    kk[pl.ds(cur, TS), :] = k_ref[...]
    vv[pl.ds(cur, TS), :] = v_ref[...]

    lane = lax.broadcasted_iota(I32, (1, LANES), 1)
    m0 = lane < HEAD_DIM
    head_lanes = (m0, jnp.logical_not(m0))
    ii = lax.broadcasted_iota(I32, (NW, 2 * NW), 0)
    jj = lax.broadcasted_iota(I32, (NW, 2 * NW), 1)
    off = ii + NW - jj
    band = (off >= 0) & (off <= NW)
    off_f = off.astype(F32)
    ones_h = [jnp.broadcast_to(jnp.where(hl, 1.0, 0.0).astype(BF16), (2 * NW, LANES)) for hl in head_lanes]

    for br, (_, dil) in enumerate(DIL_BRANCHES):
        nblk = TS // (NW * dil)
        shift = int(math.log2(nblk))
        bias = [jnp.where(band, -(sl_ref[2 * hp + hh] * float(dil)) * off_f, -jnp.inf) for hh in range(2)]

        def one_block(bs, dil=dil, nblk=nblk, shift=shift, br=br, bias=bias):
            r = lax.shift_right_logical(bs, shift)
            blk = bs & (nblk - 1)
            qs = r + dil * NW * blk
            q_rows = pl.ds(qs, NW, stride=dil)
            own = pl.ds(cur + qs, NW, stride=dil)
            prev = pl.ds((cur + qs - dil * NW) & (2 * TS - 1), NW, stride=dil)
            qb = q_ref[q_rows, :]
            kb = jnp.concatenate([kk[prev, :], kk[own, :]], axis=0).astype(BF16)
            vb = jnp.concatenate([vv[prev, :], vv[own, :]], axis=0)
            first_key = jnp.where(jnp.logical_or(t > 0, blk > 0), 0, NW)
            started = jj >= first_key
            res = None
            mx = []
            for hh in range(2):
                qh = jnp.where(head_lanes[hh], qb, 0.0).astype(BF16)
                s = jnp.where(started, _dot_nt(qh, kb) + bias[hh], -jnp.inf)
                mh = jnp.max(s, axis=-1, keepdims=True)
                p = jnp.exp(s - mh).astype(BF16)
                vh = jnp.concatenate([jnp.where(head_lanes[hh], vb, 0.0).astype(BF16), ones_h[hh]], axis=1)
                part = _dot(p, vh)
                res = part if res is None else res + part
                mx.append(mh)
            m_br[br][q_rows, :] = jnp.where(m0, mx[0], mx[1])
            acc_br[br][q_rows, :] = res[:, :LANES]
            l_br[br][q_rows, :] = res[:, LANES:]

        def body(i, carry, one_block=one_block):
            for u in range(ATT_UNROLL):
                one_block(i * ATT_UNROLL + u)
            return carry

        lax.fori_loop(0, TS // NW // ATT_UNROLL, body, 0)

    rows_per_step = 256

    def merge(i, carry):
        rows = pl.ds(pl.multiple_of(i * rows_per_step, rows_per_step), rows_per_step)
        ms = [m[rows, :] for m in m_br]
        m_all = functools.reduce(jnp.maximum, ms)
        ws = [jnp.exp(m - m_all) for m in ms]
        num = sum(w * a[rows, :] for w, a in zip(ws, acc_br))
        den = sum(w * l[rows, :] for w, l in zip(ws, l_br))
        o_ref[rows, :] = (num / den).astype(BF16)
        return carry

    lax.fori_loop(0, TS // rows_per_step, merge, 0)


def _attention_prompt(slopes, aq, ak, av):
    b, s, _ = aq.shape
    ts = ATT_TILE
    blk = pl.BlockSpec((None, ts, LANES), lambda bi, hp, t, sl: (bi, t, hp))
    grid_spec = pltpu.PrefetchScalarGridSpec(
        num_scalar_prefetch=1,
        grid=(b, H_ATT // 2, s // ts),
        in_specs=[blk, blk, blk],
        out_specs=blk,
        scratch_shapes=[pltpu.VMEM((2 * ts, LANES), F32)] * 2
        + [pltpu.VMEM((ts, LANES), F32)] * (3 * len(DIL_BRANCHES)),
    )
    return pl.pallas_call(
        _att_kernel,
        out_shape=SDS((b, s, ATT_W), BF16),
        grid_spec=grid_spec,
        compiler_params=_params(("arbitrary", "arbitrary", "arbitrary")),
        name="attention_prompt",
    )(slopes, aq, ak, av)


def _sret_kernel(lg_ref, q_ref, k_ref, v_ref, rg_ref, gr_ref, st_ref, o_ref, sn_ref):
    q = q_ref[...].astype(F32)
    k = k_ref[...].astype(F32)
    v = v_ref[...].astype(F32)
    rg = rg_ref[...].astype(F32)
    gr = gr_ref[...]
    r_i = lax.broadcasted_iota(I32, (HEAD_DIM, HEAD_DIM), 0)
    c_i = lax.broadcasted_iota(I32, (HEAD_DIM, HEAD_DIM), 1)
    eye = r_i == c_i
    outs = []
    for h in range(H_RET):
        sl = slice(h * HEAD_DIM, (h + 1) * HEAD_DIM)
        qh, kh, vh = q[:, sl], k[:, sl], v[:, sl]
        gamma = jnp.exp(jnp.full((1, HEAD_DIM), lg_ref[h], F32))
        q_col = jnp.sum(jnp.where(eye, qh, 0.0), axis=1, keepdims=True)
        k_col = jnp.sum(jnp.where(eye, kh, 0.0), axis=1, keepdims=True)
        state = st_ref[h]
        o_inter = gamma * jnp.sum(q_col * state, axis=0, keepdims=True)
        o_intra = jnp.sum(qh * kh, axis=-1, keepdims=True) * vh
        sn_ref[h] = gamma * state + k_col * vh
        o = o_intra + o_inter
        mu = jnp.mean(o, axis=-1, keepdims=True)
        xc = o - mu
        var = jnp.mean(xc * xc, axis=-1, keepdims=True)
        outs.append(xc * lax.rsqrt(var + EPS) * gr[:, sl] * _silu(rg[:, sl]))
    o_ref[...] = jnp.concatenate(outs, axis=-1).astype(BF16)


def _retention_sample(lg, rq, rk, rv, rg, g_ret, state):
    db = rq.shape[0]
    row = pl.BlockSpec((None, 1, RET_W), lambda i: (i, 0, 0))
    st = pl.BlockSpec((None, H_RET, HEAD_DIM, HEAD_DIM), lambda i: (i, 0, 0, 0))
    r3 = lambda a: a.reshape(db, 1, RET_W)
    return pl.pallas_call(
        _sret_kernel,
        out_shape=[SDS((db, 1, RET_W), BF16), SDS(state.shape, F32)],
        grid=(db,),
        in_specs=[pl.BlockSpec(memory_space=pltpu.SMEM), row, row, row, row,
                  pl.BlockSpec((1, RET_W), lambda i: (0, 0)), st],
        out_specs=[row, st],
        compiler_params=_params(("arbitrary",)),
        name="retention_sample",
    )(lg, r3(rq), r3(rk), r3(rv), r3(rg), g_ret, state)


def _satt_kernel(sl_ref, q_ref, kn_ref, vn_ref, wk_ref, wv_ref, hm_ref, hmt_ref, o_ref):
    hp = pl.program_id(1)
    win = wk_ref.shape[0]
    q = q_ref[...]
    hm = hm_ref[...]
    hmt = hmt_ref[...]
    lane = lax.broadcasted_iota(I32, (1, LANES), 1)
    slope_v = jnp.where(lane == 0, sl_ref[2 * hp], jnp.where(lane == 1, sl_ref[2 * hp + 1], 0.0))

    def head_sums(x):
        hi, lo = _split(x)
        return _dot(hi, hm) + _dot(lo, hm)

    def expand(x):
        hi, lo = _split(x)
        return _dot(hi, hmt) + _dot(lo, hmt)

    s_new = head_sums(q * kn_ref[...])
    steps = float(NW) - lax.broadcasted_iota(I32, (NW, LANES), 0).astype(F32)
    scores, rows = [], []
    m = s_new
    for _, dil in DIL_BRANCHES:
        r = pl.ds(win - NW * dil, NW, stride=dil)
        s = head_sums(wk_ref[r, :] * q) - slope_v * (steps * float(dil))
        scores.append(s)
        rows.append(r)
        m = jnp.maximum(m, jnp.max(s, axis=0, keepdims=True))
    p_new = jnp.exp(s_new - m)
    nb = float(len(DIL_BRANCHES))
    den = nb * p_new
    num = nb * expand(p_new) * vn_ref[...]
    for s, r in zip(scores, rows):
        p = jnp.exp(s - m)
        den = den + jnp.sum(p, axis=0, keepdims=True)
        num = num + jnp.sum(expand(p) * wv_ref[r, :], axis=0, keepdims=True)
    o_ref[...] = (num / expand(den)).astype(BF16)


def _attention_sample(slopes, aq, ak, av, win_k, win_v, hm, hmt):
    db = aq.shape[0]
    win = win_k.shape[1]
    row = pl.BlockSpec((None, 1, LANES), lambda i, hp: (i, 0, hp))
    wblk = pl.BlockSpec((None, win, LANES), lambda i, hp: (i, 0, hp))
    r3 = lambda a: a.reshape(db, 1, ATT_W)
    return pl.pallas_call(
        _satt_kernel,
        out_shape=SDS((db, 1, ATT_W), BF16),
        grid=(db, H_ATT // 2),
        in_specs=[pl.BlockSpec(memory_space=pltpu.SMEM), row, row, row, wblk, wblk,
                  pl.BlockSpec(hm.shape, lambda i, hp: (0, 0)), pl.BlockSpec(hmt.shape, lambda i, hp: (0, 0))],
        out_specs=row,
        compiler_params=_params(("arbitrary", "arbitrary")),
        name="attention_sample",
    )(slopes, r3(aq), r3(ak), r3(av), win_k.reshape(db, win, ATT_W), win_v.reshape(db, win, ATT_W), hm, hmt)


def _outproj_kernel(x_ref, ret_ref, att_ref, wo_ref, gt1_ref, g2_ref, sc2_ref, sh2_ref, wqh_ref, wql_ref,
                    x1_ref, h2_ref, qp_ref):
    mix = _dot(ret_ref[...], wo_ref[0:RET_W, :]) + _dot(att_ref[...], wo_ref[RET_W:, :])
    x1 = x_ref[...] + gt1_ref[...] * mix
    x1_ref[...] = x1
    ms = jnp.mean(x1 * x1, axis=-1, keepdims=True)
    h2 = x1 * lax.rsqrt(ms + EPS) * g2_ref[...]
    h2 = h2 * (1.0 + sc2_ref[...]) + sh2_ref[...]
    h2_ref[...] = h2
    hi, lo = _split(h2)
    wqh = wqh_ref[...]
    qp_ref[...] = _dot(hi, wqh) + (_dot(lo, wqh) + _dot(hi, wql_ref[...]))


def _outproj(x2d, ret, att, wo_bf, gt1, g2, sc2, sh2, wq_hi, wq_lo, tb):
    n = x2d.shape[0]
    groups, rows, _ = gt1.shape
    tiles_per_group = n // tb // groups
    mod_spec = pl.BlockSpec((None, rows, D_MODEL), lambda i: (i // tiles_per_group, 0, 0))
    tok = lambda w: pl.BlockSpec((tb, w), lambda i: (i, 0))
    full = lambda a: pl.BlockSpec(a.shape, lambda i: (0,) * a.ndim)
    return pl.pallas_call(
        _outproj_kernel,
        out_shape=[SDS((n, D_MODEL), F32)] * 3,
        grid=(n // tb,),
        in_specs=[tok(D_MODEL), tok(RET_W), tok(ATT_W), full(wo_bf), mod_spec, full(g2), mod_spec, mod_spec,
                  full(wq_hi), full(wq_lo)],
        out_specs=[tok(D_MODEL)] * 3,
        compiler_params=_params(("arbitrary",)),
        name="outproj",
    )(x2d, ret, att, wo_bf, gt1, g2, sc2, sh2, wq_hi, wq_lo)


def _top16(sc, ids=None):
    if ids is None:
        ids = lax.broadcasted_iota(I32, sc.shape, 0)
    ids = ids.astype(F32)
    vals, idxs = [], []
    for _ in range(PEER_TOPK):
        m = jnp.max(sc, axis=0, keepdims=True)
        idx = jnp.min(jnp.where(sc == m, ids, jnp.inf), axis=0, keepdims=True)
        vals.append(m)
        idxs.append(idx)
        sc = jnp.where(ids == idx, -jnp.inf, sc)
    return jnp.concatenate(vals, axis=0), jnp.concatenate(idxs, axis=0).astype(I32)


def _pair_candidates(s1, s2):
    k = PEER_TOPK
    half = k // 2
    pieces = [s1[0:1, :] + s2]
    pieces += [s1[a:a + 1, :] + s2[0:half, :] for a in range(1, half)]
    pieces.append(s1[half:, :] + s2[0:1, :])
    cand = jnp.concatenate(pieces, axis=0)
    r = lax.broadcasted_iota(I32, cand.shape, 0)
    mid = r - k
    a_mid = 1 + lax.shift_right_logical(mid, 3)
    b_mid = mid & (half - 1)
    tail_start = k + half * (half - 1)
    pos = jnp.where(r < k, r, jnp.where(r < tail_start, a_mid * k + b_mid, (half + r - tail_start) * k))
    reachable = jnp.logical_or(jnp.logical_or(r < k, r >= tail_start), (a_mid + 1) * (b_mid + 1) <= k)
    return jnp.where(reachable, cand, -jnp.inf), pos


def _topk_kernel(q_ref, skh_ref, skl_ref, e_ref, g_ref, e_scr, g_scr):
    qh, ql = _split(q_ref[...])

    def scores(c):
        kh = skh_ref[c]
        return _dot_nt(kh, qh) + (_dot_nt(kh, ql) + _dot_nt(skl_ref[c], qh))

    s1, i1 = _top16(scores(0))
    s2, i2 = _top16(scores(1))
    top_s, pos = _top16(*_pair_candidates(s1, s2))
    a_sel = lax.shift_right_logical(pos, 4)
    b_sel = pos & (PEER_TOPK - 1)
    io = lax.broadcasted_iota(I32, i1.shape, 0)
    rows = []
    for r in range(PEER_TOPK):
        ia = jnp.sum(jnp.where(io == a_sel[r:r + 1, :], i1, 0), axis=0, keepdims=True)
        ib = jnp.sum(jnp.where(io == b_sel[r:r + 1, :], i2, 0), axis=0, keepdims=True)
        rows.append(ia * N_KEYS + ib)
    h = pl.program_id(1)
    slot_rows = pl.ds(pl.multiple_of(h * PEER_TOPK, PEER_TOPK), PEER_TOPK)
    e_scr[slot_rows, :] = jnp.concatenate(rows, axis=0) * ROW_SUB
    ex = jnp.exp(top_s - top_s[0:1, :])
    g_scr[slot_rows, :] = ex / jnp.sum(ex, axis=0, keepdims=True)

    @pl.when(h == PEER_HEADS - 1)
    def _():
        e_ref[...] = e_scr[...].T
        g_ref[...] = g_scr[...].T


def _peer_topk(qp, sk_hi, sk_lo, tt):
    n = qp.shape[0]
    out_blk = pl.BlockSpec((tt, PEER_SLOTS), lambda i, h: (i, 0))
    sk_blk = pl.BlockSpec((None, 2, N_KEYS, LANES), lambda i, h: (h, 0, 0, 0))
    return pl.pallas_call(
        _topk_kernel,
        out_shape=[SDS((n, PEER_SLOTS), I32), SDS((n, PEER_SLOTS), F32)],
        grid=(n // tt, PEER_HEADS),
        in_specs=[pl.BlockSpec((tt, LANES), lambda i, h: (i, h)), sk_blk, sk_blk],
        out_specs=[out_blk, out_blk],
        scratch_shapes=[pltpu.VMEM((PEER_SLOTS, tt), I32), pltpu.VMEM((PEER_SLOTS, tt), F32)],
        compiler_params=_params(("arbitrary", "arbitrary")),
        name="peer_topk",
    )(qp, sk_hi, sk_lo)


def _gather_rows(idx_ref, tab_ref, gbuf, t):
    for k in range(PEER_SLOTS):
        e = pl.multiple_of(idx_ref[t, k], ROW_SUB)
        gbuf[ROW_SUB * k:ROW_SUB * (k + 1), :] = tab_ref[pl.ds(e, ROW_SUB), :]


def _diag_mask():
    sub = lax.broadcasted_iota(I32, (SUBLANES, D_MODEL), 0)
    col = lax.broadcasted_iota(I32, (SUBLANES, D_MODEL), 1)
    return (col & (SUBLANES - 1)) == sub


def _for_each_token(tb, gbufs, one_token):
    def group(i, carry):
        for u, gbuf in enumerate(gbufs):
            one_token(i * len(gbufs) + u, gbuf)
        return carry

    lax.fori_loop(0, tb // len(gbufs), group, 0)


def _peer_u_kernel(idx_ref, tab_ref, h2_ref, g_ref, sel_ref, w_ref, *scratch):
    gbufs, (xbuf, rall) = scratch[:TOK_UNROLL], scratch[TOK_UNROLL:]
    tb = h2_ref.shape[0]
    for j in range(SUBLANES):
        xbuf[pl.ds(j, tb, stride=SUBLANES), :] = h2_ref[:, LANES * j:LANES * (j + 1)]
    diag = _diag_mask()

    def tok(t, gbuf):
        _gather_rows(idx_ref, tab_ref, gbuf, t)
        rows = pltpu.bitcast(gbuf[...], BF16)
        xt = xbuf[pl.ds(pl.multiple_of(t * SUBLANES, SUBLANES), SUBLANES), :].astype(BF16)
        r = _dot_nt(xt, rows)
        rall[pl.ds(t, 1), :] = jnp.sum(jnp.where(diag, r, 0.0), axis=0, keepdims=True)

    _for_each_token(tb, gbufs, tok)
    hi, lo = _split(rall[...])
    sel = sel_ref[...]
    a = _dot(hi, sel) + _dot(lo, sel)
    gelu = 0.5 * a * (1.0 + lax.erf(a * (2.0 ** -0.5)))
    w_ref[...] = g_ref[...] * gelu


def _peer_v_kernel(idx_ref, tab_ref, w_ref, x1_ref, gt2_ref, selt_ref, y_ref, *scratch):
    gbufs, (wrep, ybuf) = scratch[:TOK_UNROLL], scratch[TOK_UNROLL:]
    tb = x1_ref.shape[0]
    wrep[...] = _dot(w_ref[...].astype(BF16), selt_ref[...])
    diag = _diag_mask()

    def tok(t, gbuf):
        _gather_rows(idx_ref, tab_ref, gbuf, t)
        rows = pltpu.bitcast(gbuf[...], BF16)
        wt = jnp.where(diag, wrep[pl.ds(t, 1), :], 0.0).astype(BF16)
        ybuf[pl.ds(pl.multiple_of(t * SUBLANES, SUBLANES), SUBLANES), :] = _dot(wt, rows)

    _for_each_token(tb, gbufs, tok)
    for j in range(SUBLANES):
        cols = slice(LANES * j, LANES * (j + 1))
        y_ref[:, cols] = x1_ref[:, cols] + gt2_ref[:, cols] * ybuf[pl.ds(j, tb, stride=SUBLANES), :]


def _peer_u(idx, tab, h2, g, sel, tb):
    n = h2.shape[0]
    tok = lambda w: pl.BlockSpec((tb, w), lambda i: (i, 0))
    return pl.pallas_call(
        _peer_u_kernel,
        out_shape=SDS((n, PEER_SLOTS), F32),
        grid=(n // tb,),
        in_specs=[pl.BlockSpec((tb, PEER_SLOTS), lambda i: (i, 0), memory_space=pltpu.SMEM),
                  pl.BlockSpec(memory_space=pltpu.VMEM), tok(D_MODEL), tok(PEER_SLOTS),
                  pl.BlockSpec(sel.shape, lambda i: (0, 0))],
        out_specs=tok(PEER_SLOTS),
        scratch_shapes=[pltpu.VMEM((ROW_SUB * PEER_SLOTS, LANES), I32)] * TOK_UNROLL + [
                        pltpu.VMEM((tb * SUBLANES, LANES), F32),
                        pltpu.VMEM((tb, D_MODEL), F32)],
        compiler_params=_params(("arbitrary",)),
        name="peer_u",
    )(idx, tab, h2, g, sel)


def _peer_v(idx, tab, w, x1, gt2, selt, tb):
    n = x1.shape[0]
    groups, rows, _ = gt2.shape
    tiles_per_group = n // tb // groups
    tok = lambda wd: pl.BlockSpec((tb, wd), lambda i: (i, 0))
    return pl.pallas_call(
        _peer_v_kernel,
        out_shape=SDS((n, D_MODEL), F32),
        grid=(n // tb,),
        in_specs=[pl.BlockSpec((tb, PEER_SLOTS), lambda i: (i, 0), memory_space=pltpu.SMEM),
                  pl.BlockSpec(memory_space=pltpu.VMEM), tok(PEER_SLOTS), tok(D_MODEL),
                  pl.BlockSpec((None, rows, D_MODEL), lambda i: (i // tiles_per_group, 0, 0)),
                  pl.BlockSpec(selt.shape, lambda i: (0, 0))],
        out_specs=tok(D_MODEL),
        scratch_shapes=[pltpu.VMEM((ROW_SUB * PEER_SLOTS, LANES), I32)] * TOK_UNROLL + [
                        pltpu.VMEM((tb, D_MODEL), F32),
                        pltpu.VMEM((tb * SUBLANES, LANES), F32)],
        compiler_params=_params(("arbitrary",)),
        name="peer_v",
    )(idx, tab, w, x1, gt2, selt)


def _pack_table(tab):
    e = tab.shape[0]
    bits = lax.bitcast_convert_type(tab.astype(BF16), jnp.uint16).astype(jnp.uint32)
    bits = bits.reshape(e, ROW_SUB, 2, LANES)
    word = bits[:, :, 0, :] | (bits[:, :, 1, :] << 16)
    return lax.bitcast_convert_type(word, I32).reshape(e * ROW_SUB, LANES)


def _block_avg(width):
    i = np.arange(width) // HEAD_DIM
    return jnp.asarray((i[:, None] == i[None, :]).astype(np.float32) / HEAD_DIM, BF16)


def _slot_select():
    col = np.arange(D_MODEL) // SUBLANES
    return jnp.asarray((col[:, None] == np.arange(PEER_SLOTS)[None, :]).astype(np.float32), BF16)


def _head_indicator():
    d = np.arange(LANES) // HEAD_DIM
    return jnp.asarray((d[:, None] == np.arange(LANES)[None, :]).astype(np.float32), BF16)


def _ret_log_decay():
    return jnp.log1p(-jnp.exp2(-5.0 - jnp.arange(H_RET, dtype=F32)))


def _alibi_slopes():
    return jnp.exp2(-8.0 * (jnp.arange(H_ATT, dtype=F32) + 1.0) / H_ATT)


def _pad_sub_keys(sub_keys):
    z = jnp.zeros_like(sub_keys[:, 0])
    return jnp.stack([jnp.concatenate([sub_keys[:, 0], z], axis=-1),
                      jnp.concatenate([z, sub_keys[:, 1]], axis=-1)], axis=1)


def _group_w_out(w):
    w4 = w.reshape(H_RET, 2, HEAD_DIM, D_MODEL)
    return jnp.concatenate([w4[:, 0].reshape(RET_W, D_MODEL), w4[:, 1].reshape(ATT_W, D_MODEL)], axis=0)


def _split_w(w):
    hi = w.astype(BF16)
    return hi, (w - hi.astype(F32)).astype(BF16)


def _token_tile(n, pref):
    return pref if n % pref == 0 else n


def _peer(h2, qp, x1, gt2, consts, weights):
    n = h2.shape[0]
    idx, gate = _peer_topk(qp, weights["sk_hi"], weights["sk_lo"], _token_tile(n, 512))
    tb = _token_tile(n, 256)
    w = _peer_u(idx, weights["u_tab"], h2, gate, consts["sel"], tb)
    return _peer_v(idx, weights["v_tab"], w, x1, gt2, consts["selt"], tb)


def _mix_and_peer(x2d, ret, att, mods, consts, weights):
    sh1, sc1, gt1, sh2, sc2, gt2 = mods
    n = x2d.shape[0]
    x1, h2, qp = _outproj(x2d, ret, att, weights["w_out"], gt1, weights["g_norm2"], sc2, sh2,
                          weights["wq_hi"], weights["wq_lo"], _token_tile(n, 512))
    return _peer(h2, qp, x1, gt2, consts, weights)


def _project(x2d, mods, consts, weights):
    sh1, sc1 = mods[0], mods[1]
    n = x2d.shape[0]
    return _inproj(x2d, sc1, sh1, weights["g_norm1"], weights["w_in"], weights["g_qn"], weights["g_kn"],
                   consts["avg_att"], _token_tile(n, 512))


def kernel(x_prompt, x_sample, c_prompt, c_sample, state_ret, cache_win_k, cache_win_v, w_ada, b_ada, g_norm1, w_in, g_qn, g_kn, g_ret, w_out, g_norm2, w_pq, sub_keys, peer_u, peer_v):
    b, s, d = x_prompt.shape
    db, ds_, _ = x_sample.shape
    depth = w_ada.shape[0]
    assert d == D_MODEL and ds_ == 1 and s % ATT_TILE == 0
    win = cache_win_k.shape[2]
    assert win == MAX_WINDOW

    consts = {
        "avg_att": _block_avg(ATT_W),
        "avg_pair": _block_avg(LANES),
        "sel": _slot_select(),
        "selt": _slot_select().T,
        "hm": _head_indicator(),
        "hmt": _head_indicator().T,
        "lg": _ret_log_decay(),
        "slopes": _alibi_slopes(),
    }

    y_p = x_prompt.reshape(b * s, d)
    y_s = x_sample.reshape(db, d)
    outs = {k: [] for k in ("ret_p", "wk_p", "wv_p", "ret_s", "wk_s", "wv_s")}
    for l in range(depth):
        wq_hi, wq_lo = _split_w(w_pq[l])
        sk_hi, sk_lo = _split_w(_pad_sub_keys(sub_keys[l]))
        weights = {
            "g_norm1": g_norm1[l].reshape(1, d), "g_norm2": g_norm2[l].reshape(1, d),
            "w_in": w_in[l].astype(BF16), "w_out": _group_w_out(w_out[l]).astype(BF16),
            "g_qn": g_qn[l].reshape(1, ATT_W), "g_kn": g_kn[l].reshape(1, ATT_W),
            "wq_hi": wq_hi, "wq_lo": wq_lo, "sk_hi": sk_hi, "sk_lo": sk_lo,
            "u_tab": _pack_table(peer_u[l]), "v_tab": _pack_table(peer_v[l]),
        }
        gr = g_ret[l].reshape(1, RET_W)

        mod = _adaln(jnp.concatenate([c_prompt, c_sample], axis=0), w_ada[l], b_ada[l])
        mod6 = jnp.split(mod, 6, axis=-1)
        mods_p = [m[:b].reshape(b, 1, d) for m in mod6]
        mods_s = [m[b:].reshape(1, db, d) for m in mod6]

        rq, rk, rv, rg, aq, ak, av = _project(y_p, mods_p, consts, weights)
        seq = lambda a: a.reshape(b, s, a.shape[-1])
        ret, ret_state = _retention_prompt(consts["lg"], seq(rq), seq(rk), seq(rv), seq(rg), gr,
                                           consts["avg_pair"], 1024)
        att = _attention_prompt(consts["slopes"], seq(aq), seq(ak), seq(av))
        y_p = _mix_and_peer(y_p, ret.reshape(b * s, RET_W), att.reshape(b * s, ATT_W), mods_p, consts, weights)
        keep = min(MAX_WINDOW, s)
        outs["ret_p"].append(ret_state)
        outs["wk_p"].append(seq(ak)[:, s - keep:].reshape(b, keep, H_ATT, HEAD_DIM))
        outs["wv_p"].append(seq(av)[:, s - keep:].reshape(b, keep, H_ATT, HEAD_DIM))

        rq, rk, rv, rg, aq, ak, av = _project(y_s, mods_s, consts, weights)
        ret, new_state = _retention_sample(consts["lg"], rq, rk, rv, rg, gr, state_ret[l])
        att = _attention_sample(consts["slopes"], aq, ak, av, cache_win_k[l], cache_win_v[l],
                                consts["hm"], consts["hmt"])
        y_s = _mix_and_peer(y_s, ret.reshape(db, RET_W), att.reshape(db, ATT_W), mods_s, consts, weights)
        outs["ret_s"].append(new_state)
        outs["wk_s"].append(ak.reshape(db, 1, H_ATT, HEAD_DIM))
        outs["wv_s"].append(av.reshape(db, 1, H_ATT, HEAD_DIM))

    st = lambda k: jnp.stack(outs[k])
    return (y_p.reshape(b, s, d), y_s.reshape(db, 1, d), st("ret_p"), st("wk_p"), st("wv_p"),
            st("ret_s"), st("wk_s"), st("wv_s"))
```

```python
import functools
import math

import numpy as np
import jax
import jax.numpy as jnp
from jax import lax
from jax.experimental import pallas as pl
from jax.experimental.pallas import tpu as pltpu

F32 = jnp.float32
BF16 = jnp.bfloat16
I32 = jnp.int32

D_MODEL = 1024
HEAD_DIM = 64
RET_W = 512
ATT_W = 512
H_RET = 8
H_ATT = 8
N_PROJ = 4 * RET_W + 3 * ATT_W
RET_CHUNK = 128
DIL_BRANCHES = ((128, 1), (512, 4), (2048, 16))
NW = 128
MAX_WINDOW = 2048
N_KEYS = 128
PEER_HEADS = 8
PEER_TOPK = 16
PEER_SLOTS = PEER_HEADS * PEER_TOPK
EPS = 1e-6

LANES = 128
SUBLANES = 8
ROW_WORDS = D_MODEL // 2
ROW_SUB = ROW_WORDS // LANES
VMEM_LIMIT = 56 * 1024 * 1024
RET_UNROLL = 4
TOK_UNROLL = 8

SDS = jax.ShapeDtypeStruct


def _dot(a, b):
    return jnp.dot(a, b, preferred_element_type=F32)


def _dot_nt(a, b):
    return lax.dot_general(a, b, (((1,), (1,)), ((), ())), preferred_element_type=F32)


def _dot_tn(a, b):
    return lax.dot_general(a, b, (((0,), (0,)), ((), ())), preferred_element_type=F32)


def _split(x):
    hi = x.astype(BF16)
    lo = (x - hi.astype(F32)).astype(BF16)
    return hi, lo


def _silu(x):
    return x * jax.nn.sigmoid(x)


def _params(sem, vmem=VMEM_LIMIT):
    return pltpu.CompilerParams(dimension_semantics=sem, vmem_limit_bytes=vmem)


def _adaln_kernel(c_ref, w_ref, b_ref, o_ref):
    s = _silu(c_ref[...])
    o_ref[...] = jnp.dot(s, w_ref[...], preferred_element_type=F32,
                         precision=lax.Precision.HIGHEST) + b_ref[...]


def _adaln(c, w_ada, b_ada):
    rows = c.shape[0]
    tn = 512
    return pl.pallas_call(
        _adaln_kernel,
        out_shape=SDS((rows, 6 * D_MODEL), F32),
        grid=(6 * D_MODEL // tn,),
        in_specs=[pl.BlockSpec((rows, D_MODEL), lambda j: (0, 0)),
                  pl.BlockSpec((D_MODEL, tn), lambda j: (0, j)),
                  pl.BlockSpec((1, tn), lambda j: (0, j))],
        out_specs=pl.BlockSpec((rows, tn), lambda j: (0, j)),
        compiler_params=_params(("arbitrary",)),
        name="adaln",
    )(c, w_ada, b_ada.reshape(1, -1))


def _inproj_kernel(x_ref, sc_ref, sh_ref, g1_ref, w_ref, gq_ref, gk_ref, bd_ref,
                   rq_ref, rk_ref, rv_ref, rg_ref, aq_ref, ak_ref, av_ref):
    x = x_ref[...]
    ms = jnp.mean(x * x, axis=-1, keepdims=True)
    h = x * lax.rsqrt(ms + EPS) * g1_ref[...]
    h = h * (1.0 + sc_ref[...]) + sh_ref[...]
    p = _dot(h.astype(BF16), w_ref[...])
    scale = HEAD_DIM ** -0.5
    rq_ref[...] = p[:, 0:RET_W].astype(BF16)
    rk_ref[...] = (p[:, RET_W:2 * RET_W] * scale).astype(BF16)
    rv_ref[...] = p[:, 2 * RET_W:3 * RET_W].astype(BF16)
    rg_ref[...] = p[:, 3 * RET_W:4 * RET_W].astype(BF16)
    o = 4 * RET_W

    def head_norm(a, g):
        msq = _dot((a * a).astype(BF16), bd_ref[...])
        return a * lax.rsqrt(msq + EPS) * g

    aq_ref[...] = head_norm(p[:, o:o + ATT_W], gq_ref[...]) * scale
    ak_ref[...] = head_norm(p[:, o + ATT_W:o + 2 * ATT_W], gk_ref[...])
    av_ref[...] = p[:, o + 2 * ATT_W:o + 3 * ATT_W]


def _inproj(x2d, sc, sh, g1, w_in_bf, gq, gk, bd, tb):
    n = x2d.shape[0]
    groups, rows, _ = sc.shape
    tiles_per_group = n // tb // groups
    mod_spec = pl.BlockSpec((None, rows, D_MODEL), lambda i: (i // tiles_per_group, 0, 0))
    tok = lambda w: pl.BlockSpec((tb, w), lambda i: (i, 0))
    full = lambda a: pl.BlockSpec(a.shape, lambda i: (0,) * a.ndim)
    outs = [SDS((n, RET_W), BF16)] * 4 + [SDS((n, ATT_W), F32)] * 3
    return pl.pallas_call(
        _inproj_kernel,
        out_shape=outs,
        grid=(n // tb,),
        in_specs=[tok(D_MODEL), mod_spec, mod_spec, full(g1), full(w_in_bf), full(gq), full(gk), full(bd)],
        out_specs=[tok(RET_W)] * 4 + [tok(ATT_W)] * 3,
        compiler_params=_params(("arbitrary",)),
        name="inproj",
    )(x2d, sc, sh, g1, w_in_bf, gq, gk, bd)


def _ret_kernel(lg_ref, q_ref, k_ref, v_ref, rg_ref, gr_ref, o_ref, st_ref, s_scr, *, nchunk):
    hp = pl.program_id(1)
    t = pl.program_id(2)
    C = RET_CHUNK
    lane = lax.broadcasted_iota(I32, (1, LANES), 1)
    m0 = lane < HEAD_DIM
    lg0 = lg_ref[2 * hp]
    lg1 = lg_ref[2 * hp + 1]
    lgv = jnp.where(m0, lg0, lg1)

    @pl.when(t == 0)
    def _():
        s_scr[...] = jnp.zeros_like(s_scr)

    ii = lax.broadcasted_iota(I32, (C, C), 0)
    jj = lax.broadcasted_iota(I32, (C, C), 1)
    diff = ii - jj
    dpos = jnp.maximum(diff, 0).astype(F32)
    dec01 = jnp.concatenate([jnp.where(diff >= 0, jnp.exp(dpos * lg0), 0.0),
                             jnp.where(diff >= 0, jnp.exp(dpos * lg1), 0.0)], axis=0)

    def head_mean(x):
        s0 = jnp.sum(jnp.where(m0, x, 0.0), axis=-1, keepdims=True)
        s1 = jnp.sum(jnp.where(m0, 0.0, x), axis=-1, keepdims=True)
        return jnp.where(m0, s0, s1) * (1.0 / HEAD_DIM)
    ri = lax.broadcasted_iota(I32, (C, LANES), 0).astype(F32)
    w_start = jnp.exp((ri + 1.0) * lgv)
    w_end = jnp.exp((float(C) - 1.0 - ri) * lgv)
    chunk_decay = jnp.exp(float(C) * lgv)
    rr = lax.broadcasted_iota(I32, (LANES, LANES), 0)
    cc = lax.broadcasted_iota(I32, (LANES, LANES), 1)
    same_head = (rr < HEAD_DIM) == (cc < HEAD_DIM)
    gr = gr_ref[...]

    def chunk(c, state):
        rows = pl.ds(pl.multiple_of(c * C, C), C)
        q = q_ref[rows, :]
        k = k_ref[rows, :]
        v = v_ref[rows, :]
        qf = q.astype(F32)
        kf = k.astype(F32)
        q01 = jnp.concatenate([jnp.where(m0, qf, 0.0), jnp.where(m0, 0.0, qf)], axis=0).astype(BF16)
        p01 = (_dot_nt(q01, k) * dec01).astype(BF16)
        pv = _dot(p01, v)
        o_intra = jnp.where(m0, pv[0:C, :], pv[C:, :])
        o_inter = _dot((qf * w_start).astype(BF16), state.astype(BF16))
        o = o_intra + o_inter
        kv = _dot_tn((kf * w_end).astype(BF16), v)
        mu = head_mean(o)
        xc = o - mu
        var = head_mean(xc * xc)
        y = xc * lax.rsqrt(var + EPS) * gr
        y = y * _silu(rg_ref[rows, :].astype(F32))
        o_ref[rows, :] = y.astype(BF16)
        return chunk_decay * state + jnp.where(same_head, kv, 0.0)

    def chunk_group(i, state):
        for u in range(RET_UNROLL):
            state = chunk(i * RET_UNROLL + u, state)
        return state

    s_scr[...] = lax.fori_loop(0, nchunk // RET_UNROLL, chunk_group, s_scr[...])

    @pl.when(t == pl.num_programs(2) - 1)
    def _():
        state = s_scr[...]
        st_ref[0] = state[0:HEAD_DIM, 0:HEAD_DIM]
        st_ref[1] = state[HEAD_DIM:, HEAD_DIM:]


def _retention_prompt(lg, rq, rk, rv, rg, g_ret, cb):
    b, s, _ = rq.shape
    nchunk = cb // RET_CHUNK
    blk = pl.BlockSpec((None, cb, LANES), lambda bi, hp, t, lg_: (bi, t, hp))
    grid_spec = pltpu.PrefetchScalarGridSpec(
        num_scalar_prefetch=1,
        grid=(b, H_RET // 2, s // cb),
        in_specs=[blk, blk, blk, blk,
                  pl.BlockSpec((1, LANES), lambda bi, hp, t, lg_: (0, hp))],
        out_specs=[blk, pl.BlockSpec((None, 2, HEAD_DIM, HEAD_DIM), lambda bi, hp, t, lg_: (bi, hp, 0, 0))],
        scratch_shapes=[pltpu.VMEM((LANES, LANES), F32)],
    )
    return pl.pallas_call(
        functools.partial(_ret_kernel, nchunk=nchunk),
        out_shape=[SDS((b, s, RET_W), BF16), SDS((b, H_RET, HEAD_DIM, HEAD_DIM), F32)],
        grid_spec=grid_spec,
        compiler_params=_params(("arbitrary", "arbitrary", "arbitrary")),
        name="retention_prompt",
    )(lg, rq, rk, rv, rg, g_ret)


ATT_TILE = MAX_WINDOW


ATT_UNROLL = 4


def _att_kernel(sl_ref, q_ref, k_ref, v_ref, o_ref, kk, vv, *stats):
    hp = pl.program_id(1)
    t = pl.program_id(2)
    TS = ATT_TILE
    nbr = len(DIL_BRANCHES)
    m_br, l_br, acc_br = stats[:nbr], stats[nbr:2 * nbr], stats[2 * nbr:]
    cur = pl.multiple_of((t & 1) * TS, TS)

    @pl.when(t == 0)
    def _():
        kk[TS:2 * TS, :] = jnp.zeros((TS, LANES), F32)
        vv[TS:2 * TS, :] = jnp.zeros((TS, LANES), F32)

    kk[pl.ds(cur, TS), :] = k_ref[...]
    vv[pl.ds(cur, TS), :] = v_ref[...]

    lane = lax.broadcasted_iota(I32, (1, LANES), 1)
    m0 = lane < HEAD_DIM
    head_lanes = (m0, jnp.logical_not(m0))
    ii = lax.broadcasted_iota(I32, (NW, 2 * NW), 0)
    jj = lax.broadcasted_iota(I32, (NW, 2 * NW), 1)
    off = ii + NW - jj
    band = (off >= 0) & (off <= NW)
    off_f = off.astype(F32)
    ones_h = [jnp.broadcast_to(jnp.where(hl, 1.0, 0.0).astype(BF16), (2 * NW, LANES)) for hl in head_lanes]

    for br, (_, dil) in enumerate(DIL_BRANCHES):
        nblk = TS // (NW * dil)
        shift = int(math.log2(nblk))
        bias = [jnp.where(band, -(sl_ref[2 * hp + hh] * float(dil)) * off_f, -jnp.inf) for hh in range(2)]

        def one_block(bs, dil=dil, nblk=nblk, shift=shift, br=br, bias=bias):
            r = lax.shift_right_logical(bs, shift)
            blk = bs & (nblk - 1)
            qs = r + dil * NW * blk
            q_rows = pl.ds(qs, NW, stride=dil)
            own = pl.ds(cur + qs, NW, stride=dil)
            prev = pl.ds((cur + qs - dil * NW) & (2 * TS - 1), NW, stride=dil)
            qb = q_ref[q_rows, :]
            kb = jnp.concatenate([kk[prev, :], kk[own, :]], axis=0).astype(BF16)
            vb = jnp.concatenate([vv[prev, :], vv[own, :]], axis=0)
            first_key = jnp.where(jnp.logical_or(t > 0, blk > 0), 0, NW)
            started = jj >= first_key
            res = None
            mx = []
            for hh in range(2):
                qh = jnp.where(head_lanes[hh], qb, 0.0).astype(BF16)
                s = jnp.where(started, _dot_nt(qh, kb) + bias[hh], -jnp.inf)
                mh = jnp.max(s, axis=-1, keepdims=True)
                p = jnp.exp(s - mh).astype(BF16)
                vh = jnp.concatenate([jnp.where(head_lanes[hh], vb, 0.0).astype(BF16), ones_h[hh]], axis=1)
                part = _dot(p, vh)
                res = part if res is None else res + part
                mx.append(mh)
            m_br[br][q_rows, :] = jnp.where(m0, mx[0], mx[1])
            acc_br[br][q_rows, :] = res[:, :LANES]
            l_br[br][q_rows, :] = res[:, LANES:]

        def body(i, carry, one_block=one_block):
            for u in range(ATT_UNROLL):
                one_block(i * ATT_UNROLL + u)
            return carry

        lax.fori_loop(0, TS // NW // ATT_UNROLL, body, 0)

    rows_per_step = 256

    def merge(i, carry):
        rows = pl.ds(pl.multiple_of(i * rows_per_step, rows_per_step), rows_per_step)
        ms = [m[rows, :] for m in m_br]
        m_all = functools.reduce(jnp.maximum, ms)
        ws = [jnp.exp(m - m_all) for m in ms]
        num = sum(w * a[rows, :] for w, a in zip(ws, acc_br))
        den = sum(w * l[rows, :] for w, l in zip(ws, l_br))
        o_ref[rows, :] = (num / den).astype(BF16)
        return carry

    lax.fori_loop(0, TS // rows_per_step, merge, 0)


def _attention_prompt(slopes, aq, ak, av):
    b, s, _ = aq.shape
    ts = ATT_TILE
    blk = pl.BlockSpec((None, ts, LANES), lambda bi, hp, t, sl: (bi, t, hp))
    grid_spec = pltpu.PrefetchScalarGridSpec(
        num_scalar_prefetch=1,
        grid=(b, H_ATT // 2, s // ts),
        in_specs=[blk, blk, blk],
        out_specs=blk,
        scratch_shapes=[pltpu.VMEM((2 * ts, LANES), F32)] * 2
        + [pltpu.VMEM((ts, LANES), F32)] * (3 * len(DIL_BRANCHES)),
    )
    return pl.pallas_call(
        _att_kernel,
        out_shape=SDS((b, s, ATT_W), BF16),
        grid_spec=grid_spec,
        compiler_params=_params(("arbitrary", "arbitrary", "arbitrary")),
        name="attention_prompt",
    )(slopes, aq, ak, av)


def _sret_kernel(lg_ref, q_ref, k_ref, v_ref, rg_ref, gr_ref, st_ref, o_ref, sn_ref):
    q = q_ref[...].astype(F32)
    k = k_ref[...].astype(F32)
    v = v_ref[...].astype(F32)
    rg = rg_ref[...].astype(F32)
    gr = gr_ref[...]
    r_i = lax.broadcasted_iota(I32, (HEAD_DIM, HEAD_DIM), 0)
    c_i = lax.broadcasted_iota(I32, (HEAD_DIM, HEAD_DIM), 1)
    eye = r_i == c_i
    outs = []
    for h in range(H_RET):
        sl = slice(h * HEAD_DIM, (h + 1) * HEAD_DIM)
        qh, kh, vh = q[:, sl], k[:, sl], v[:, sl]
        gamma = jnp.exp(jnp.full((1, HEAD_DIM), lg_ref[h], F32))
        q_col = jnp.sum(jnp.where(eye, qh, 0.0), axis=1, keepdims=True)
        k_col = jnp.sum(jnp.where(eye, kh, 0.0), axis=1, keepdims=True)
        state = st_ref[h]
        o_inter = gamma * jnp.sum(q_col * state, axis=0, keepdims=True)
        o_intra = jnp.sum(qh * kh, axis=-1, keepdims=True) * vh
        sn_ref[h] = gamma * state + k_col * vh
        o = o_intra + o_inter
        mu = jnp.mean(o, axis=-1, keepdims=True)
        xc = o - mu
        var = jnp.mean(xc * xc, axis=-1, keepdims=True)
        outs.append(xc * lax.rsqrt(var + EPS) * gr[:, sl] * _silu(rg[:, sl]))
    o_ref[...] = jnp.concatenate(outs, axis=-1).astype(BF16)


def _retention_sample(lg, rq, rk, rv, rg, g_ret, state):
    db = rq.shape[0]
    row = pl.BlockSpec((None, 1, RET_W), lambda i: (i, 0, 0))
    st = pl.BlockSpec((None, H_RET, HEAD_DIM, HEAD_DIM), lambda i: (i, 0, 0, 0))
    r3 = lambda a: a.reshape(db, 1, RET_W)
    return pl.pallas_call(
        _sret_kernel,
        out_shape=[SDS((db, 1, RET_W), BF16), SDS(state.shape, F32)],
        grid=(db,),
        in_specs=[pl.BlockSpec(memory_space=pltpu.SMEM), row, row, row, row,
                  pl.BlockSpec((1, RET_W), lambda i: (0, 0)), st],
        out_specs=[row, st],
        compiler_params=_params(("arbitrary",)),
        name="retention_sample",
    )(lg, r3(rq), r3(rk), r3(rv), r3(rg), g_ret, state)


def _satt_kernel(sl_ref, q_ref, kn_ref, vn_ref, *refs):
    nbr = len(DIL_BRANCHES)
    k_refs, v_refs, o_ref = refs[:nbr], refs[nbr:2 * nbr], refs[2 * nbr]
    q = q_ref[...]
    sub = lax.broadcasted_iota(I32, (H_ATT, 1), 0)
    slope = jnp.zeros((H_ATT, 1), F32)
    for h in range(H_ATT):
        slope = jnp.where(sub == h, sl_ref[h], slope)
    steps = float(NW) - lax.broadcasted_iota(I32, (NW, 1, 1), 0).astype(F32)
    s_new = jnp.sum(q * kn_ref[...], axis=-1, keepdims=True)
    scores = []
    m = s_new
    for (_, dil), k_ref in zip(DIL_BRANCHES, k_refs):
        s = jnp.sum(k_ref[...] * q, axis=-1, keepdims=True) - slope * (steps * float(dil))
        scores.append(s)
        m = jnp.maximum(m, jnp.max(s, axis=0))
    p_new = jnp.exp(s_new - m)
    den = float(nbr) * p_new
    num = float(nbr) * p_new * vn_ref[...]
    for s, v_ref in zip(scores, v_refs):
        p = jnp.exp(s - m)
        den = den + jnp.sum(p, axis=0)
        num = num + jnp.sum(p * v_ref[...], axis=0)
    o_ref[...] = num / den


def _attention_sample(slopes, aq, ak, av, win_k, win_v):
    db = aq.shape[0]
    win = win_k.shape[1]
    row = pl.BlockSpec((None, H_ATT, HEAD_DIM), lambda i: (i, 0, 0))
    heads = lambda a: a.reshape(db, H_ATT, HEAD_DIM)
    views, specs = [], []
    for cache in (win_k, win_v):
        for _, dil in DIL_BRANCHES:
            views.append(cache.reshape(db, win // dil, dil, H_ATT, HEAD_DIM))
            last = win // dil // NW - 1
            specs.append(pl.BlockSpec((None, NW, None, H_ATT, HEAD_DIM),
                                      lambda i, last=last: (i, last, 0, 0, 0)))
    out = pl.pallas_call(
        _satt_kernel,
        out_shape=SDS((db, H_ATT, HEAD_DIM), F32),
        grid=(db,),
        in_specs=[pl.BlockSpec(memory_space=pltpu.SMEM), row, row, row] + specs,
        out_specs=row,
        compiler_params=_params(("arbitrary",)),
        name="attention_sample",
    )(slopes, heads(aq), heads(ak), heads(av), *views)
    return out.reshape(db, ATT_W).astype(BF16)


def _outproj_kernel(x_ref, ret_ref, att_ref, wo_ref, gt1_ref, g2_ref, sc2_ref, sh2_ref, wqh_ref, wql_ref,
                    x1_ref, h2_ref, qp_ref):
    mix = _dot(ret_ref[...], wo_ref[0:RET_W, :]) + _dot(att_ref[...], wo_ref[RET_W:, :])
    x1 = x_ref[...] + gt1_ref[...] * mix
    x1_ref[...] = x1
    ms = jnp.mean(x1 * x1, axis=-1, keepdims=True)
    h2 = x1 * lax.rsqrt(ms + EPS) * g2_ref[...]
    h2 = h2 * (1.0 + sc2_ref[...]) + sh2_ref[...]
    h2_ref[...] = h2
    hi, lo = _split(h2)
    wqh = wqh_ref[...]
    qp_ref[...] = _dot(hi, wqh) + (_dot(lo, wqh) + _dot(hi, wql_ref[...]))


def _outproj(x2d, ret, att, wo_bf, gt1, g2, sc2, sh2, wq_hi, wq_lo, tb):
    n = x2d.shape[0]
    groups, rows, _ = gt1.shape
    tiles_per_group = n // tb // groups
    mod_spec = pl.BlockSpec((None, rows, D_MODEL), lambda i: (i // tiles_per_group, 0, 0))
    tok = lambda w: pl.BlockSpec((tb, w), lambda i: (i, 0))
    full = lambda a: pl.BlockSpec(a.shape, lambda i: (0,) * a.ndim)
    return pl.pallas_call(
        _outproj_kernel,
        out_shape=[SDS((n, D_MODEL), F32)] * 3,
        grid=(n // tb,),
        in_specs=[tok(D_MODEL), tok(RET_W), tok(ATT_W), full(wo_bf), mod_spec, full(g2), mod_spec, mod_spec,
                  full(wq_hi), full(wq_lo)],
        out_specs=[tok(D_MODEL)] * 3,
        compiler_params=_params(("arbitrary",)),
        name="outproj",
    )(x2d, ret, att, wo_bf, gt1, g2, sc2, sh2, wq_hi, wq_lo)


def _top16(sc, ids=None):
    if ids is None:
        ids = lax.broadcasted_iota(I32, sc.shape, 0)
    ids = ids.astype(F32)
    vals, idxs = [], []
    for _ in range(PEER_TOPK):
        m = jnp.max(sc, axis=0, keepdims=True)
        idx = jnp.min(jnp.where(sc == m, ids, jnp.inf), axis=0, keepdims=True)
        vals.append(m)
        idxs.append(idx)
        sc = jnp.where(ids == idx, -jnp.inf, sc)
    return jnp.concatenate(vals, axis=0), jnp.concatenate(idxs, axis=0).astype(I32)


def _pair_candidates(s1, s2):
    k = PEER_TOPK
    half = k // 2
    pieces = [s1[0:1, :] + s2]
    pieces += [s1[a:a + 1, :] + s2[0:half, :] for a in range(1, half)]
    pieces.append(s1[half:, :] + s2[0:1, :])
    cand = jnp.concatenate(pieces, axis=0)
    r = lax.broadcasted_iota(I32, cand.shape, 0)
    mid = r - k
    a_mid = 1 + lax.shift_right_logical(mid, 3)
    b_mid = mid & (half - 1)
    tail_start = k + half * (half - 1)
    pos = jnp.where(r < k, r, jnp.where(r < tail_start, a_mid * k + b_mid, (half + r - tail_start) * k))
    reachable = jnp.logical_or(jnp.logical_or(r < k, r >= tail_start), (a_mid + 1) * (b_mid + 1) <= k)
    return jnp.where(reachable, cand, -jnp.inf), pos


def _topk_kernel(q_ref, skh_ref, skl_ref, e_ref, g_ref, e_scr, g_scr):
    qh, ql = _split(q_ref[...])

    def scores(c):
        kh = skh_ref[c]
        return _dot_nt(kh, qh) + (_dot_nt(kh, ql) + _dot_nt(skl_ref[c], qh))

    s1, i1 = _top16(scores(0))
    s2, i2 = _top16(scores(1))
    top_s, pos = _top16(*_pair_candidates(s1, s2))
    a_sel = lax.shift_right_logical(pos, 4)
    b_sel = pos & (PEER_TOPK - 1)
    io = lax.broadcasted_iota(I32, i1.shape, 0)
    rows = []
    for r in range(PEER_TOPK):
        ia = jnp.sum(jnp.where(io == a_sel[r:r + 1, :], i1, 0), axis=0, keepdims=True)
        ib = jnp.sum(jnp.where(io == b_sel[r:r + 1, :], i2, 0), axis=0, keepdims=True)
        rows.append(ia * N_KEYS + ib)
    h = pl.program_id(1)
    slot_rows = pl.ds(pl.multiple_of(h * PEER_TOPK, PEER_TOPK), PEER_TOPK)
    e_scr[slot_rows, :] = jnp.concatenate(rows, axis=0) * ROW_SUB
    ex = jnp.exp(top_s - top_s[0:1, :])
    g_scr[slot_rows, :] = ex / jnp.sum(ex, axis=0, keepdims=True)

    @pl.when(h == PEER_HEADS - 1)
    def _():
        e_ref[...] = e_scr[...].T
        g_ref[...] = g_scr[...].T


def _peer_topk(qp, sk_hi, sk_lo, tt):
    n = qp.shape[0]
    out_blk = pl.BlockSpec((tt, PEER_SLOTS), lambda i, h: (i, 0))
    sk_blk = pl.BlockSpec((None, 2, N_KEYS, LANES), lambda i, h: (h, 0, 0, 0))
    return pl.pallas_call(
        _topk_kernel,
        out_shape=[SDS((n, PEER_SLOTS), I32), SDS((n, PEER_SLOTS), F32)],
        grid=(n // tt, PEER_HEADS),
        in_specs=[pl.BlockSpec((tt, LANES), lambda i, h: (i, h)), sk_blk, sk_blk],
        out_specs=[out_blk, out_blk],
        scratch_shapes=[pltpu.VMEM((PEER_SLOTS, tt), I32), pltpu.VMEM((PEER_SLOTS, tt), F32)],
        compiler_params=_params(("arbitrary", "arbitrary")),
        name="peer_topk",
    )(qp, sk_hi, sk_lo)


def _gather_rows(idx_ref, tab_ref, gbuf, t):
    for k in range(PEER_SLOTS):
        e = pl.multiple_of(idx_ref[t, k], ROW_SUB)
        gbuf[ROW_SUB * k:ROW_SUB * (k + 1), :] = tab_ref[pl.ds(e, ROW_SUB), :]


def _gather_rows_value(idx_ref, tab_ref, t):
    rows = []
    for k in range(PEER_SLOTS):
        e = pl.multiple_of(idx_ref[t, k], ROW_SUB)
        rows.append(tab_ref[pl.ds(e, ROW_SUB), :])
    return jnp.concatenate(rows, axis=0)


def _diag_mask():
    sub = lax.broadcasted_iota(I32, (SUBLANES, D_MODEL), 0)
    col = lax.broadcasted_iota(I32, (SUBLANES, D_MODEL), 1)
    return (col & (SUBLANES - 1)) == sub


def _for_each_token(tb, gbufs, one_token):
    def group(i, carry):
        for u, gbuf in enumerate(gbufs):
            one_token(i * len(gbufs) + u, gbuf)
        return carry

    lax.fori_loop(0, tb // len(gbufs), group, 0)


def _peer_u_kernel(idx_ref, tab_ref, h2_ref, g_ref, sel_ref, w_ref, *scratch):
    gbufs, (xbuf, rall) = scratch[:TOK_UNROLL], scratch[TOK_UNROLL:]
    tb = h2_ref.shape[0]
    for j in range(SUBLANES):
        xbuf[pl.ds(j, tb, stride=SUBLANES), :] = h2_ref[:, LANES * j:LANES * (j + 1)]
    diag = _diag_mask()

    def tok(t, gbuf):
        del gbuf
        rows = pltpu.bitcast(_gather_rows_value(idx_ref, tab_ref, t), BF16)
        xt = xbuf[pl.ds(pl.multiple_of(t * SUBLANES, SUBLANES), SUBLANES), :].astype(BF16)
        r = _dot_nt(xt, rows)
        rall[pl.ds(t, 1), :] = jnp.sum(jnp.where(diag, r, 0.0), axis=0, keepdims=True)

    _for_each_token(tb, gbufs, tok)
    hi, lo = _split(rall[...])
    sel = sel_ref[...]
    a = _dot(hi, sel) + _dot(lo, sel)
    gelu = 0.5 * a * (1.0 + lax.erf(a * (2.0 ** -0.5)))
    w_ref[...] = g_ref[...] * gelu


def _peer_v_kernel(idx_ref, tab_ref, w_ref, x1_ref, gt2_ref, selt_ref, y_ref, *scratch):
    gbufs, (wrep, ybuf) = scratch[:TOK_UNROLL], scratch[TOK_UNROLL:]
    tb = x1_ref.shape[0]
    wrep[...] = _dot(w_ref[...].astype(BF16), selt_ref[...])
    diag = _diag_mask()

    def tok(t, gbuf):
        _gather_rows(idx_ref, tab_ref, gbuf, t)
        rows = pltpu.bitcast(gbuf[...], BF16)
        wt = jnp.where(diag, wrep[pl.ds(t, 1), :], 0.0).astype(BF16)
        ybuf[pl.ds(pl.multiple_of(t * SUBLANES, SUBLANES), SUBLANES), :] = _dot(wt, rows)

    _for_each_token(tb, gbufs, tok)
    for j in range(SUBLANES):
        cols = slice(LANES * j, LANES * (j + 1))
        y_ref[:, cols] = x1_ref[:, cols] + gt2_ref[:, cols] * ybuf[pl.ds(j, tb, stride=SUBLANES), :]


def _peer_u(idx, tab, h2, g, sel, tb):
    n = h2.shape[0]
    tok = lambda w: pl.BlockSpec((tb, w), lambda i: (i, 0))
    return pl.pallas_call(
        _peer_u_kernel,
        out_shape=SDS((n, PEER_SLOTS), F32),
        grid=(n // tb,),
        in_specs=[pl.BlockSpec((tb, PEER_SLOTS), lambda i: (i, 0), memory_space=pltpu.SMEM),
                  pl.BlockSpec(memory_space=pltpu.VMEM), tok(D_MODEL), tok(PEER_SLOTS),
                  pl.BlockSpec(sel.shape, lambda i: (0, 0))],
        out_specs=tok(PEER_SLOTS),
        scratch_shapes=[pltpu.VMEM((ROW_SUB * PEER_SLOTS, LANES), I32)] * TOK_UNROLL + [
                        pltpu.VMEM((tb * SUBLANES, LANES), F32),
                        pltpu.VMEM((tb, D_MODEL), F32)],
        compiler_params=_params(("arbitrary",)),
        name="peer_u",
    )(idx, tab, h2, g, sel)


def _peer_v(idx, tab, w, x1, gt2, selt, tb):
    n = x1.shape[0]
    groups, rows, _ = gt2.shape
    tiles_per_group = n // tb // groups
    tok = lambda wd: pl.BlockSpec((tb, wd), lambda i: (i, 0))
    return pl.pallas_call(
        _peer_v_kernel,
        out_shape=SDS((n, D_MODEL), F32),
        grid=(n // tb,),
        in_specs=[pl.BlockSpec((tb, PEER_SLOTS), lambda i: (i, 0), memory_space=pltpu.SMEM),
                  pl.BlockSpec(memory_space=pltpu.VMEM), tok(PEER_SLOTS), tok(D_MODEL),
                  pl.BlockSpec((None, rows, D_MODEL), lambda i: (i // tiles_per_group, 0, 0)),
                  pl.BlockSpec(selt.shape, lambda i: (0, 0))],
        out_specs=tok(D_MODEL),
        scratch_shapes=[pltpu.VMEM((ROW_SUB * PEER_SLOTS, LANES), I32)] * TOK_UNROLL + [
                        pltpu.VMEM((tb, D_MODEL), F32),
                        pltpu.VMEM((tb * SUBLANES, LANES), F32)],
        compiler_params=_params(("arbitrary",)),
        name="peer_v",
    )(idx, tab, w, x1, gt2, selt)


def _pack_table(tab):
    e = tab.shape[0]
    bits = lax.bitcast_convert_type(tab.astype(BF16), jnp.uint16).astype(jnp.uint32)
    bits = bits.reshape(e, ROW_SUB, 2, LANES)
    word = bits[:, :, 0, :] | (bits[:, :, 1, :] << 16)
    return lax.bitcast_convert_type(word, I32).reshape(e * ROW_SUB, LANES)


def _block_avg(width):
    i = np.arange(width) // HEAD_DIM
    return jnp.asarray((i[:, None] == i[None, :]).astype(np.float32) / HEAD_DIM, BF16)


def _slot_select():
    col = np.arange(D_MODEL) // SUBLANES
    return jnp.asarray((col[:, None] == np.arange(PEER_SLOTS)[None, :]).astype(np.float32), BF16)


def _ret_log_decay():
    return jnp.log1p(-jnp.exp2(-5.0 - jnp.arange(H_RET, dtype=F32)))


def _alibi_slopes():
    return jnp.exp2(-8.0 * (jnp.arange(H_ATT, dtype=F32) + 1.0) / H_ATT)


def _pad_sub_keys(sub_keys):
    z = jnp.zeros_like(sub_keys[:, 0])
    return jnp.stack([jnp.concatenate([sub_keys[:, 0], z], axis=-1),
                      jnp.concatenate([z, sub_keys[:, 1]], axis=-1)], axis=1)


def _group_w_out(w):
    w4 = w.reshape(H_RET, 2, HEAD_DIM, D_MODEL)
    return jnp.concatenate([w4[:, 0].reshape(RET_W, D_MODEL), w4[:, 1].reshape(ATT_W, D_MODEL)], axis=0)


def _split_w(w):
    hi = w.astype(BF16)
    return hi, (w - hi.astype(F32)).astype(BF16)


def _token_tile(n, pref):
    return pref if n % pref == 0 else n


def _peer(h2, qp, x1, gt2, consts, weights):
    n = h2.shape[0]
    idx, gate = _peer_topk(qp, weights["sk_hi"], weights["sk_lo"], _token_tile(n, 512))
    tb = _token_tile(n, 256)
    w = _peer_u(idx, weights["u_tab"], h2, gate, consts["sel"], tb)
    return _peer_v(idx, weights["v_tab"], w, x1, gt2, consts["selt"], tb)


def _mix_and_peer(x2d, ret, att, mods, consts, weights):
    sh1, sc1, gt1, sh2, sc2, gt2 = mods
    n = x2d.shape[0]
    x1, h2, qp = _outproj(x2d, ret, att, weights["w_out"], gt1, weights["g_norm2"], sc2, sh2,
                          weights["wq_hi"], weights["wq_lo"], _token_tile(n, 512))
    return _peer(h2, qp, x1, gt2, consts, weights)


def _project(x2d, mods, consts, weights):
    sh1, sc1 = mods[0], mods[1]
    n = x2d.shape[0]
    return _inproj(x2d, sc1, sh1, weights["g_norm1"], weights["w_in"], weights["g_qn"], weights["g_kn"],
                   consts["avg_att"], _token_tile(n, 512))


def kernel(x_prompt, x_sample, c_prompt, c_sample, state_ret, cache_win_k, cache_win_v, w_ada, b_ada, g_norm1, w_in, g_qn, g_kn, g_ret, w_out, g_norm2, w_pq, sub_keys, peer_u, peer_v):
    b, s, d = x_prompt.shape
    db, ds_, _ = x_sample.shape
    depth = w_ada.shape[0]
    assert d == D_MODEL and ds_ == 1 and s % ATT_TILE == 0
    win = cache_win_k.shape[2]
    assert win == MAX_WINDOW

    consts = {
        "avg_att": _block_avg(ATT_W),
        "sel": _slot_select(),
        "selt": _slot_select().T,
        "lg": _ret_log_decay(),
        "slopes": _alibi_slopes(),
    }

    y_p = x_prompt.reshape(b * s, d)
    y_s = x_sample.reshape(db, d)
    outs = {k: [] for k in ("ret_p", "wk_p", "wv_p", "ret_s", "wk_s", "wv_s")}
    for l in range(depth):
        wq_hi, wq_lo = _split_w(w_pq[l])
        sk_hi, sk_lo = _split_w(_pad_sub_keys(sub_keys[l]))
        weights = {
            "g_norm1": g_norm1[l].reshape(1, d), "g_norm2": g_norm2[l].reshape(1, d),
            "w_in": w_in[l].astype(BF16), "w_out": _group_w_out(w_out[l]).astype(BF16),
            "g_qn": g_qn[l].reshape(1, ATT_W), "g_kn": g_kn[l].reshape(1, ATT_W),
            "wq_hi": wq_hi, "wq_lo": wq_lo, "sk_hi": sk_hi, "sk_lo": sk_lo,
            "u_tab": _pack_table(peer_u[l]), "v_tab": _pack_table(peer_v[l]),
        }
        gr = g_ret[l].reshape(1, RET_W)

        mod = _adaln(jnp.concatenate([c_prompt, c_sample], axis=0), w_ada[l], b_ada[l])
        mod6 = jnp.split(mod, 6, axis=-1)
        mods_p = [m[:b].reshape(b, 1, d) for m in mod6]
        mods_s = [m[b:].reshape(1, db, d) for m in mod6]

        rq, rk, rv, rg, aq, ak, av = _project(y_p, mods_p, consts, weights)
        seq = lambda a: a.reshape(b, s, a.shape[-1])
        ret, ret_state = _retention_prompt(consts["lg"], seq(rq), seq(rk), seq(rv), seq(rg), gr, 1024)
        att = _attention_prompt(consts["slopes"], seq(aq), seq(ak), seq(av))
        y_p = _mix_and_peer(y_p, ret.reshape(b * s, RET_W), att.reshape(b * s, ATT_W), mods_p, consts, weights)
        keep = min(MAX_WINDOW, s)
        outs["ret_p"].append(ret_state)
        outs["wk_p"].append(seq(ak)[:, s - keep:].reshape(b, keep, H_ATT, HEAD_DIM))
        outs["wv_p"].append(seq(av)[:, s - keep:].reshape(b, keep, H_ATT, HEAD_DIM))

        rq, rk, rv, rg, aq, ak, av = _project(y_s, mods_s, consts, weights)
        ret, new_state = _retention_sample(consts["lg"], rq, rk, rv, rg, gr, state_ret[l])
        att = _attention_sample(consts["slopes"], aq, ak, av, cache_win_k[l], cache_win_v[l])
        y_s = _mix_and_peer(y_s, ret.reshape(db, RET_W), att.reshape(db, ATT_W), mods_s, consts, weights)
        outs["ret_s"].append(new_state)
        outs["wk_s"].append(ak.reshape(db, 1, H_ATT, HEAD_DIM))
        outs["wv_s"].append(av.reshape(db, 1, H_ATT, HEAD_DIM))

    st = lambda k: jnp.stack(outs[k])
    return (y_p.reshape(b, s, d), y_s.reshape(db, 1, d), st("ret_p"), st("wk_p"), st("wv_p"),
            st("ret_s"), st("wk_s"), st("wv_s"))
```

```python
import functools
import math

import numpy as np
import jax
import jax.numpy as jnp
from jax import lax
from jax.experimental import pallas as pl
from jax.experimental.pallas import tpu as pltpu

F32 = jnp.float32
BF16 = jnp.bfloat16
I32 = jnp.int32

D_MODEL = 1024
HEAD_DIM = 64
RET_W = 512
ATT_W = 512
H_RET = 8
H_ATT = 8
N_PROJ = 4 * RET_W + 3 * ATT_W
RET_CHUNK = 128
DIL_BRANCHES = ((128, 1), (512, 4), (2048, 16))
NW = 128
MAX_WINDOW = 2048
N_KEYS = 128
PEER_HEADS = 8
PEER_TOPK = 16
PEER_SLOTS = PEER_HEADS * PEER_TOPK
EPS = 1e-6

LANES = 128
SUBLANES = 8
ROW_WORDS = D_MODEL // 2
ROW_SUB = ROW_WORDS // LANES
VMEM_LIMIT = 56 * 1024 * 1024
RET_UNROLL = 4
TOK_UNROLL = 16

SDS = jax.ShapeDtypeStruct


def _dot(a, b):
    return jnp.dot(a, b, preferred_element_type=F32)


def _dot_nt(a, b):
    return lax.dot_general(a, b, (((1,), (1,)), ((), ())), preferred_element_type=F32)


def _dot_tn(a, b):
    return lax.dot_general(a, b, (((0,), (0,)), ((), ())), preferred_element_type=F32)


def _split(x):
    hi = x.astype(BF16)
    lo = (x - hi.astype(F32)).astype(BF16)
    return hi, lo


def _silu(x):
    return x * jax.nn.sigmoid(x)


def _params(sem, vmem=VMEM_LIMIT):
    return pltpu.CompilerParams(dimension_semantics=sem, vmem_limit_bytes=vmem)


def _adaln_kernel(c_ref, w_ref, b_ref, o_ref):
    s = _silu(c_ref[...])
    o_ref[...] = jnp.dot(s, w_ref[...], preferred_element_type=F32,
                         precision=lax.Precision.HIGHEST) + b_ref[...]


def _adaln(c, w_ada, b_ada):
    rows = c.shape[0]
    tn = 512
    return pl.pallas_call(
        _adaln_kernel,
        out_shape=SDS((rows, 6 * D_MODEL), F32),
        grid=(6 * D_MODEL // tn,),
        in_specs=[pl.BlockSpec((rows, D_MODEL), lambda j: (0, 0)),
                  pl.BlockSpec((D_MODEL, tn), lambda j: (0, j)),
                  pl.BlockSpec((1, tn), lambda j: (0, j))],
        out_specs=pl.BlockSpec((rows, tn), lambda j: (0, j)),
        compiler_params=_params(("arbitrary",)),
        name="adaln",
    )(c, w_ada, b_ada.reshape(1, -1))


def _inproj_kernel(x_ref, sc_ref, sh_ref, g1_ref, w_ref, gq_ref, gk_ref, bd_ref,
                   rq_ref, rk_ref, rv_ref, rg_ref, aq_ref, ak_ref, av_ref):
    x = x_ref[...]
    ms = jnp.mean(x * x, axis=-1, keepdims=True)
    h = x * lax.rsqrt(ms + EPS) * g1_ref[...]
    h = h * (1.0 + sc_ref[...]) + sh_ref[...]
    p = _dot(h.astype(BF16), w_ref[...])
    scale = HEAD_DIM ** -0.5
    rq_ref[...] = p[:, 0:RET_W].astype(BF16)
    rk_ref[...] = (p[:, RET_W:2 * RET_W] * scale).astype(BF16)
    rv_ref[...] = p[:, 2 * RET_W:3 * RET_W].astype(BF16)
    rg_ref[...] = p[:, 3 * RET_W:4 * RET_W].astype(BF16)
    o = 4 * RET_W

    def head_norm(a, g):
        msq = _dot((a * a).astype(BF16), bd_ref[...])
        return a * lax.rsqrt(msq + EPS) * g

    aq_ref[...] = head_norm(p[:, o:o + ATT_W], gq_ref[...]) * scale
    ak_ref[...] = head_norm(p[:, o + ATT_W:o + 2 * ATT_W], gk_ref[...])
    av_ref[...] = p[:, o + 2 * ATT_W:o + 3 * ATT_W]


def _inproj(x2d, sc, sh, g1, w_in_bf, gq, gk, bd, tb):
    n = x2d.shape[0]
    groups, rows, _ = sc.shape
    tiles_per_group = n // tb // groups
    mod_spec = pl.BlockSpec((None, rows, D_MODEL), lambda i: (i // tiles_per_group, 0, 0))
    tok = lambda w: pl.BlockSpec((tb, w), lambda i: (i, 0))
    full = lambda a: pl.BlockSpec(a.shape, lambda i: (0,) * a.ndim)
    outs = [SDS((n, RET_W), BF16)] * 4 + [SDS((n, ATT_W), F32)] * 3
    return pl.pallas_call(
        _inproj_kernel,
        out_shape=outs,
        grid=(n // tb,),
        in_specs=[tok(D_MODEL), mod_spec, mod_spec, full(g1), full(w_in_bf), full(gq), full(gk), full(bd)],
        out_specs=[tok(RET_W)] * 4 + [tok(ATT_W)] * 3,
        compiler_params=_params(("arbitrary",)),
        name="inproj",
    )(x2d, sc, sh, g1, w_in_bf, gq, gk, bd)


def _ret_kernel(lg_ref, q_ref, k_ref, v_ref, rg_ref, gr_ref, o_ref, st_ref, s_scr, *, nchunk):
    hp = pl.program_id(1)
    t = pl.program_id(2)
    C = RET_CHUNK
    lane = lax.broadcasted_iota(I32, (1, LANES), 1)
    m0 = lane < HEAD_DIM
    lg0 = lg_ref[2 * hp]
    lg1 = lg_ref[2 * hp + 1]
    lgv = jnp.where(m0, lg0, lg1)

    @pl.when(t == 0)
    def _():
        s_scr[...] = jnp.zeros_like(s_scr)

    ii = lax.broadcasted_iota(I32, (C, C), 0)
    jj = lax.broadcasted_iota(I32, (C, C), 1)
    diff = ii - jj
    dpos = jnp.maximum(diff, 0).astype(F32)
    dec01 = jnp.concatenate([jnp.where(diff >= 0, jnp.exp(dpos * lg0), 0.0),
                             jnp.where(diff >= 0, jnp.exp(dpos * lg1), 0.0)], axis=0)

    def head_mean(x):
        s0 = jnp.sum(jnp.where(m0, x, 0.0), axis=-1, keepdims=True)
        s1 = jnp.sum(jnp.where(m0, 0.0, x), axis=-1, keepdims=True)
        return jnp.where(m0, s0, s1) * (1.0 / HEAD_DIM)
    ri = lax.broadcasted_iota(I32, (C, LANES), 0).astype(F32)
    w_start = jnp.exp((ri + 1.0) * lgv)
    w_end = jnp.exp((float(C) - 1.0 - ri) * lgv)
    chunk_decay = jnp.exp(float(C) * lgv)
    rr = lax.broadcasted_iota(I32, (LANES, LANES), 0)
    cc = lax.broadcasted_iota(I32, (LANES, LANES), 1)
    same_head = (rr < HEAD_DIM) == (cc < HEAD_DIM)
    gr = gr_ref[...]

    def chunk(c, state):
        rows = pl.ds(pl.multiple_of(c * C, C), C)
        q = q_ref[rows, :]
        k = k_ref[rows, :]
        v = v_ref[rows, :]
        qf = q.astype(F32)
        kf = k.astype(F32)
        q01 = jnp.concatenate([jnp.where(m0, qf, 0.0), jnp.where(m0, 0.0, qf)], axis=0).astype(BF16)
        p01 = (_dot_nt(q01, k) * dec01).astype(BF16)
        pv = _dot(p01, v)
        o_intra = jnp.where(m0, pv[0:C, :], pv[C:, :])
        o_inter = _dot((qf * w_start).astype(BF16), state.astype(BF16))
        o = o_intra + o_inter
        kv = _dot_tn((kf * w_end).astype(BF16), v)
        mu = head_mean(o)
        xc = o - mu
        var = head_mean(xc * xc)
        y = xc * lax.rsqrt(var + EPS) * gr
        y = y * _silu(rg_ref[rows, :].astype(F32))
        o_ref[rows, :] = y.astype(BF16)
        return chunk_decay * state + jnp.where(same_head, kv, 0.0)

    def chunk_group(i, state):
        for u in range(RET_UNROLL):
            state = chunk(i * RET_UNROLL + u, state)
        return state

    s_scr[...] = lax.fori_loop(0, nchunk // RET_UNROLL, chunk_group, s_scr[...])

    @pl.when(t == pl.num_programs(2) - 1)
    def _():
        state = s_scr[...]
        st_ref[0] = state[0:HEAD_DIM, 0:HEAD_DIM]
        st_ref[1] = state[HEAD_DIM:, HEAD_DIM:]


def _retention_prompt(lg, rq, rk, rv, rg, g_ret, cb):
    b, s, _ = rq.shape
    nchunk = cb // RET_CHUNK
    blk = pl.BlockSpec((None, cb, LANES), lambda bi, hp, t, lg_: (bi, t, hp))
    grid_spec = pltpu.PrefetchScalarGridSpec(
        num_scalar_prefetch=1,
        grid=(b, H_RET // 2, s // cb),
        in_specs=[blk, blk, blk, blk,
                  pl.BlockSpec((1, LANES), lambda bi, hp, t, lg_: (0, hp))],
        out_specs=[blk, pl.BlockSpec((None, 2, HEAD_DIM, HEAD_DIM), lambda bi, hp, t, lg_: (bi, hp, 0, 0))],
        scratch_shapes=[pltpu.VMEM((LANES, LANES), F32)],
    )
    return pl.pallas_call(
        functools.partial(_ret_kernel, nchunk=nchunk),
        out_shape=[SDS((b, s, RET_W), BF16), SDS((b, H_RET, HEAD_DIM, HEAD_DIM), F32)],
        grid_spec=grid_spec,
        compiler_params=_params(("arbitrary", "arbitrary", "arbitrary")),
        name="retention_prompt",
    )(lg, rq, rk, rv, rg, g_ret)


ATT_TILE = MAX_WINDOW


ATT_UNROLL = 4


def _att_kernel(sl_ref, q_ref, k_ref, v_ref, o_ref, kk, vv, *stats):
    hp = pl.program_id(1)
    t = pl.program_id(2)
    TS = ATT_TILE
    nbr = len(DIL_BRANCHES)
    m_br, l_br, acc_br = stats[:nbr], stats[nbr:2 * nbr], stats[2 * nbr:]
    cur = pl.multiple_of((t & 1) * TS, TS)

    @pl.when(t == 0)
    def _():
        kk[TS:2 * TS, :] = jnp.zeros((TS, LANES), F32)
        vv[TS:2 * TS, :] = jnp.zeros((TS, LANES), F32)

    kk[pl.ds(cur, TS), :] = k_ref[...]
    vv[pl.ds(cur, TS), :] = v_ref[...]

    lane = lax.broadcasted_iota(I32, (1, LANES), 1)
    m0 = lane < HEAD_DIM
    head_lanes = (m0, jnp.logical_not(m0))
    ii = lax.broadcasted_iota(I32, (NW, 2 * NW), 0)
    jj = lax.broadcasted_iota(I32, (NW, 2 * NW), 1)
    off = ii + NW - jj
    band = (off >= 0) & (off <= NW)
    off_f = off.astype(F32)
    ones_h = [jnp.broadcast_to(jnp.where(hl, 1.0, 0.0).astype(BF16), (2 * NW, LANES)) for hl in head_lanes]

    for br, (_, dil) in enumerate(DIL_BRANCHES):
        nblk = TS // (NW * dil)
        shift = int(math.log2(nblk))
        bias = [jnp.where(band, -(sl_ref[2 * hp + hh] * float(dil)) * off_f, -jnp.inf) for hh in range(2)]

        def one_block(bs, dil=dil, nblk=nblk, shift=shift, br=br, bias=bias):
            r = lax.shift_right_logical(bs, shift)
            blk = bs & (nblk - 1)
            qs = r + dil * NW * blk
            q_rows = pl.ds(qs, NW, stride=dil)
            own = pl.ds(cur + qs, NW, stride=dil)
            prev = pl.ds((cur + qs - dil * NW) & (2 * TS - 1), NW, stride=dil)
            qb = q_ref[q_rows, :]
            kb = jnp.concatenate([kk[prev, :], kk[own, :]], axis=0).astype(BF16)
            vb = jnp.concatenate([vv[prev, :], vv[own, :]], axis=0)
            first_key = jnp.where(jnp.logical_or(t > 0, blk > 0), 0, NW)
            started = jj >= first_key
            res = None
            mx = []
            for hh in range(2):
                qh = jnp.where(head_lanes[hh], qb, 0.0).astype(BF16)
                s = jnp.where(started, _dot_nt(qh, kb) + bias[hh], -jnp.inf)
                mh = jnp.max(s, axis=-1, keepdims=True)
                p = jnp.exp(s - mh).astype(BF16)
                vh = jnp.concatenate([jnp.where(head_lanes[hh], vb, 0.0).astype(BF16), ones_h[hh]], axis=1)
                part = _dot(p, vh)
                res = part if res is None else res + part
                mx.append(mh)
            m_br[br][q_rows, :] = jnp.where(m0, mx[0], mx[1])
            acc_br[br][q_rows, :] = res[:, :LANES]
            l_br[br][q_rows, :] = res[:, LANES:]

        def body(i, carry, one_block=one_block):
            for u in range(ATT_UNROLL):
                one_block(i * ATT_UNROLL + u)
            return carry

        lax.fori_loop(0, TS // NW // ATT_UNROLL, body, 0)

    rows_per_step = 256

    def merge(i, carry):
        rows = pl.ds(pl.multiple_of(i * rows_per_step, rows_per_step), rows_per_step)
        ms = [m[rows, :] for m in m_br]
        m_all = functools.reduce(jnp.maximum, ms)
        ws = [jnp.exp(m - m_all) for m in ms]
        num = sum(w * a[rows, :] for w, a in zip(ws, acc_br))
        den = sum(w * l[rows, :] for w, l in zip(ws, l_br))
        o_ref[rows, :] = (num / den).astype(BF16)
        return carry

    lax.fori_loop(0, TS // rows_per_step, merge, 0)


def _attention_prompt(slopes, aq, ak, av):
    b, s, _ = aq.shape
    ts = ATT_TILE
    blk = pl.BlockSpec((None, ts, LANES), lambda bi, hp, t, sl: (bi, t, hp))
    grid_spec = pltpu.PrefetchScalarGridSpec(
        num_scalar_prefetch=1,
        grid=(b, H_ATT // 2, s // ts),
        in_specs=[blk, blk, blk],
        out_specs=blk,
        scratch_shapes=[pltpu.VMEM((2 * ts, LANES), F32)] * 2
        + [pltpu.VMEM((ts, LANES), F32)] * (3 * len(DIL_BRANCHES)),
    )
    return pl.pallas_call(
        _att_kernel,
        out_shape=SDS((b, s, ATT_W), BF16),
        grid_spec=grid_spec,
        compiler_params=_params(("arbitrary", "arbitrary", "arbitrary")),
        name="attention_prompt",
    )(slopes, aq, ak, av)


def _sret_kernel(lg_ref, q_ref, k_ref, v_ref, rg_ref, gr_ref, st_ref, o_ref, sn_ref):
    q = q_ref[...].astype(F32)
    k = k_ref[...].astype(F32)
    v = v_ref[...].astype(F32)
    rg = rg_ref[...].astype(F32)
    gr = gr_ref[...]
    r_i = lax.broadcasted_iota(I32, (HEAD_DIM, HEAD_DIM), 0)
    c_i = lax.broadcasted_iota(I32, (HEAD_DIM, HEAD_DIM), 1)
    eye = r_i == c_i
    outs = []
    for h in range(H_RET):
        sl = slice(h * HEAD_DIM, (h + 1) * HEAD_DIM)
        qh, kh, vh = q[:, sl], k[:, sl], v[:, sl]
        gamma = jnp.exp(jnp.full((1, HEAD_DIM), lg_ref[h], F32))
        q_col = jnp.sum(jnp.where(eye, qh, 0.0), axis=1, keepdims=True)
        k_col = jnp.sum(jnp.where(eye, kh, 0.0), axis=1, keepdims=True)
        state = st_ref[h]
        o_inter = gamma * jnp.sum(q_col * state, axis=0, keepdims=True)
        o_intra = jnp.sum(qh * kh, axis=-1, keepdims=True) * vh
        sn_ref[h] = gamma * state + k_col * vh
        o = o_intra + o_inter
        mu = jnp.mean(o, axis=-1, keepdims=True)
        xc = o - mu
        var = jnp.mean(xc * xc, axis=-1, keepdims=True)
        outs.append(xc * lax.rsqrt(var + EPS) * gr[:, sl] * _silu(rg[:, sl]))
    o_ref[...] = jnp.concatenate(outs, axis=-1).astype(BF16)


def _retention_sample(lg, rq, rk, rv, rg, g_ret, state):
    db = rq.shape[0]
    row = pl.BlockSpec((None, 1, RET_W), lambda i: (i, 0, 0))
    st = pl.BlockSpec((None, H_RET, HEAD_DIM, HEAD_DIM), lambda i: (i, 0, 0, 0))
    r3 = lambda a: a.reshape(db, 1, RET_W)
    return pl.pallas_call(
        _sret_kernel,
        out_shape=[SDS((db, 1, RET_W), BF16), SDS(state.shape, F32)],
        grid=(db,),
        in_specs=[pl.BlockSpec(memory_space=pltpu.SMEM), row, row, row, row,
                  pl.BlockSpec((1, RET_W), lambda i: (0, 0)), st],
        out_specs=[row, st],
        compiler_params=_params(("arbitrary",)),
        name="retention_sample",
    )(lg, r3(rq), r3(rk), r3(rv), r3(rg), g_ret, state)


def _satt_kernel(sl_ref, qt_ref, knt_ref, vnt_ref, kt_ref, vt_ref, o_ref):
    win = kt_ref.shape[-1]
    dist = win - lax.broadcasted_iota(I32, (1, win), 1)
    count = jnp.zeros((1, win), F32)
    for window, dil in DIL_BRANCHES:
        reads = ((dist & (dil - 1)) == 0) & (dist <= window)
        count = count + jnp.where(reads, 1.0, 0.0)
    dist_f = dist.astype(F32)
    nbr = float(len(DIL_BRANCHES))
    for h in range(H_ATT):
        q = qt_ref[:, h:h + 1]
        s = jnp.sum(kt_ref[h] * q, axis=0, keepdims=True) - sl_ref[h] * dist_f
        s = jnp.where(count > 0.0, s, -jnp.inf)
        s_new = jnp.sum(q * knt_ref[:, h:h + 1], axis=0, keepdims=True)
        m = jnp.maximum(jnp.max(s, axis=-1, keepdims=True), s_new)
        p = count * jnp.exp(s - m)
        p_new = nbr * jnp.exp(s_new - m)
        den = jnp.sum(p, axis=-1, keepdims=True) + p_new
        num = jnp.sum(vt_ref[h] * p, axis=-1, keepdims=True) + p_new * vnt_ref[:, h:h + 1]
        o_ref[:, h:h + 1] = num / den


def _attention_sample(slopes, aq, ak, av, win_k, win_v):
    db = aq.shape[0]
    win = win_k.shape[1]
    col = pl.BlockSpec((None, HEAD_DIM, H_ATT), lambda i: (i, 0, 0))
    cache = pl.BlockSpec((None, H_ATT, HEAD_DIM, win), lambda i: (i, 0, 0, 0))
    cols = lambda a: jnp.transpose(a.reshape(db, H_ATT, HEAD_DIM), (0, 2, 1))
    keys_minor = lambda c: jnp.transpose(c, (0, 2, 3, 1))
    out = pl.pallas_call(
        _satt_kernel,
        out_shape=SDS((db, HEAD_DIM, H_ATT), F32),
        grid=(db,),
        in_specs=[pl.BlockSpec(memory_space=pltpu.SMEM), col, col, col, cache, cache],
        out_specs=col,
        compiler_params=_params(("arbitrary",)),
        name="attention_sample",
    )(slopes, cols(aq), cols(ak), cols(av), keys_minor(win_k), keys_minor(win_v))
    return jnp.transpose(out, (0, 2, 1)).reshape(db, ATT_W).astype(BF16)


def _outproj_kernel(x_ref, ret_ref, att_ref, wo_ref, gt1_ref, g2_ref, sc2_ref, sh2_ref, wqh_ref, wql_ref,
                    x1_ref, h2_ref, qp_ref):
    mix = _dot(ret_ref[...], wo_ref[0:RET_W, :]) + _dot(att_ref[...], wo_ref[RET_W:, :])
    x1 = x_ref[...] + gt1_ref[...] * mix
    x1_ref[...] = x1
    ms = jnp.mean(x1 * x1, axis=-1, keepdims=True)
    h2 = x1 * lax.rsqrt(ms + EPS) * g2_ref[...]
    h2 = h2 * (1.0 + sc2_ref[...]) + sh2_ref[...]
    h2_ref[...] = h2
    hi, lo = _split(h2)
    wqh = wqh_ref[...]
    qp_ref[...] = _dot(hi, wqh) + (_dot(lo, wqh) + _dot(hi, wql_ref[...]))


def _outproj(x2d, ret, att, wo_bf, gt1, g2, sc2, sh2, wq_hi, wq_lo, tb):
    n = x2d.shape[0]
    groups, rows, _ = gt1.shape
    tiles_per_group = n // tb // groups
    mod_spec = pl.BlockSpec((None, rows, D_MODEL), lambda i: (i // tiles_per_group, 0, 0))
    tok = lambda w: pl.BlockSpec((tb, w), lambda i: (i, 0))
    full = lambda a: pl.BlockSpec(a.shape, lambda i: (0,) * a.ndim)
    return pl.pallas_call(
        _outproj_kernel,
        out_shape=[SDS((n, D_MODEL), F32)] * 3,
        grid=(n // tb,),
        in_specs=[tok(D_MODEL), tok(RET_W), tok(ATT_W), full(wo_bf), mod_spec, full(g2), mod_spec, mod_spec,
                  full(wq_hi), full(wq_lo)],
        out_specs=[tok(D_MODEL)] * 3,
        compiler_params=_params(("arbitrary",)),
        name="outproj",
    )(x2d, ret, att, wo_bf, gt1, g2, sc2, sh2, wq_hi, wq_lo)


def _top16(sc, ids=None):
    if ids is None:
        ids = lax.broadcasted_iota(I32, sc.shape, 0)
    ids = ids.astype(F32)
    vals, idxs = [], []
    for _ in range(PEER_TOPK):
        m = jnp.max(sc, axis=0, keepdims=True)
        idx = jnp.min(jnp.where(sc == m, ids, jnp.inf), axis=0, keepdims=True)
        vals.append(m)
        idxs.append(idx)
        sc = jnp.where(ids == idx, -jnp.inf, sc)
    return jnp.concatenate(vals, axis=0), jnp.concatenate(idxs, axis=0).astype(I32)


def _pair_candidates(s1, s2):
    k = PEER_TOPK
    half = k // 2
    pieces = [s1[0:1, :] + s2]
    pieces += [s1[a:a + 1, :] + s2[0:half, :] for a in range(1, half)]
    pieces.append(s1[half:, :] + s2[0:1, :])
    cand = jnp.concatenate(pieces, axis=0)
    r = lax.broadcasted_iota(I32, cand.shape, 0)
    mid = r - k
    a_mid = 1 + lax.shift_right_logical(mid, 3)
    b_mid = mid & (half - 1)
    tail_start = k + half * (half - 1)
    pos = jnp.where(r < k, r, jnp.where(r < tail_start, a_mid * k + b_mid, (half + r - tail_start) * k))
    reachable = jnp.logical_or(jnp.logical_or(r < k, r >= tail_start), (a_mid + 1) * (b_mid + 1) <= k)
    return jnp.where(reachable, cand, -jnp.inf), pos


def _topk_kernel(q_ref, skh_ref, skl_ref, e_ref, g_ref, e_scr, g_scr):
    qh, ql = _split(q_ref[...])

    def scores(c):
        kh = skh_ref[c]
        return _dot_nt(kh, qh) + (_dot_nt(kh, ql) + _dot_nt(skl_ref[c], qh))

    s1, i1 = _top16(scores(0))
    s2, i2 = _top16(scores(1))
    top_s, pos = _top16(*_pair_candidates(s1, s2))
    a_sel = lax.shift_right_logical(pos, 4)
    b_sel = pos & (PEER_TOPK - 1)
    io = lax.broadcasted_iota(I32, i1.shape, 0)
    rows = []
    for r in range(PEER_TOPK):
        ia = jnp.sum(jnp.where(io == a_sel[r:r + 1, :], i1, 0), axis=0, keepdims=True)
        ib = jnp.sum(jnp.where(io == b_sel[r:r + 1, :], i2, 0), axis=0, keepdims=True)
        rows.append(ia * N_KEYS + ib)
    h = pl.program_id(1)
    slot_rows = pl.ds(pl.multiple_of(h * PEER_TOPK, PEER_TOPK), PEER_TOPK)
    e_scr[slot_rows, :] = jnp.concatenate(rows, axis=0) * ROW_SUB
    ex = jnp.exp(top_s - top_s[0:1, :])
    g_scr[slot_rows, :] = ex / jnp.sum(ex, axis=0, keepdims=True)

    @pl.when(h == PEER_HEADS - 1)
    def _():
        e_ref[...] = e_scr[...].T
        g_ref[...] = g_scr[...].T


def _peer_topk(qp, sk_hi, sk_lo, tt):
    n = qp.shape[0]
    out_blk = pl.BlockSpec((tt, PEER_SLOTS), lambda i, h: (i, 0))
    sk_blk = pl.BlockSpec((None, 2, N_KEYS, LANES), lambda i, h: (h, 0, 0, 0))
    return pl.pallas_call(
        _topk_kernel,
        out_shape=[SDS((n, PEER_SLOTS), I32), SDS((n, PEER_SLOTS), F32)],
        grid=(n // tt, PEER_HEADS),
        in_specs=[pl.BlockSpec((tt, LANES), lambda i, h: (i, h)), sk_blk, sk_blk],
        out_specs=[out_blk, out_blk],
        scratch_shapes=[pltpu.VMEM((PEER_SLOTS, tt), I32), pltpu.VMEM((PEER_SLOTS, tt), F32)],
        compiler_params=_params(("arbitrary", "arbitrary")),
        name="peer_topk",
    )(qp, sk_hi, sk_lo)


def _gather_rows(idx_ref, tab_ref, t):
    rows = []
    for k in range(PEER_SLOTS):
        e = pl.multiple_of(idx_ref[t, k], ROW_SUB)
        rows.append(tab_ref[pl.ds(e, ROW_SUB), :])
    return pltpu.bitcast(jnp.concatenate(rows, axis=0), BF16)


def _diag_mask():
    sub = lax.broadcasted_iota(I32, (SUBLANES, D_MODEL), 0)
    col = lax.broadcasted_iota(I32, (SUBLANES, D_MODEL), 1)
    return (col & (SUBLANES - 1)) == sub


def _for_each_token(tb, one_token):
    def group(i, carry):
        for u in range(TOK_UNROLL):
            one_token(i * TOK_UNROLL + u)
        return carry

    lax.fori_loop(0, tb // TOK_UNROLL, group, 0)


def _peer_u_kernel(idx_ref, tab_ref, h2_ref, g_ref, sel_ref, w_ref, xbuf, rall):
    tb = h2_ref.shape[0]
    for j in range(SUBLANES):
        xbuf[pl.ds(j, tb, stride=SUBLANES), :] = h2_ref[:, LANES * j:LANES * (j + 1)]
    diag = _diag_mask()

    def tok(t):
        rows = _gather_rows(idx_ref, tab_ref, t)
        xt = xbuf[pl.ds(pl.multiple_of(t * SUBLANES, SUBLANES), SUBLANES), :].astype(BF16)
        r = _dot_nt(xt, rows)
        rall[pl.ds(t, 1), :] = jnp.sum(jnp.where(diag, r, 0.0), axis=0, keepdims=True)

    _for_each_token(tb, tok)
    hi, lo = _split(rall[...])
    sel = sel_ref[...]
    a = _dot(hi, sel) + _dot(lo, sel)
    gelu = 0.5 * a * (1.0 + lax.erf(a * (2.0 ** -0.5)))
    w_ref[...] = g_ref[...] * gelu


def _peer_v_kernel(idx_ref, tab_ref, w_ref, x1_ref, gt2_ref, selt_ref, y_ref, wrep, ybuf):
    tb = x1_ref.shape[0]
    wrep[...] = _dot(w_ref[...].astype(BF16), selt_ref[...])
    diag = _diag_mask()

    def tok(t):
        rows = _gather_rows(idx_ref, tab_ref, t)
        wt = jnp.where(diag, wrep[pl.ds(t, 1), :], 0.0).astype(BF16)
        ybuf[pl.ds(pl.multiple_of(t * SUBLANES, SUBLANES), SUBLANES), :] = _dot(wt, rows)

    _for_each_token(tb, tok)
    for j in range(SUBLANES):
        cols = slice(LANES * j, LANES * (j + 1))
        y_ref[:, cols] = x1_ref[:, cols] + gt2_ref[:, cols] * ybuf[pl.ds(j, tb, stride=SUBLANES), :]


def _peer_u(idx, tab, h2, g, sel, tb):
    n = h2.shape[0]
    tok = lambda w: pl.BlockSpec((tb, w), lambda i: (i, 0))
    return pl.pallas_call(
        _peer_u_kernel,
        out_shape=SDS((n, PEER_SLOTS), F32),
        grid=(n // tb,),
        in_specs=[pl.BlockSpec((tb, PEER_SLOTS), lambda i: (i, 0), memory_space=pltpu.SMEM),
                  pl.BlockSpec(memory_space=pltpu.VMEM), tok(D_MODEL), tok(PEER_SLOTS),
                  pl.BlockSpec(sel.shape, lambda i: (0, 0))],
        out_specs=tok(PEER_SLOTS),
        scratch_shapes=[pltpu.VMEM((tb * SUBLANES, LANES), F32), pltpu.VMEM((tb, D_MODEL), F32)],
        compiler_params=_params(("arbitrary",)),
        name="peer_u",
    )(idx, tab, h2, g, sel)


def _peer_v(idx, tab, w, x1, gt2, selt, tb):
    n = x1.shape[0]
    groups, rows, _ = gt2.shape
    tiles_per_group = n // tb // groups
    tok = lambda wd: pl.BlockSpec((tb, wd), lambda i: (i, 0))
    return pl.pallas_call(
        _peer_v_kernel,
        out_shape=SDS((n, D_MODEL), F32),
        grid=(n // tb,),
        in_specs=[pl.BlockSpec((tb, PEER_SLOTS), lambda i: (i, 0), memory_space=pltpu.SMEM),
                  pl.BlockSpec(memory_space=pltpu.VMEM), tok(PEER_SLOTS), tok(D_MODEL),
                  pl.BlockSpec((None, rows, D_MODEL), lambda i: (i // tiles_per_group, 0, 0)),
                  pl.BlockSpec(selt.shape, lambda i: (0, 0))],
        out_specs=tok(D_MODEL),
        scratch_shapes=[pltpu.VMEM((tb, D_MODEL), F32), pltpu.VMEM((tb * SUBLANES, LANES), F32)],
        compiler_params=_params(("arbitrary",)),
        name="peer_v",
    )(idx, tab, w, x1, gt2, selt)


def _pack_table(tab):
    e = tab.shape[0]
    bits = lax.bitcast_convert_type(tab.astype(BF16), jnp.uint16).astype(jnp.uint32)
    bits = bits.reshape(e, ROW_SUB, 2, LANES)
    word = bits[:, :, 0, :] | (bits[:, :, 1, :] << 16)
    return lax.bitcast_convert_type(word, I32).reshape(e * ROW_SUB, LANES)


def _block_avg(width):
    i = np.arange(width) // HEAD_DIM
    return jnp.asarray((i[:, None] == i[None, :]).astype(np.float32) / HEAD_DIM, BF16)


def _slot_select():
    col = np.arange(D_MODEL) // SUBLANES
    return jnp.asarray((col[:, None] == np.arange(PEER_SLOTS)[None, :]).astype(np.float32), BF16)


def _ret_log_decay():
    return jnp.log1p(-jnp.exp2(-5.0 - jnp.arange(H_RET, dtype=F32)))


def _alibi_slopes():
    return jnp.exp2(-8.0 * (jnp.arange(H_ATT, dtype=F32) + 1.0) / H_ATT)


def _pad_sub_keys(sub_keys):
    z = jnp.zeros_like(sub_keys[:, 0])
    return jnp.stack([jnp.concatenate([sub_keys[:, 0], z], axis=-1),
                      jnp.concatenate([z, sub_keys[:, 1]], axis=-1)], axis=1)


def _group_w_out(w):
    w4 = w.reshape(H_RET, 2, HEAD_DIM, D_MODEL)
    return jnp.concatenate([w4[:, 0].reshape(RET_W, D_MODEL), w4[:, 1].reshape(ATT_W, D_MODEL)], axis=0)


def _split_w(w):
    hi = w.astype(BF16)
    return hi, (w - hi.astype(F32)).astype(BF16)


def _token_tile(n, pref):
    return pref if n % pref == 0 else n


def _peer(h2, qp, x1, gt2, consts, weights):
    n = h2.shape[0]
    idx, gate = _peer_topk(qp, weights["sk_hi"], weights["sk_lo"], _token_tile(n, 512))
    tb = _token_tile(n, 256)
    w = _peer_u(idx, weights["u_tab"], h2, gate, consts["sel"], tb)
    return _peer_v(idx, weights["v_tab"], w, x1, gt2, consts["selt"], tb)


def _mix_and_peer(x2d, ret, att, mods, consts, weights):
    sh1, sc1, gt1, sh2, sc2, gt2 = mods
    n = x2d.shape[0]
    x1, h2, qp = _outproj(x2d, ret, att, weights["w_out"], gt1, weights["g_norm2"], sc2, sh2,
                          weights["wq_hi"], weights["wq_lo"], _token_tile(n, 512))
    return _peer(h2, qp, x1, gt2, consts, weights)


def _project(x2d, mods, consts, weights):
    sh1, sc1 = mods[0], mods[1]
    n = x2d.shape[0]
    return _inproj(x2d, sc1, sh1, weights["g_norm1"], weights["w_in"], weights["g_qn"], weights["g_kn"],
                   consts["avg_att"], _token_tile(n, 512))


def kernel(x_prompt, x_sample, c_prompt, c_sample, state_ret, cache_win_k, cache_win_v, w_ada, b_ada, g_norm1, w_in, g_qn, g_kn, g_ret, w_out, g_norm2, w_pq, sub_keys, peer_u, peer_v):
    b, s, d = x_prompt.shape
    db, ds_, _ = x_sample.shape
    depth = w_ada.shape[0]
    assert d == D_MODEL and ds_ == 1 and s % ATT_TILE == 0
    win = cache_win_k.shape[2]
    assert win == MAX_WINDOW

    consts = {
        "avg_att": _block_avg(ATT_W),
        "sel": _slot_select(),
        "selt": _slot_select().T,
        "lg": _ret_log_decay(),
        "slopes": _alibi_slopes(),
    }

    y_p = x_prompt.reshape(b * s, d)
    y_s = x_sample.reshape(db, d)
    outs = {k: [] for k in ("ret_p", "wk_p", "wv_p", "ret_s", "wk_s", "wv_s")}
    for l in range(depth):
        wq_hi, wq_lo = _split_w(w_pq[l])
        sk_hi, sk_lo = _split_w(_pad_sub_keys(sub_keys[l]))
        weights = {
            "g_norm1": g_norm1[l].reshape(1, d), "g_norm2": g_norm2[l].reshape(1, d),
            "w_in": w_in[l].astype(BF16), "w_out": _group_w_out(w_out[l]).astype(BF16),
            "g_qn": g_qn[l].reshape(1, ATT_W), "g_kn": g_kn[l].reshape(1, ATT_W),
            "wq_hi": wq_hi, "wq_lo": wq_lo, "sk_hi": sk_hi, "sk_lo": sk_lo,
            "u_tab": _pack_table(peer_u[l]), "v_tab": _pack_table(peer_v[l]),
        }
        gr = g_ret[l].reshape(1, RET_W)

        mod = _adaln(jnp.concatenate([c_prompt, c_sample], axis=0), w_ada[l], b_ada[l])
        mod6 = jnp.split(mod, 6, axis=-1)
        mods_p = [m[:b].reshape(b, 1, d) for m in mod6]
        mods_s = [m[b:].reshape(1, db, d) for m in mod6]

        rq, rk, rv, rg, aq, ak, av = _project(y_p, mods_p, consts, weights)
        seq = lambda a: a.reshape(b, s, a.shape[-1])
        ret, ret_state = _retention_prompt(consts["lg"], seq(rq), seq(rk), seq(rv), seq(rg), gr, 1024)
        att = _attention_prompt(consts["slopes"], seq(aq), seq(ak), seq(av))
        y_p = _mix_and_peer(y_p, ret.reshape(b * s, RET_W), att.reshape(b * s, ATT_W), mods_p, consts, weights)
        keep = min(MAX_WINDOW, s)
        outs["ret_p"].append(ret_state)
        outs["wk_p"].append(seq(ak)[:, s - keep:].reshape(b, keep, H_ATT, HEAD_DIM))
        outs["wv_p"].append(seq(av)[:, s - keep:].reshape(b, keep, H_ATT, HEAD_DIM))

        rq, rk, rv, rg, aq, ak, av = _project(y_s, mods_s, consts, weights)
        ret, new_state = _retention_sample(consts["lg"], rq, rk, rv, rg, gr, state_ret[l])
        att = _attention_sample(consts["slopes"], aq, ak, av, cache_win_k[l], cache_win_v[l])
        y_s = _mix_and_peer(y_s, ret.reshape(db, RET_W), att.reshape(db, ATT_W), mods_s, consts, weights)
        outs["ret_s"].append(new_state)
        outs["wk_s"].append(ak.reshape(db, 1, H_ATT, HEAD_DIM))
        outs["wv_s"].append(av.reshape(db, 1, H_ATT, HEAD_DIM))

    st = lambda k: jnp.stack(outs[k])
    return (y_p.reshape(b, s, d), y_s.reshape(db, 1, d), st("ret_p"), st("wk_p"), st("wv_p"),
            st("ret_s"), st("wk_s"), st("wv_s"))
```

```python
import functools
import math

import numpy as np
import jax
import jax.numpy as jnp
from jax import lax
from jax.experimental import pallas as pl
from jax.experimental.pallas import tpu as pltpu

F32 = jnp.float32
BF16 = jnp.bfloat16
I32 = jnp.int32

D_MODEL = 1024
HEAD_DIM = 64
RET_W = 512
ATT_W = 512
H_RET = 8
H_ATT = 8
N_PROJ = 4 * RET_W + 3 * ATT_W
RET_CHUNK = 128
DIL_BRANCHES = ((128, 1), (512, 4), (2048, 16))
NW = 128
MAX_WINDOW = 2048
N_KEYS = 128
PEER_HEADS = 8
PEER_TOPK = 16
PEER_SLOTS = PEER_HEADS * PEER_TOPK
EPS = 1e-6

LANES = 128
SUBLANES = 8
ROW_WORDS = D_MODEL // 2
ROW_SUB = ROW_WORDS // LANES
VMEM_LIMIT = 56 * 1024 * 1024
RET_UNROLL = 4
TOK_UNROLL = 16

SDS = jax.ShapeDtypeStruct


def _dot(a, b):
    return jnp.dot(a, b, preferred_element_type=F32)


def _dot_nt(a, b):
    return lax.dot_general(a, b, (((1,), (1,)), ((), ())), preferred_element_type=F32)


def _dot_tn(a, b):
    return lax.dot_general(a, b, (((0,), (0,)), ((), ())), preferred_element_type=F32)


def _split(x):
    hi = x.astype(BF16)
    lo = (x - hi.astype(F32)).astype(BF16)
    return hi, lo


def _silu(x):
    return x * jax.nn.sigmoid(x)


def _params(sem, vmem=VMEM_LIMIT):
    return pltpu.CompilerParams(dimension_semantics=sem, vmem_limit_bytes=vmem)


def _adaln_kernel(c_ref, w_ref, b_ref, o_ref):
    s = _silu(c_ref[...])
    o_ref[...] = jnp.dot(s, w_ref[...], preferred_element_type=F32,
                         precision=lax.Precision.HIGHEST) + b_ref[...]


def _adaln(c, w_ada, b_ada):
    rows = c.shape[0]
    tn = 512
    return pl.pallas_call(
        _adaln_kernel,
        out_shape=SDS((rows, 6 * D_MODEL), F32),
        grid=(6 * D_MODEL // tn,),
        in_specs=[pl.BlockSpec((rows, D_MODEL), lambda j: (0, 0)),
                  pl.BlockSpec((D_MODEL, tn), lambda j: (0, j)),
                  pl.BlockSpec((1, tn), lambda j: (0, j))],
        out_specs=pl.BlockSpec((rows, tn), lambda j: (0, j)),
        compiler_params=_params(("arbitrary",)),
        name="adaln",
    )(c, w_ada, b_ada.reshape(1, -1))


def _inproj_kernel(x_ref, sc_ref, sh_ref, g1_ref, w_ref, gq_ref, gk_ref, bd_ref,
                   rq_ref, rk_ref, rv_ref, rg_ref, aq_ref, ak_ref, av_ref):
    x = x_ref[...]
    ms = jnp.mean(x * x, axis=-1, keepdims=True)
    h = x * lax.rsqrt(ms + EPS) * g1_ref[...]
    h = h * (1.0 + sc_ref[...]) + sh_ref[...]
    p = _dot(h.astype(BF16), w_ref[...])
    scale = HEAD_DIM ** -0.5
    rq_ref[...] = p[:, 0:RET_W].astype(BF16)
    rk_ref[...] = (p[:, RET_W:2 * RET_W] * scale).astype(BF16)
    rv_ref[...] = p[:, 2 * RET_W:3 * RET_W].astype(BF16)
    rg_ref[...] = p[:, 3 * RET_W:4 * RET_W].astype(BF16)
    o = 4 * RET_W

    def head_norm(a, g):
        msq = _dot((a * a).astype(BF16), bd_ref[...])
        return a * lax.rsqrt(msq + EPS) * g

    aq_ref[...] = head_norm(p[:, o:o + ATT_W], gq_ref[...]) * scale
    ak_ref[...] = head_norm(p[:, o + ATT_W:o + 2 * ATT_W], gk_ref[...])
    av_ref[...] = p[:, o + 2 * ATT_W:o + 3 * ATT_W]


def _inproj(x2d, sc, sh, g1, w_in_bf, gq, gk, bd, tb):
    n = x2d.shape[0]
    groups, rows, _ = sc.shape
    tiles_per_group = n // tb // groups
    mod_spec = pl.BlockSpec((None, rows, D_MODEL), lambda i: (i // tiles_per_group, 0, 0))
    tok = lambda w: pl.BlockSpec((tb, w), lambda i: (i, 0))
    full = lambda a: pl.BlockSpec(a.shape, lambda i: (0,) * a.ndim)
    outs = [SDS((n, RET_W), BF16)] * 4 + [SDS((n, ATT_W), F32)] * 3
    return pl.pallas_call(
        _inproj_kernel,
        out_shape=outs,
        grid=(n // tb,),
        in_specs=[tok(D_MODEL), mod_spec, mod_spec, full(g1), full(w_in_bf), full(gq), full(gk), full(bd)],
        out_specs=[tok(RET_W)] * 4 + [tok(ATT_W)] * 3,
        compiler_params=_params(("arbitrary",)),
        name="inproj",
    )(x2d, sc, sh, g1, w_in_bf, gq, gk, bd)


def _ret_kernel(lg_ref, q_ref, k_ref, v_ref, rg_ref, gr_ref, o_ref, st_ref, s_scr, *, nchunk):
    hp = pl.program_id(1)
    t = pl.program_id(2)
    C = RET_CHUNK
    lane = lax.broadcasted_iota(I32, (1, LANES), 1)
    m0 = lane < HEAD_DIM
    lg0 = lg_ref[2 * hp]
    lg1 = lg_ref[2 * hp + 1]
    lgv = jnp.where(m0, lg0, lg1)

    @pl.when(t == 0)
    def _():
        s_scr[...] = jnp.zeros_like(s_scr)

    ii = lax.broadcasted_iota(I32, (C, C), 0)
    jj = lax.broadcasted_iota(I32, (C, C), 1)
    diff = ii - jj
    dpos = jnp.maximum(diff, 0).astype(F32)
    dec01 = jnp.concatenate([jnp.where(diff >= 0, jnp.exp(dpos * lg0), 0.0),
                             jnp.where(diff >= 0, jnp.exp(dpos * lg1), 0.0)], axis=0)

    def head_mean(x):
        s0 = jnp.sum(jnp.where(m0, x, 0.0), axis=-1, keepdims=True)
        s1 = jnp.sum(jnp.where(m0, 0.0, x), axis=-1, keepdims=True)
        return jnp.where(m0, s0, s1) * (1.0 / HEAD_DIM)
    ri = lax.broadcasted_iota(I32, (C, LANES), 0).astype(F32)
    w_start = jnp.exp((ri + 1.0) * lgv)
    w_end = jnp.exp((float(C) - 1.0 - ri) * lgv)
    chunk_decay = jnp.exp(float(C) * lgv)
    rr = lax.broadcasted_iota(I32, (LANES, LANES), 0)
    cc = lax.broadcasted_iota(I32, (LANES, LANES), 1)
    same_head = (rr < HEAD_DIM) == (cc < HEAD_DIM)
    gr = gr_ref[...]

    def chunk(c, state):
        rows = pl.ds(pl.multiple_of(c * C, C), C)
        q = q_ref[rows, :]
        k = k_ref[rows, :]
        v = v_ref[rows, :]
        qf = q.astype(F32)
        kf = k.astype(F32)
        q01 = jnp.concatenate([jnp.where(m0, qf, 0.0), jnp.where(m0, 0.0, qf)], axis=0).astype(BF16)
        p01 = (_dot_nt(q01, k) * dec01).astype(BF16)
        pv = _dot(p01, v)
        o_intra = jnp.where(m0, pv[0:C, :], pv[C:, :])
        o_inter = _dot((qf * w_start).astype(BF16), state.astype(BF16))
        o = o_intra + o_inter
        kv = _dot_tn((kf * w_end).astype(BF16), v)
        mu = head_mean(o)
        xc = o - mu
        var = head_mean(xc * xc)
        y = xc * lax.rsqrt(var + EPS) * gr
        y = y * _silu(rg_ref[rows, :].astype(F32))
        o_ref[rows, :] = y.astype(BF16)
        return chunk_decay * state + jnp.where(same_head, kv, 0.0)

    def chunk_group(i, state):
        for u in range(RET_UNROLL):
            state = chunk(i * RET_UNROLL + u, state)
        return state

    s_scr[...] = lax.fori_loop(0, nchunk // RET_UNROLL, chunk_group, s_scr[...])

    @pl.when(t == pl.num_programs(2) - 1)
    def _():
        state = s_scr[...]
        st_ref[0] = state[0:HEAD_DIM, 0:HEAD_DIM]
        st_ref[1] = state[HEAD_DIM:, HEAD_DIM:]


def _retention_prompt(lg, rq, rk, rv, rg, g_ret, cb):
    b, s, _ = rq.shape
    nchunk = cb // RET_CHUNK
    blk = pl.BlockSpec((None, cb, LANES), lambda bi, hp, t, lg_: (bi, t, hp))
    grid_spec = pltpu.PrefetchScalarGridSpec(
        num_scalar_prefetch=1,
        grid=(b, H_RET // 2, s // cb),
        in_specs=[blk, blk, blk, blk,
                  pl.BlockSpec((1, LANES), lambda bi, hp, t, lg_: (0, hp))],
        out_specs=[blk, pl.BlockSpec((None, 2, HEAD_DIM, HEAD_DIM), lambda bi, hp, t, lg_: (bi, hp, 0, 0))],
        scratch_shapes=[pltpu.VMEM((LANES, LANES), F32)],
    )
    return pl.pallas_call(
        functools.partial(_ret_kernel, nchunk=nchunk),
        out_shape=[SDS((b, s, RET_W), BF16), SDS((b, H_RET, HEAD_DIM, HEAD_DIM), F32)],
        grid_spec=grid_spec,
        compiler_params=_params(("arbitrary", "arbitrary", "arbitrary")),
        name="retention_prompt",
    )(lg, rq, rk, rv, rg, g_ret)


ATT_TILE = MAX_WINDOW


ATT_UNROLL = 8


def _att_kernel(sl_ref, q_ref, k_ref, v_ref, o_ref, kk, vv, *stats):
    hp = pl.program_id(1)
    t = pl.program_id(2)
    TS = ATT_TILE
    nbr = len(DIL_BRANCHES)
    m_br, l_br, acc_br = stats[:nbr], stats[nbr:2 * nbr], stats[2 * nbr:]
    cur = pl.multiple_of((t & 1) * TS, TS)

    @pl.when(t == 0)
    def _():
        kk[TS:2 * TS, :] = jnp.zeros((TS, LANES), F32)
        vv[TS:2 * TS, :] = jnp.zeros((TS, LANES), F32)

    kk[pl.ds(cur, TS), :] = k_ref[...]
    vv[pl.ds(cur, TS), :] = v_ref[...]

    lane = lax.broadcasted_iota(I32, (1, LANES), 1)
    m0 = lane < HEAD_DIM
    head_lanes = (m0, jnp.logical_not(m0))
    ii = lax.broadcasted_iota(I32, (NW, 2 * NW), 0)
    jj = lax.broadcasted_iota(I32, (NW, 2 * NW), 1)
    off = ii + NW - jj
    band = (off >= 0) & (off <= NW)
    off_f = off.astype(F32)
    ones_h = [jnp.broadcast_to(jnp.where(hl, 1.0, 0.0).astype(BF16), (2 * NW, LANES)) for hl in head_lanes]

    for br, (_, dil) in enumerate(DIL_BRANCHES):
        nblk = TS // (NW * dil)
        shift = int(math.log2(nblk))
        bias = [jnp.where(band, -(sl_ref[2 * hp + hh] * float(dil)) * off_f, -jnp.inf) for hh in range(2)]

        def one_block(bs, dil=dil, nblk=nblk, shift=shift, br=br, bias=bias):
            r = lax.shift_right_logical(bs, shift)
            blk = bs & (nblk - 1)
            qs = r + dil * NW * blk
            q_rows = pl.ds(qs, NW, stride=dil)
            own = pl.ds(cur + qs, NW, stride=dil)
            prev = pl.ds((cur + qs - dil * NW) & (2 * TS - 1), NW, stride=dil)
            qb = q_ref[q_rows, :]
            kb = jnp.concatenate([kk[prev, :], kk[own, :]], axis=0).astype(BF16)
            vb = jnp.concatenate([vv[prev, :], vv[own, :]], axis=0)
            first_key = jnp.where(jnp.logical_or(t > 0, blk > 0), 0, NW)
            started = jj >= first_key
            res = None
            mx = []
            for hh in range(2):
                qh = jnp.where(head_lanes[hh], qb, 0.0).astype(BF16)
                s = jnp.where(started, _dot_nt(qh, kb) + bias[hh], -jnp.inf)
                mh = jnp.max(s, axis=-1, keepdims=True)
                p = jnp.exp(s - mh).astype(BF16)
                vh = jnp.concatenate([jnp.where(head_lanes[hh], vb, 0.0).astype(BF16), ones_h[hh]], axis=1)
                part = _dot(p, vh)
                res = part if res is None else res + part
                mx.append(mh)
            m_br[br][q_rows, :] = jnp.where(m0, mx[0], mx[1])
            acc_br[br][q_rows, :] = res[:, :LANES]
            l_br[br][q_rows, :] = res[:, LANES:]

        def body(i, carry, one_block=one_block):
            for u in range(ATT_UNROLL):
                one_block(i * ATT_UNROLL + u)
            return carry

        lax.fori_loop(0, TS // NW // ATT_UNROLL, body, 0)

    rows_per_step = 256

    def merge(i, carry):
        rows = pl.ds(pl.multiple_of(i * rows_per_step, rows_per_step), rows_per_step)
        ms = [m[rows, :] for m in m_br]
        m_all = functools.reduce(jnp.maximum, ms)
        ws = [jnp.exp(m - m_all) for m in ms]
        num = sum(w * a[rows, :] for w, a in zip(ws, acc_br))
        den = sum(w * l[rows, :] for w, l in zip(ws, l_br))
        o_ref[rows, :] = (num / den).astype(BF16)
        return carry

    lax.fori_loop(0, TS // rows_per_step, merge, 0)


def _attention_prompt(slopes, aq, ak, av):
    b, s, _ = aq.shape
    ts = ATT_TILE
    blk = pl.BlockSpec((None, ts, LANES), lambda bi, hp, t, sl: (bi, t, hp))
    grid_spec = pltpu.PrefetchScalarGridSpec(
        num_scalar_prefetch=1,
        grid=(b, H_ATT // 2, s // ts),
        in_specs=[blk, blk, blk],
        out_specs=blk,
        scratch_shapes=[pltpu.VMEM((2 * ts, LANES), F32)] * 2
        + [pltpu.VMEM((ts, LANES), F32)] * (3 * len(DIL_BRANCHES)),
    )
    return pl.pallas_call(
        _att_kernel,
        out_shape=SDS((b, s, ATT_W), BF16),
        grid_spec=grid_spec,
        compiler_params=_params(("arbitrary", "arbitrary", "arbitrary")),
        name="attention_prompt",
    )(slopes, aq, ak, av)


def _sret_kernel(lg_ref, q_ref, k_ref, v_ref, rg_ref, gr_ref, st_ref, o_ref, sn_ref):
    q = q_ref[...].astype(F32)
    k = k_ref[...].astype(F32)
    v = v_ref[...].astype(F32)
    rg = rg_ref[...].astype(F32)
    gr = gr_ref[...]
    r_i = lax.broadcasted_iota(I32, (HEAD_DIM, HEAD_DIM), 0)
    c_i = lax.broadcasted_iota(I32, (HEAD_DIM, HEAD_DIM), 1)
    eye = r_i == c_i
    outs = []
    for h in range(H_RET):
        sl = slice(h * HEAD_DIM, (h + 1) * HEAD_DIM)
        qh, kh, vh = q[:, sl], k[:, sl], v[:, sl]
        gamma = jnp.exp(jnp.full((1, HEAD_DIM), lg_ref[h], F32))
        q_col = jnp.sum(jnp.where(eye, qh, 0.0), axis=1, keepdims=True)
        k_col = jnp.sum(jnp.where(eye, kh, 0.0), axis=1, keepdims=True)
        state = st_ref[h]
        o_inter = gamma * jnp.sum(q_col * state, axis=0, keepdims=True)
        o_intra = jnp.sum(qh * kh, axis=-1, keepdims=True) * vh
        sn_ref[h] = gamma * state + k_col * vh
        o = o_intra + o_inter
        mu = jnp.mean(o, axis=-1, keepdims=True)
        xc = o - mu
        var = jnp.mean(xc * xc, axis=-1, keepdims=True)
        outs.append(xc * lax.rsqrt(var + EPS) * gr[:, sl] * _silu(rg[:, sl]))
    o_ref[...] = jnp.concatenate(outs, axis=-1).astype(BF16)


def _retention_sample(lg, rq, rk, rv, rg, g_ret, state):
    db = rq.shape[0]
    row = pl.BlockSpec((None, 1, RET_W), lambda i: (i, 0, 0))
    st = pl.BlockSpec((None, H_RET, HEAD_DIM, HEAD_DIM), lambda i: (i, 0, 0, 0))
    r3 = lambda a: a.reshape(db, 1, RET_W)
    return pl.pallas_call(
        _sret_kernel,
        out_shape=[SDS((db, 1, RET_W), BF16), SDS(state.shape, F32)],
        grid=(db,),
        in_specs=[pl.BlockSpec(memory_space=pltpu.SMEM), row, row, row, row,
                  pl.BlockSpec((1, RET_W), lambda i: (0, 0)), st],
        out_specs=[row, st],
        compiler_params=_params(("arbitrary",)),
        name="retention_sample",
    )(lg, r3(rq), r3(rk), r3(rv), r3(rg), g_ret, state)


def _satt_kernel(sl_ref, qt_ref, knt_ref, vnt_ref, kt_ref, vt_ref, o_ref):
    win = kt_ref.shape[-1]
    dist = win - lax.broadcasted_iota(I32, (1, win), 1)
    count = jnp.zeros((1, win), F32)
    for window, dil in DIL_BRANCHES:
        reads = ((dist & (dil - 1)) == 0) & (dist <= window)
        count = count + jnp.where(reads, 1.0, 0.0)
    dist_f = dist.astype(F32)
    nbr = float(len(DIL_BRANCHES))
    for h in range(H_ATT):
        q = qt_ref[:, h:h + 1]
        s = jnp.sum(kt_ref[h] * q, axis=0, keepdims=True) - sl_ref[h] * dist_f
        s = jnp.where(count > 0.0, s, -jnp.inf)
        s_new = jnp.sum(q * knt_ref[:, h:h + 1], axis=0, keepdims=True)
        m = jnp.maximum(jnp.max(s, axis=-1, keepdims=True), s_new)
        p = count * jnp.exp(s - m)
        p_new = nbr * jnp.exp(s_new - m)
        den = jnp.sum(p, axis=-1, keepdims=True) + p_new
        num = jnp.sum(vt_ref[h] * p, axis=-1, keepdims=True) + p_new * vnt_ref[:, h:h + 1]
        o_ref[:, h:h + 1] = num / den


def _attention_sample(slopes, aq, ak, av, win_k, win_v):
    db = aq.shape[0]
    win = win_k.shape[1]
    col = pl.BlockSpec((None, HEAD_DIM, H_ATT), lambda i: (i, 0, 0))
    cache = pl.BlockSpec((None, H_ATT, HEAD_DIM, win), lambda i: (i, 0, 0, 0))
    cols = lambda a: jnp.transpose(a.reshape(db, H_ATT, HEAD_DIM), (0, 2, 1))
    keys_minor = lambda c: jnp.transpose(c, (0, 2, 3, 1))
    out = pl.pallas_call(
        _satt_kernel,
        out_shape=SDS((db, HEAD_DIM, H_ATT), F32),
        grid=(db,),
        in_specs=[pl.BlockSpec(memory_space=pltpu.SMEM), col, col, col, cache, cache],
        out_specs=col,
        compiler_params=_params(("arbitrary",)),
        name="attention_sample",
    )(slopes, cols(aq), cols(ak), cols(av), keys_minor(win_k), keys_minor(win_v))
    return jnp.transpose(out, (0, 2, 1)).reshape(db, ATT_W).astype(BF16)


def _outproj_kernel(x_ref, ret_ref, att_ref, wo_ref, gt1_ref, g2_ref, sc2_ref, sh2_ref, wqh_ref, wql_ref,
                    x1_ref, h2_ref, qp_ref):
    mix = _dot(ret_ref[...], wo_ref[0:RET_W, :]) + _dot(att_ref[...], wo_ref[RET_W:, :])
    x1 = x_ref[...] + gt1_ref[...] * mix
    x1_ref[...] = x1
    ms = jnp.mean(x1 * x1, axis=-1, keepdims=True)
    h2 = x1 * lax.rsqrt(ms + EPS) * g2_ref[...]
    h2 = h2 * (1.0 + sc2_ref[...]) + sh2_ref[...]
    h2_ref[...] = h2
    hi, lo = _split(h2)
    wqh = wqh_ref[...]
    qp_ref[...] = _dot(hi, wqh) + (_dot(lo, wqh) + _dot(hi, wql_ref[...]))


def _outproj(x2d, ret, att, wo_bf, gt1, g2, sc2, sh2, wq_hi, wq_lo, tb):
    n = x2d.shape[0]
    groups, rows, _ = gt1.shape
    tiles_per_group = n // tb // groups
    mod_spec = pl.BlockSpec((None, rows, D_MODEL), lambda i: (i // tiles_per_group, 0, 0))
    tok = lambda w: pl.BlockSpec((tb, w), lambda i: (i, 0))
    full = lambda a: pl.BlockSpec(a.shape, lambda i: (0,) * a.ndim)
    return pl.pallas_call(
        _outproj_kernel,
        out_shape=[SDS((n, D_MODEL), F32)] * 3,
        grid=(n // tb,),
        in_specs=[tok(D_MODEL), tok(RET_W), tok(ATT_W), full(wo_bf), mod_spec, full(g2), mod_spec, mod_spec,
                  full(wq_hi), full(wq_lo)],
        out_specs=[tok(D_MODEL)] * 3,
        compiler_params=_params(("arbitrary",)),
        name="outproj",
    )(x2d, ret, att, wo_bf, gt1, g2, sc2, sh2, wq_hi, wq_lo)


def _top16(sc, ids=None):
    if ids is None:
        ids = lax.broadcasted_iota(I32, sc.shape, 0)
    ids = ids.astype(F32)
    vals, idxs = [], []
    for _ in range(PEER_TOPK):
        m = jnp.max(sc, axis=0, keepdims=True)
        idx = jnp.min(jnp.where(sc == m, ids, jnp.inf), axis=0, keepdims=True)
        vals.append(m)
        idxs.append(idx)
        sc = jnp.where(ids == idx, -jnp.inf, sc)
    return jnp.concatenate(vals, axis=0), jnp.concatenate(idxs, axis=0).astype(I32)


def _pair_candidates(s1, s2):
    k = PEER_TOPK
    half = k // 2
    pieces = [s1[0:1, :] + s2]
    pieces += [s1[a:a + 1, :] + s2[0:half, :] for a in range(1, half)]
    pieces.append(s1[half:, :] + s2[0:1, :])
    cand = jnp.concatenate(pieces, axis=0)
    r = lax.broadcasted_iota(I32, cand.shape, 0)
    mid = r - k
    a_mid = 1 + lax.shift_right_logical(mid, 3)
    b_mid = mid & (half - 1)
    tail_start = k + half * (half - 1)
    pos = jnp.where(r < k, r, jnp.where(r < tail_start, a_mid * k + b_mid, (half + r - tail_start) * k))
    reachable = jnp.logical_or(jnp.logical_or(r < k, r >= tail_start), (a_mid + 1) * (b_mid + 1) <= k)
    return jnp.where(reachable, cand, -jnp.inf), pos


TOPK_HEADS = 8


def _topk_one_head(q, skh_ref, skl_ref, hh):
    qh, ql = _split(q)

    def scores(c):
        kh = skh_ref[hh, c]
        return _dot_nt(kh, qh) + (_dot_nt(kh, ql) + _dot_nt(skl_ref[hh, c], qh))

    s1, i1 = _top16(scores(0))
    s2, i2 = _top16(scores(1))
    top_s, pos = _top16(*_pair_candidates(s1, s2))
    a_sel = lax.shift_right_logical(pos, 4)
    b_sel = pos & (PEER_TOPK - 1)
    io = lax.broadcasted_iota(I32, i1.shape, 0)
    rows = []
    for r in range(PEER_TOPK):
        ia = jnp.sum(jnp.where(io == a_sel[r:r + 1, :], i1, 0), axis=0, keepdims=True)
        ib = jnp.sum(jnp.where(io == b_sel[r:r + 1, :], i2, 0), axis=0, keepdims=True)
        rows.append(ia * N_KEYS + ib)
    ex = jnp.exp(top_s - top_s[0:1, :])
    return jnp.concatenate(rows, axis=0) * ROW_SUB, ex / jnp.sum(ex, axis=0, keepdims=True)


def _topk_kernel(q_ref, skh_ref, skl_ref, e_ref, g_ref, e_scr, g_scr):
    step = pl.program_id(1)
    for hh in range(TOPK_HEADS):
        e, g = _topk_one_head(q_ref[:, LANES * hh:LANES * (hh + 1)], skh_ref, skl_ref, hh)
        slot_rows = pl.ds(pl.multiple_of((step * TOPK_HEADS + hh) * PEER_TOPK, PEER_TOPK), PEER_TOPK)
        e_scr[slot_rows, :] = e
        g_scr[slot_rows, :] = g

    @pl.when(step == pl.num_programs(1) - 1)
    def _():
        e_ref[...] = e_scr[...].T
        g_ref[...] = g_scr[...].T


def _peer_topk(qp, sk_hi, sk_lo, tt):
    n = qp.shape[0]
    out_blk = pl.BlockSpec((tt, PEER_SLOTS), lambda i, h: (i, 0))
    sk_blk = pl.BlockSpec((TOPK_HEADS, 2, N_KEYS, LANES), lambda i, h: (h, 0, 0, 0))
    return pl.pallas_call(
        _topk_kernel,
        out_shape=[SDS((n, PEER_SLOTS), I32), SDS((n, PEER_SLOTS), F32)],
        grid=(n // tt, PEER_HEADS // TOPK_HEADS),
        in_specs=[pl.BlockSpec((tt, TOPK_HEADS * LANES), lambda i, h: (i, h)), sk_blk, sk_blk],
        out_specs=[out_blk, out_blk],
        scratch_shapes=[pltpu.VMEM((PEER_SLOTS, tt), I32), pltpu.VMEM((PEER_SLOTS, tt), F32)],
        compiler_params=_params(("arbitrary", "arbitrary")),
        name="peer_topk",
    )(qp, sk_hi, sk_lo)


def _gather_rows(idx_ref, tab_ref, t):
    rows = []
    for k in range(PEER_SLOTS):
        e = pl.multiple_of(idx_ref[t, k], ROW_SUB)
        rows.append(tab_ref[pl.ds(e, ROW_SUB), :])
    return pltpu.bitcast(jnp.concatenate(rows, axis=0), BF16)


def _diag_mask():
    sub = lax.broadcasted_iota(I32, (SUBLANES, D_MODEL), 0)
    col = lax.broadcasted_iota(I32, (SUBLANES, D_MODEL), 1)
    return (col & (SUBLANES - 1)) == sub


def _for_each_token(tb, one_token):
    def group(i, carry):
        for u in range(TOK_UNROLL):
            one_token(i * TOK_UNROLL + u)
        return carry

    lax.fori_loop(0, tb // TOK_UNROLL, group, 0)


def _peer_u_kernel(idx_ref, tab_ref, h2_ref, g_ref, sel_ref, w_ref, xbuf, rall):
    tb = h2_ref.shape[0]
    for j in range(SUBLANES):
        xbuf[pl.ds(j, tb, stride=SUBLANES), :] = h2_ref[:, LANES * j:LANES * (j + 1)]
    diag = _diag_mask()

    def tok(t):
        rows = _gather_rows(idx_ref, tab_ref, t)
        xt = xbuf[pl.ds(pl.multiple_of(t * SUBLANES, SUBLANES), SUBLANES), :].astype(BF16)
        r = _dot_nt(xt, rows)
        rall[pl.ds(t, 1), :] = jnp.sum(jnp.where(diag, r, 0.0), axis=0, keepdims=True)

    _for_each_token(tb, tok)
    hi, lo = _split(rall[...])
    sel = sel_ref[...]
    a = _dot(hi, sel) + _dot(lo, sel)
    gelu = 0.5 * a * (1.0 + lax.erf(a * (2.0 ** -0.5)))
    w_ref[...] = g_ref[...] * gelu


def _peer_v_kernel(idx_ref, tab_ref, w_ref, x1_ref, gt2_ref, selt_ref, y_ref, wrep, ybuf):
    tb = x1_ref.shape[0]
    wrep[...] = _dot(w_ref[...].astype(BF16), selt_ref[...])
    diag = _diag_mask()

    def tok(t):
        rows = _gather_rows(idx_ref, tab_ref, t)
        wt = jnp.where(diag, wrep[pl.ds(t, 1), :], 0.0).astype(BF16)
        ybuf[pl.ds(pl.multiple_of(t * SUBLANES, SUBLANES), SUBLANES), :] = _dot(wt, rows)

    _for_each_token(tb, tok)
    for j in range(SUBLANES):
        cols = slice(LANES * j, LANES * (j + 1))
        y_ref[:, cols] = x1_ref[:, cols] + gt2_ref[:, cols] * ybuf[pl.ds(j, tb, stride=SUBLANES), :]


def _peer_u(idx, tab, h2, g, sel, tb):
    n = h2.shape[0]
    tok = lambda w: pl.BlockSpec((tb, w), lambda i: (i, 0))
    return pl.pallas_call(
        _peer_u_kernel,
        out_shape=SDS((n, PEER_SLOTS), F32),
        grid=(n // tb,),
        in_specs=[pl.BlockSpec((tb, PEER_SLOTS), lambda i: (i, 0), memory_space=pltpu.SMEM),
                  pl.BlockSpec(memory_space=pltpu.VMEM), tok(D_MODEL), tok(PEER_SLOTS),
                  pl.BlockSpec(sel.shape, lambda i: (0, 0))],
        out_specs=tok(PEER_SLOTS),
        scratch_shapes=[pltpu.VMEM((tb * SUBLANES, LANES), F32), pltpu.VMEM((tb, D_MODEL), F32)],
        compiler_params=_params(("arbitrary",)),
        name="peer_u",
    )(idx, tab, h2, g, sel)


def _peer_v(idx, tab, w, x1, gt2, selt, tb):
    n = x1.shape[0]
    groups, rows, _ = gt2.shape
    tiles_per_group = n // tb // groups
    tok = lambda wd: pl.BlockSpec((tb, wd), lambda i: (i, 0))
    return pl.pallas_call(
        _peer_v_kernel,
        out_shape=SDS((n, D_MODEL), F32),
        grid=(n // tb,),
        in_specs=[pl.BlockSpec((tb, PEER_SLOTS), lambda i: (i, 0), memory_space=pltpu.SMEM),
                  pl.BlockSpec(memory_space=pltpu.VMEM), tok(PEER_SLOTS), tok(D_MODEL),
                  pl.BlockSpec((None, rows, D_MODEL), lambda i: (i // tiles_per_group, 0, 0)),
                  pl.BlockSpec(selt.shape, lambda i: (0, 0))],
        out_specs=tok(D_MODEL),
        scratch_shapes=[pltpu.VMEM((tb, D_MODEL), F32), pltpu.VMEM((tb * SUBLANES, LANES), F32)],
        compiler_params=_params(("arbitrary",)),
        name="peer_v",
    )(idx, tab, w, x1, gt2, selt)


def _pack_table(tab):
    e = tab.shape[0]
    bits = lax.bitcast_convert_type(tab.astype(BF16), jnp.uint16).astype(jnp.uint32)
    bits = bits.reshape(e, ROW_SUB, 2, LANES)
    word = bits[:, :, 0, :] | (bits[:, :, 1, :] << 16)
    return lax.bitcast_convert_type(word, I32).reshape(e * ROW_SUB, LANES)


def _block_avg(width):
    i = np.arange(width) // HEAD_DIM
    return jnp.asarray((i[:, None] == i[None, :]).astype(np.float32) / HEAD_DIM, BF16)


def _slot_select():
    col = np.arange(D_MODEL) // SUBLANES
    return jnp.asarray((col[:, None] == np.arange(PEER_SLOTS)[None, :]).astype(np.float32), BF16)


def _ret_log_decay():
    return jnp.log1p(-jnp.exp2(-5.0 - jnp.arange(H_RET, dtype=F32)))


def _alibi_slopes():
    return jnp.exp2(-8.0 * (jnp.arange(H_ATT, dtype=F32) + 1.0) / H_ATT)


def _pad_sub_keys(sub_keys):
    z = jnp.zeros_like(sub_keys[:, 0])
    return jnp.stack([jnp.concatenate([sub_keys[:, 0], z], axis=-1),
                      jnp.concatenate([z, sub_keys[:, 1]], axis=-1)], axis=1)


def _group_w_out(w):
    w4 = w.reshape(H_RET, 2, HEAD_DIM, D_MODEL)
    return jnp.concatenate([w4[:, 0].reshape(RET_W, D_MODEL), w4[:, 1].reshape(ATT_W, D_MODEL)], axis=0)


def _split_w(w):
    hi = w.astype(BF16)
    return hi, (w - hi.astype(F32)).astype(BF16)


def _token_tile(n, pref):
    return pref if n % pref == 0 else n


def _peer(h2, qp, x1, gt2, consts, weights):
    n = h2.shape[0]
    idx, gate = _peer_topk(qp, weights["sk_hi"], weights["sk_lo"], _token_tile(n, 128))
    tb = _token_tile(n, 256)
    w = _peer_u(idx, weights["u_tab"], h2, gate, consts["sel"], tb)
    return _peer_v(idx, weights["v_tab"], w, x1, gt2, consts["selt"], tb)


def _mix_and_peer(x2d, ret, att, mods, consts, weights):
    sh1, sc1, gt1, sh2, sc2, gt2 = mods
    n = x2d.shape[0]
    x1, h2, qp = _outproj(x2d, ret, att, weights["w_out"], gt1, weights["g_norm2"], sc2, sh2,
                          weights["wq_hi"], weights["wq_lo"], _token_tile(n, 512))
    return _peer(h2, qp, x1, gt2, consts, weights)


def _project(x2d, mods, consts, weights):
    sh1, sc1 = mods[0], mods[1]
    n = x2d.shape[0]
    return _inproj(x2d, sc1, sh1, weights["g_norm1"], weights["w_in"], weights["g_qn"], weights["g_kn"],
                   consts["avg_att"], _token_tile(n, 512))


def kernel(x_prompt, x_sample, c_prompt, c_sample, state_ret, cache_win_k, cache_win_v, w_ada, b_ada, g_norm1, w_in, g_qn, g_kn, g_ret, w_out, g_norm2, w_pq, sub_keys, peer_u, peer_v):
    b, s, d = x_prompt.shape
    db, ds_, _ = x_sample.shape
    depth = w_ada.shape[0]
    assert d == D_MODEL and ds_ == 1 and s % ATT_TILE == 0
    win = cache_win_k.shape[2]
    assert win == MAX_WINDOW

    consts = {
        "avg_att": _block_avg(ATT_W),
        "sel": _slot_select(),
        "selt": _slot_select().T,
        "lg": _ret_log_decay(),
        "slopes": _alibi_slopes(),
    }

    y_p = x_prompt.reshape(b * s, d)
    y_s = x_sample.reshape(db, d)
    outs = {k: [] for k in ("ret_p", "wk_p", "wv_p", "ret_s", "wk_s", "wv_s")}
    for l in range(depth):
        wq_hi, wq_lo = _split_w(w_pq[l])
        sk_hi, sk_lo = _split_w(_pad_sub_keys(sub_keys[l]))
        weights = {
            "g_norm1": g_norm1[l].reshape(1, d), "g_norm2": g_norm2[l].reshape(1, d),
            "w_in": w_in[l].astype(BF16), "w_out": _group_w_out(w_out[l]).astype(BF16),
            "g_qn": g_qn[l].reshape(1, ATT_W), "g_kn": g_kn[l].reshape(1, ATT_W),
            "wq_hi": wq_hi, "wq_lo": wq_lo, "sk_hi": sk_hi, "sk_lo": sk_lo,
            "u_tab": _pack_table(peer_u[l]), "v_tab": _pack_table(peer_v[l]),
        }
        gr = g_ret[l].reshape(1, RET_W)

        mod = _adaln(jnp.concatenate([c_prompt, c_sample], axis=0), w_ada[l], b_ada[l])
        mod6 = jnp.split(mod, 6, axis=-1)
        mods_p = [m[:b].reshape(b, 1, d) for m in mod6]
        mods_s = [m[b:].reshape(1, db, d) for m in mod6]

        rq, rk, rv, rg, aq, ak, av = _project(y_p, mods_p, consts, weights)
        seq = lambda a: a.reshape(b, s, a.shape[-1])
        ret, ret_state = _retention_prompt(consts["lg"], seq(rq), seq(rk), seq(rv), seq(rg), gr, 1024)
        att = _attention_prompt(consts["slopes"], seq(aq), seq(ak), seq(av))
        y_p = _mix_and_peer(y_p, ret.reshape(b * s, RET_W), att.reshape(b * s, ATT_W), mods_p, consts, weights)
        keep = min(MAX_WINDOW, s)
        outs["ret_p"].append(ret_state)
        outs["wk_p"].append(seq(ak)[:, s - keep:].reshape(b, keep, H_ATT, HEAD_DIM))
        outs["wv_p"].append(seq(av)[:, s - keep:].reshape(b, keep, H_ATT, HEAD_DIM))

        rq, rk, rv, rg, aq, ak, av = _project(y_s, mods_s, consts, weights)
        ret, new_state = _retention_sample(consts["lg"], rq, rk, rv, rg, gr, state_ret[l])
        att = _attention_sample(consts["slopes"], aq, ak, av, cache_win_k[l], cache_win_v[l])
        y_s = _mix_and_peer(y_s, ret.reshape(db, RET_W), att.reshape(db, ATT_W), mods_s, consts, weights)
        outs["ret_s"].append(new_state)
        outs["wk_s"].append(ak.reshape(db, 1, H_ATT, HEAD_DIM))
        outs["wv_s"].append(av.reshape(db, 1, H_ATT, HEAD_DIM))

    st = lambda k: jnp.stack(outs[k])
    return (y_p.reshape(b, s, d), y_s.reshape(db, 1, d), st("ret_p"), st("wk_p"), st("wv_p"),
            st("ret_s"), st("wk_s"), st("wv_s"))
```

```python
import functools
import math

import numpy as np
import jax
import jax.numpy as jnp
from jax import lax
from jax.experimental import pallas as pl
from jax.experimental.pallas import tpu as pltpu

F32 = jnp.float32
BF16 = jnp.bfloat16
I32 = jnp.int32

D_MODEL = 1024
HEAD_DIM = 64
RET_W = 512
ATT_W = 512
H_RET = 8
H_ATT = 8
N_PROJ = 4 * RET_W + 3 * ATT_W
RET_CHUNK = 128
DIL_BRANCHES = ((128, 1), (512, 4), (2048, 16))
NW = 128
MAX_WINDOW = 2048
N_KEYS = 128
PEER_HEADS = 8
PEER_TOPK = 16
PEER_SLOTS = PEER_HEADS * PEER_TOPK
EPS = 1e-6

LANES = 128
SUBLANES = 8
ROW_WORDS = D_MODEL // 2
ROW_SUB = ROW_WORDS // LANES
VMEM_LIMIT = 56 * 1024 * 1024
RET_UNROLL = 4
TOK_UNROLL = 32

SDS = jax.ShapeDtypeStruct


def _dot(a, b):
    return jnp.dot(a, b, preferred_element_type=F32)


def _dot_nt(a, b):
    return lax.dot_general(a, b, (((1,), (1,)), ((), ())), preferred_element_type=F32)


def _dot_tn(a, b):
    return lax.dot_general(a, b, (((0,), (0,)), ((), ())), preferred_element_type=F32)


def _split(x):
    hi = x.astype(BF16)
    lo = (x - hi.astype(F32)).astype(BF16)
    return hi, lo


def _silu(x):
    return x * jax.nn.sigmoid(x)


def _params(sem, vmem=VMEM_LIMIT):
    return pltpu.CompilerParams(dimension_semantics=sem, vmem_limit_bytes=vmem)


def _adaln_kernel(c_ref, w_ref, b_ref, o_ref):
    s = _silu(c_ref[...])
    o_ref[...] = jnp.dot(s, w_ref[...], preferred_element_type=F32,
                         precision=lax.Precision.HIGHEST) + b_ref[...]


def _adaln(c, w_ada, b_ada):
    rows = c.shape[0]
    tn = 512
    return pl.pallas_call(
        _adaln_kernel,
        out_shape=SDS((rows, 6 * D_MODEL), F32),
        grid=(6 * D_MODEL // tn,),
        in_specs=[pl.BlockSpec((rows, D_MODEL), lambda j: (0, 0)),
                  pl.BlockSpec((D_MODEL, tn), lambda j: (0, j)),
                  pl.BlockSpec((1, tn), lambda j: (0, j))],
        out_specs=pl.BlockSpec((rows, tn), lambda j: (0, j)),
        compiler_params=_params(("arbitrary",)),
        name="adaln",
    )(c, w_ada, b_ada.reshape(1, -1))


def _inproj_kernel(x_ref, sc_ref, sh_ref, g1_ref, w_ref, gq_ref, gk_ref, bd_ref,
                   rq_ref, rk_ref, rv_ref, rg_ref, aq_ref, ak_ref, av_ref):
    x = x_ref[...]
    ms = jnp.mean(x * x, axis=-1, keepdims=True)
    h = x * lax.rsqrt(ms + EPS) * g1_ref[...]
    h = h * (1.0 + sc_ref[...]) + sh_ref[...]
    p = _dot(h.astype(BF16), w_ref[...])
    scale = HEAD_DIM ** -0.5
    rq_ref[...] = p[:, 0:RET_W].astype(BF16)
    rk_ref[...] = (p[:, RET_W:2 * RET_W] * scale).astype(BF16)
    rv_ref[...] = p[:, 2 * RET_W:3 * RET_W].astype(BF16)
    rg_ref[...] = p[:, 3 * RET_W:4 * RET_W].astype(BF16)
    o = 4 * RET_W

    def head_norm(a, g):
        msq = _dot((a * a).astype(BF16), bd_ref[...])
        return a * lax.rsqrt(msq + EPS) * g

    aq_ref[...] = head_norm(p[:, o:o + ATT_W], gq_ref[...]) * scale
    ak_ref[...] = head_norm(p[:, o + ATT_W:o + 2 * ATT_W], gk_ref[...])
    av_ref[...] = p[:, o + 2 * ATT_W:o + 3 * ATT_W]


def _inproj(x2d, sc, sh, g1, w_in_bf, gq, gk, bd, tb):
    n = x2d.shape[0]
    groups, rows, _ = sc.shape
    tiles_per_group = n // tb // groups
    mod_spec = pl.BlockSpec((None, rows, D_MODEL), lambda i: (i // tiles_per_group, 0, 0))
    tok = lambda w: pl.BlockSpec((tb, w), lambda i: (i, 0))
    full = lambda a: pl.BlockSpec(a.shape, lambda i: (0,) * a.ndim)
    outs = [SDS((n, RET_W), BF16)] * 4 + [SDS((n, ATT_W), F32)] * 3
    return pl.pallas_call(
        _inproj_kernel,
        out_shape=outs,
        grid=(n // tb,),
        in_specs=[tok(D_MODEL), mod_spec, mod_spec, full(g1), full(w_in_bf), full(gq), full(gk), full(bd)],
        out_specs=[tok(RET_W)] * 4 + [tok(ATT_W)] * 3,
        compiler_params=_params(("arbitrary",)),
        name="inproj",
    )(x2d, sc, sh, g1, w_in_bf, gq, gk, bd)


def _ret_kernel(lg_ref, q_ref, k_ref, v_ref, rg_ref, gr_ref, o_ref, st_ref, s_scr, *, nchunk):
    hp = pl.program_id(1)
    t = pl.program_id(2)
    C = RET_CHUNK
    lane = lax.broadcasted_iota(I32, (1, LANES), 1)
    m0 = lane < HEAD_DIM
    lg0 = lg_ref[2 * hp]
    lg1 = lg_ref[2 * hp + 1]
    lgv = jnp.where(m0, lg0, lg1)

    @pl.when(t == 0)
    def _():
        s_scr[...] = jnp.zeros_like(s_scr)

    ii = lax.broadcasted_iota(I32, (C, C), 0)
    jj = lax.broadcasted_iota(I32, (C, C), 1)
    diff = ii - jj
    dpos = jnp.maximum(diff, 0).astype(F32)
    dec01 = jnp.concatenate([jnp.where(diff >= 0, jnp.exp(dpos * lg0), 0.0),
                             jnp.where(diff >= 0, jnp.exp(dpos * lg1), 0.0)], axis=0)

    def head_mean(x):
        s0 = jnp.sum(jnp.where(m0, x, 0.0), axis=-1, keepdims=True)
        s1 = jnp.sum(jnp.where(m0, 0.0, x), axis=-1, keepdims=True)
        return jnp.where(m0, s0, s1) * (1.0 / HEAD_DIM)
    ri = lax.broadcasted_iota(I32, (C, LANES), 0).astype(F32)
    w_start = jnp.exp((ri + 1.0) * lgv)
    w_end = jnp.exp((float(C) - 1.0 - ri) * lgv)
    chunk_decay = jnp.exp(float(C) * lgv)
    rr = lax.broadcasted_iota(I32, (LANES, LANES), 0)
    cc = lax.broadcasted_iota(I32, (LANES, LANES), 1)
    same_head = (rr < HEAD_DIM) == (cc < HEAD_DIM)
    gr = gr_ref[...]

    def chunk(c, state):
        rows = pl.ds(pl.multiple_of(c * C, C), C)
        q = q_ref[rows, :]
        k = k_ref[rows, :]
        v = v_ref[rows, :]
        qf = q.astype(F32)
        kf = k.astype(F32)
        q01 = jnp.concatenate([jnp.where(m0, qf, 0.0), jnp.where(m0, 0.0, qf)], axis=0).astype(BF16)
        p01 = (_dot_nt(q01, k) * dec01).astype(BF16)
        pv = _dot(p01, v)
        o_intra = jnp.where(m0, pv[0:C, :], pv[C:, :])
        o_inter = _dot((qf * w_start).astype(BF16), state.astype(BF16))
        o = o_intra + o_inter
        kv = _dot_tn((kf * w_end).astype(BF16), v)
        mu = head_mean(o)
        xc = o - mu
        var = head_mean(xc * xc)
        y = xc * lax.rsqrt(var + EPS) * gr
        y = y * _silu(rg_ref[rows, :].astype(F32))
        o_ref[rows, :] = y.astype(BF16)
        return chunk_decay * state + jnp.where(same_head, kv, 0.0)

    def chunk_group(i, state):
        for u in range(RET_UNROLL):
            state = chunk(i * RET_UNROLL + u, state)
        return state

    s_scr[...] = lax.fori_loop(0, nchunk // RET_UNROLL, chunk_group, s_scr[...])

    @pl.when(t == pl.num_programs(2) - 1)
    def _():
        state = s_scr[...]
        st_ref[0] = state[0:HEAD_DIM, 0:HEAD_DIM]
        st_ref[1] = state[HEAD_DIM:, HEAD_DIM:]


def _retention_prompt(lg, rq, rk, rv, rg, g_ret, cb):
    b, s, _ = rq.shape
    nchunk = cb // RET_CHUNK
    blk = pl.BlockSpec((None, cb, LANES), lambda bi, hp, t, lg_: (bi, t, hp))
    grid_spec = pltpu.PrefetchScalarGridSpec(
        num_scalar_prefetch=1,
        grid=(b, H_RET // 2, s // cb),
        in_specs=[blk, blk, blk, blk,
                  pl.BlockSpec((1, LANES), lambda bi, hp, t, lg_: (0, hp))],
        out_specs=[blk, pl.BlockSpec((None, 2, HEAD_DIM, HEAD_DIM), lambda bi, hp, t, lg_: (bi, hp, 0, 0))],
        scratch_shapes=[pltpu.VMEM((LANES, LANES), F32)],
    )
    return pl.pallas_call(
        functools.partial(_ret_kernel, nchunk=nchunk),
        out_shape=[SDS((b, s, RET_W), BF16), SDS((b, H_RET, HEAD_DIM, HEAD_DIM), F32)],
        grid_spec=grid_spec,
        compiler_params=_params(("arbitrary", "arbitrary", "arbitrary")),
        name="retention_prompt",
    )(lg, rq, rk, rv, rg, g_ret)


ATT_TILE = MAX_WINDOW


ATT_UNROLL = 8


def _att_kernel(sl_ref, q_ref, k_ref, v_ref, o_ref, kk, vv, *stats):
    hp = pl.program_id(1)
    t = pl.program_id(2)
    TS = ATT_TILE
    nbr = len(DIL_BRANCHES)
    m_br, l_br, acc_br = stats[:nbr], stats[nbr:2 * nbr], stats[2 * nbr:]
    cur = pl.multiple_of((t & 1) * TS, TS)

    @pl.when(t == 0)
    def _():
        kk[TS:2 * TS, :] = jnp.zeros((TS, LANES), F32)
        vv[TS:2 * TS, :] = jnp.zeros((TS, LANES), F32)

    kk[pl.ds(cur, TS), :] = k_ref[...]
    vv[pl.ds(cur, TS), :] = v_ref[...]

    lane = lax.broadcasted_iota(I32, (1, LANES), 1)
    m0 = lane < HEAD_DIM
    head_lanes = (m0, jnp.logical_not(m0))
    ii = lax.broadcasted_iota(I32, (NW, 2 * NW), 0)
    jj = lax.broadcasted_iota(I32, (NW, 2 * NW), 1)
    off = ii + NW - jj
    band = (off >= 0) & (off <= NW)
    off_f = off.astype(F32)
    ones_h = [jnp.broadcast_to(jnp.where(hl, 1.0, 0.0).astype(BF16), (2 * NW, LANES)) for hl in head_lanes]

    for br, (_, dil) in enumerate(DIL_BRANCHES):
        nblk = TS // (NW * dil)
        shift = int(math.log2(nblk))
        bias = [jnp.where(band, -(sl_ref[2 * hp + hh] * float(dil)) * off_f, -jnp.inf) for hh in range(2)]

        def one_block(bs, dil=dil, nblk=nblk, shift=shift, br=br, bias=bias):
            r = lax.shift_right_logical(bs, shift)
            blk = bs & (nblk - 1)
            qs = r + dil * NW * blk
            q_rows = pl.ds(qs, NW, stride=dil)
            own = pl.ds(cur + qs, NW, stride=dil)
            prev = pl.ds((cur + qs - dil * NW) & (2 * TS - 1), NW, stride=dil)
            qb = q_ref[q_rows, :]
            kb = jnp.concatenate([kk[prev, :], kk[own, :]], axis=0).astype(BF16)
            vb = jnp.concatenate([vv[prev, :], vv[own, :]], axis=0)
            first_key = jnp.where(jnp.logical_or(t > 0, blk > 0), 0, NW)
            started = jj >= first_key
            res = None
            mx = []
            for hh in range(2):
                qh = jnp.where(head_lanes[hh], qb, 0.0).astype(BF16)
                s = jnp.where(started, _dot_nt(qh, kb) + bias[hh], -jnp.inf)
                mh = jnp.max(s, axis=-1, keepdims=True)
                p = jnp.exp(s - mh).astype(BF16)
                vh = jnp.concatenate([jnp.where(head_lanes[hh], vb, 0.0).astype(BF16), ones_h[hh]], axis=1)
                part = _dot(p, vh)
                res = part if res is None else res + part
                mx.append(mh)
            m_br[br][q_rows, :] = jnp.where(m0, mx[0], mx[1])
            acc_br[br][q_rows, :] = res[:, :LANES]
            l_br[br][q_rows, :] = res[:, LANES:]

        def body(i, carry, one_block=one_block):
            for u in range(ATT_UNROLL):
                one_block(i * ATT_UNROLL + u)
            return carry

        lax.fori_loop(0, TS // NW // ATT_UNROLL, body, 0)

    rows_per_step = 256

    def merge(i, carry):
        rows = pl.ds(pl.multiple_of(i * rows_per_step, rows_per_step), rows_per_step)
        ms = [m[rows, :] for m in m_br]
        m_all = functools.reduce(jnp.maximum, ms)
        ws = [jnp.exp(m - m_all) for m in ms]
        num = sum(w * a[rows, :] for w, a in zip(ws, acc_br))
        den = sum(w * l[rows, :] for w, l in zip(ws, l_br))
        o_ref[rows, :] = (num / den).astype(BF16)
        return carry

    lax.fori_loop(0, TS // rows_per_step, merge, 0)


def _attention_prompt(slopes, aq, ak, av):
    b, s, _ = aq.shape
    ts = ATT_TILE
    blk = pl.BlockSpec((None, ts, LANES), lambda bi, hp, t, sl: (bi, t, hp))
    grid_spec = pltpu.PrefetchScalarGridSpec(
        num_scalar_prefetch=1,
        grid=(b, H_ATT // 2, s // ts),
        in_specs=[blk, blk, blk],
        out_specs=blk,
        scratch_shapes=[pltpu.VMEM((2 * ts, LANES), F32)] * 2
        + [pltpu.VMEM((ts, LANES), F32)] * (3 * len(DIL_BRANCHES)),
    )
    return pl.pallas_call(
        _att_kernel,
        out_shape=SDS((b, s, ATT_W), BF16),
        grid_spec=grid_spec,
        compiler_params=_params(("arbitrary", "arbitrary", "arbitrary")),
        name="attention_prompt",
    )(slopes, aq, ak, av)


def _sret_kernel(lg_ref, q_ref, k_ref, v_ref, rg_ref, gr_ref, st_ref, o_ref, sn_ref):
    q = q_ref[...].astype(F32)
    k = k_ref[...].astype(F32)
    v = v_ref[...].astype(F32)
    rg = rg_ref[...].astype(F32)
    gr = gr_ref[...]
    r_i = lax.broadcasted_iota(I32, (HEAD_DIM, HEAD_DIM), 0)
    c_i = lax.broadcasted_iota(I32, (HEAD_DIM, HEAD_DIM), 1)
    eye = r_i == c_i
    outs = []
    for h in range(H_RET):
        sl = slice(h * HEAD_DIM, (h + 1) * HEAD_DIM)
        qh, kh, vh = q[:, sl], k[:, sl], v[:, sl]
        gamma = jnp.exp(jnp.full((1, HEAD_DIM), lg_ref[h], F32))
        q_col = jnp.sum(jnp.where(eye, qh, 0.0), axis=1, keepdims=True)
        k_col = jnp.sum(jnp.where(eye, kh, 0.0), axis=1, keepdims=True)
        state = st_ref[h]
        o_inter = gamma * jnp.sum(q_col * state, axis=0, keepdims=True)
        o_intra = jnp.sum(qh * kh, axis=-1, keepdims=True) * vh
        sn_ref[h] = gamma * state + k_col * vh
        o = o_intra + o_inter
        mu = jnp.mean(o, axis=-1, keepdims=True)
        xc = o - mu
        var = jnp.mean(xc * xc, axis=-1, keepdims=True)
        outs.append(xc * lax.rsqrt(var + EPS) * gr[:, sl] * _silu(rg[:, sl]))
    o_ref[...] = jnp.concatenate(outs, axis=-1).astype(BF16)


def _retention_sample(lg, rq, rk, rv, rg, g_ret, state):
    db = rq.shape[0]
    row = pl.BlockSpec((None, 1, RET_W), lambda i: (i, 0, 0))
    st = pl.BlockSpec((None, H_RET, HEAD_DIM, HEAD_DIM), lambda i: (i, 0, 0, 0))
    r3 = lambda a: a.reshape(db, 1, RET_W)
    return pl.pallas_call(
        _sret_kernel,
        out_shape=[SDS((db, 1, RET_W), BF16), SDS(state.shape, F32)],
        grid=(db,),
        in_specs=[pl.BlockSpec(memory_space=pltpu.SMEM), row, row, row, row,
                  pl.BlockSpec((1, RET_W), lambda i: (0, 0)), st],
        out_specs=[row, st],
        compiler_params=_params(("arbitrary",)),
        name="retention_sample",
    )(lg, r3(rq), r3(rk), r3(rv), r3(rg), g_ret, state)


def _satt_kernel(sl_ref, qt_ref, knt_ref, vnt_ref, kt_ref, vt_ref, o_ref):
    win = kt_ref.shape[-1]
    dist = win - lax.broadcasted_iota(I32, (1, win), 1)
    count = jnp.zeros((1, win), F32)
    for window, dil in DIL_BRANCHES:
        reads = ((dist & (dil - 1)) == 0) & (dist <= window)
        count = count + jnp.where(reads, 1.0, 0.0)
    dist_f = dist.astype(F32)
    nbr = float(len(DIL_BRANCHES))
    for h in range(H_ATT):
        q = qt_ref[:, h:h + 1]
        s = jnp.sum(kt_ref[h] * q, axis=0, keepdims=True) - sl_ref[h] * dist_f
        s = jnp.where(count > 0.0, s, -jnp.inf)
        s_new = jnp.sum(q * knt_ref[:, h:h + 1], axis=0, keepdims=True)
        m = jnp.maximum(jnp.max(s, axis=-1, keepdims=True), s_new)
        p = count * jnp.exp(s - m)
        p_new = nbr * jnp.exp(s_new - m)
        den = jnp.sum(p, axis=-1, keepdims=True) + p_new
        num = jnp.sum(vt_ref[h] * p, axis=-1, keepdims=True) + p_new * vnt_ref[:, h:h + 1]
        o_ref[:, h:h + 1] = num / den


def _attention_sample(slopes, aq, ak, av, win_k, win_v):
    db = aq.shape[0]
    win = win_k.shape[1]
    col = pl.BlockSpec((None, HEAD_DIM, H_ATT), lambda i: (i, 0, 0))
    cache = pl.BlockSpec((None, H_ATT, HEAD_DIM, win), lambda i: (i, 0, 0, 0))
    cols = lambda a: jnp.transpose(a.reshape(db, H_ATT, HEAD_DIM), (0, 2, 1))
    keys_minor = lambda c: jnp.transpose(c, (0, 2, 3, 1))
    out = pl.pallas_call(
        _satt_kernel,
        out_shape=SDS((db, HEAD_DIM, H_ATT), F32),
        grid=(db,),
        in_specs=[pl.BlockSpec(memory_space=pltpu.SMEM), col, col, col, cache, cache],
        out_specs=col,
        compiler_params=_params(("arbitrary",)),
        name="attention_sample",
    )(slopes, cols(aq), cols(ak), cols(av), keys_minor(win_k), keys_minor(win_v))
    return jnp.transpose(out, (0, 2, 1)).reshape(db, ATT_W).astype(BF16)


def _outproj_kernel(x_ref, ret_ref, att_ref, wo_ref, gt1_ref, g2_ref, sc2_ref, sh2_ref, wqh_ref, wql_ref,
                    x1_ref, h2_ref, qp_ref):
    mix = _dot(ret_ref[...], wo_ref[0:RET_W, :]) + _dot(att_ref[...], wo_ref[RET_W:, :])
    x1 = x_ref[...] + gt1_ref[...] * mix
    x1_ref[...] = x1
    ms = jnp.mean(x1 * x1, axis=-1, keepdims=True)
    h2 = x1 * lax.rsqrt(ms + EPS) * g2_ref[...]
    h2 = h2 * (1.0 + sc2_ref[...]) + sh2_ref[...]
    h2_ref[...] = h2
    hi, lo = _split(h2)
    wqh = wqh_ref[...]
    qp_ref[...] = _dot(hi, wqh) + (_dot(lo, wqh) + _dot(hi, wql_ref[...]))


def _outproj(x2d, ret, att, wo_bf, gt1, g2, sc2, sh2, wq_hi, wq_lo, tb):
    n = x2d.shape[0]
    groups, rows, _ = gt1.shape
    tiles_per_group = n // tb // groups
    mod_spec = pl.BlockSpec((None, rows, D_MODEL), lambda i: (i // tiles_per_group, 0, 0))
    tok = lambda w: pl.BlockSpec((tb, w), lambda i: (i, 0))
    full = lambda a: pl.BlockSpec(a.shape, lambda i: (0,) * a.ndim)
    return pl.pallas_call(
        _outproj_kernel,
        out_shape=[SDS((n, D_MODEL), F32)] * 3,
        grid=(n // tb,),
        in_specs=[tok(D_MODEL), tok(RET_W), tok(ATT_W), full(wo_bf), mod_spec, full(g2), mod_spec, mod_spec,
                  full(wq_hi), full(wq_lo)],
        out_specs=[tok(D_MODEL)] * 3,
        compiler_params=_params(("arbitrary",)),
        name="outproj",
    )(x2d, ret, att, wo_bf, gt1, g2, sc2, sh2, wq_hi, wq_lo)


def _top16(sc, ids=None):
    if ids is None:
        ids = lax.broadcasted_iota(I32, sc.shape, 0)
    ids = ids.astype(F32)
    vals, idxs = [], []
    for _ in range(PEER_TOPK):
        m = jnp.max(sc, axis=0, keepdims=True)
        idx = jnp.min(jnp.where(sc == m, ids, jnp.inf), axis=0, keepdims=True)
        vals.append(m)
        idxs.append(idx)
        sc = jnp.where(ids == idx, -jnp.inf, sc)
    return jnp.concatenate(vals, axis=0), jnp.concatenate(idxs, axis=0).astype(I32)


def _pair_candidates(s1, s2):
    k = PEER_TOPK
    half = k // 2
    pieces = [s1[0:1, :] + s2]
    pieces += [s1[a:a + 1, :] + s2[0:half, :] for a in range(1, half)]
    pieces.append(s1[half:, :] + s2[0:1, :])
    cand = jnp.concatenate(pieces, axis=0)
    r = lax.broadcasted_iota(I32, cand.shape, 0)
    mid = r - k
    a_mid = 1 + lax.shift_right_logical(mid, 3)
    b_mid = mid & (half - 1)
    tail_start = k + half * (half - 1)
    pos = jnp.where(r < k, r, jnp.where(r < tail_start, a_mid * k + b_mid, (half + r - tail_start) * k))
    reachable = jnp.logical_or(jnp.logical_or(r < k, r >= tail_start), (a_mid + 1) * (b_mid + 1) <= k)
    return jnp.where(reachable, cand, -jnp.inf), pos


TOPK_HEADS = 8


def _topk_one_head(q, skh_ref, skl_ref, hh):
    qh, ql = _split(q)

    def scores(c):
        kh = skh_ref[hh, c]
        return _dot_nt(kh, qh) + (_dot_nt(kh, ql) + _dot_nt(skl_ref[hh, c], qh))

    s1, i1 = _top16(scores(0))
    s2, i2 = _top16(scores(1))
    top_s, pos = _top16(*_pair_candidates(s1, s2))
    a_sel = lax.shift_right_logical(pos, 4)
    b_sel = pos & (PEER_TOPK - 1)
    io = lax.broadcasted_iota(I32, i1.shape, 0)
    rows = []
    for r in range(PEER_TOPK):
        ia = jnp.sum(jnp.where(io == a_sel[r:r + 1, :], i1, 0), axis=0, keepdims=True)
        ib = jnp.sum(jnp.where(io == b_sel[r:r + 1, :], i2, 0), axis=0, keepdims=True)
        rows.append(ia * N_KEYS + ib)
    ex = jnp.exp(top_s - top_s[0:1, :])
    return jnp.concatenate(rows, axis=0) * ROW_SUB, ex / jnp.sum(ex, axis=0, keepdims=True)


def _topk_kernel(q_ref, skh_ref, skl_ref, e_ref, g_ref, e_scr, g_scr):
    step = pl.program_id(1)
    for hh in range(TOPK_HEADS):
        e, g = _topk_one_head(q_ref[:, LANES * hh:LANES * (hh + 1)], skh_ref, skl_ref, hh)
        slot_rows = pl.ds(pl.multiple_of((step * TOPK_HEADS + hh) * PEER_TOPK, PEER_TOPK), PEER_TOPK)
        e_scr[slot_rows, :] = e
        g_scr[slot_rows, :] = g

    @pl.when(step == pl.num_programs(1) - 1)
    def _():
        e_ref[...] = e_scr[...].T
        g_ref[...] = g_scr[...].T


def _peer_topk(qp, sk_hi, sk_lo, tt):
    n = qp.shape[0]
    out_blk = pl.BlockSpec((tt, PEER_SLOTS), lambda i, h: (i, 0))
    sk_blk = pl.BlockSpec((TOPK_HEADS, 2, N_KEYS, LANES), lambda i, h: (h, 0, 0, 0))
    return pl.pallas_call(
        _topk_kernel,
        out_shape=[SDS((n, PEER_SLOTS), I32), SDS((n, PEER_SLOTS), F32)],
        grid=(n // tt, PEER_HEADS // TOPK_HEADS),
        in_specs=[pl.BlockSpec((tt, TOPK_HEADS * LANES), lambda i, h: (i, h)), sk_blk, sk_blk],
        out_specs=[out_blk, out_blk],
        scratch_shapes=[pltpu.VMEM((PEER_SLOTS, tt), I32), pltpu.VMEM((PEER_SLOTS, tt), F32)],
        compiler_params=_params(("arbitrary", "arbitrary")),
        name="peer_topk",
    )(qp, sk_hi, sk_lo)


def _gather_rows(idx_ref, tab_ref, t):
    rows = []
    for k in range(PEER_SLOTS):
        e = pl.multiple_of(idx_ref[t, k], ROW_SUB)
        rows.append(tab_ref[pl.ds(e, ROW_SUB), :])
    return pltpu.bitcast(jnp.concatenate(rows, axis=0), BF16)


def _diag_mask():
    sub = lax.broadcasted_iota(I32, (SUBLANES, D_MODEL), 0)
    col = lax.broadcasted_iota(I32, (SUBLANES, D_MODEL), 1)
    return (col & (SUBLANES - 1)) == sub


def _for_each_token(tb, one_token):
    def group(i, carry):
        for u in range(TOK_UNROLL):
            one_token(i * TOK_UNROLL + u)
        return carry

    lax.fori_loop(0, tb // TOK_UNROLL, group, 0)


def _peer_u_kernel(idx_ref, tab_ref, h2_ref, g_ref, sel_ref, w_ref, xbuf, rall):
    tb = h2_ref.shape[0]
    for j in range(SUBLANES):
        xbuf[pl.ds(j, tb, stride=SUBLANES), :] = h2_ref[:, LANES * j:LANES * (j + 1)]
    diag = _diag_mask()

    def tok(t):
        rows = _gather_rows(idx_ref, tab_ref, t)
        xt = xbuf[pl.ds(pl.multiple_of(t * SUBLANES, SUBLANES), SUBLANES), :].astype(BF16)
        r = _dot_nt(xt, rows)
        rall[pl.ds(t, 1), :] = jnp.sum(jnp.where(diag, r, 0.0), axis=0, keepdims=True)

    _for_each_token(tb, tok)
    hi, lo = _split(rall[...])
    sel = sel_ref[...]
    a = _dot(hi, sel) + _dot(lo, sel)
    gelu = 0.5 * a * (1.0 + lax.erf(a * (2.0 ** -0.5)))
    w_ref[...] = g_ref[...] * gelu


def _peer_v_kernel(idx_ref, tab_ref, w_ref, x1_ref, gt2_ref, selt_ref, y_ref, wrep, ybuf):
    tb = x1_ref.shape[0]
    wrep[...] = _dot(w_ref[...].astype(BF16), selt_ref[...])
    diag = _diag_mask()

    def tok(t):
        rows = _gather_rows(idx_ref, tab_ref, t)
        wt = jnp.where(diag, wrep[pl.ds(t, 1), :], 0.0).astype(BF16)
        ybuf[pl.ds(pl.multiple_of(t * SUBLANES, SUBLANES), SUBLANES), :] = _dot(wt, rows)

    _for_each_token(tb, tok)
    for j in range(SUBLANES):
        cols = slice(LANES * j, LANES * (j + 1))
        y_ref[:, cols] = x1_ref[:, cols] + gt2_ref[:, cols] * ybuf[pl.ds(j, tb, stride=SUBLANES), :]


def _peer_u(idx, tab, h2, g, sel, tb):
    n = h2.shape[0]
    tok = lambda w: pl.BlockSpec((tb, w), lambda i: (i, 0))
    return pl.pallas_call(
        _peer_u_kernel,
        out_shape=SDS((n, PEER_SLOTS), F32),
        grid=(n // tb,),
        in_specs=[pl.BlockSpec((tb, PEER_SLOTS), lambda i: (i, 0), memory_space=pltpu.SMEM),
                  pl.BlockSpec(memory_space=pltpu.VMEM), tok(D_MODEL), tok(PEER_SLOTS),
                  pl.BlockSpec(sel.shape, lambda i: (0, 0))],
        out_specs=tok(PEER_SLOTS),
        scratch_shapes=[pltpu.VMEM((tb * SUBLANES, LANES), F32), pltpu.VMEM((tb, D_MODEL), F32)],
        compiler_params=_params(("arbitrary",)),
        name="peer_u",
    )(idx, tab, h2, g, sel)


def _peer_v(idx, tab, w, x1, gt2, selt, tb):
    n = x1.shape[0]
    groups, rows, _ = gt2.shape
    tiles_per_group = n // tb // groups
    tok = lambda wd: pl.BlockSpec((tb, wd), lambda i: (i, 0))
    return pl.pallas_call(
        _peer_v_kernel,
        out_shape=SDS((n, D_MODEL), F32),
        grid=(n // tb,),
        in_specs=[pl.BlockSpec((tb, PEER_SLOTS), lambda i: (i, 0), memory_space=pltpu.SMEM),
                  pl.BlockSpec(memory_space=pltpu.VMEM), tok(PEER_SLOTS), tok(D_MODEL),
                  pl.BlockSpec((None, rows, D_MODEL), lambda i: (i // tiles_per_group, 0, 0)),
                  pl.BlockSpec(selt.shape, lambda i: (0, 0))],
        out_specs=tok(D_MODEL),
        scratch_shapes=[pltpu.VMEM((tb, D_MODEL), F32), pltpu.VMEM((tb * SUBLANES, LANES), F32)],
        compiler_params=_params(("arbitrary",)),
        name="peer_v",
    )(idx, tab, w, x1, gt2, selt)


def _pack_table(tab):
    e = tab.shape[0]
    bits = lax.bitcast_convert_type(tab.astype(BF16), jnp.uint16).astype(jnp.uint32)
    bits = bits.reshape(e, ROW_SUB, 2, LANES)
    word = bits[:, :, 0, :] | (bits[:, :, 1, :] << 16)
    return lax.bitcast_convert_type(word, I32).reshape(e * ROW_SUB, LANES)


def _block_avg(width):
    i = np.arange(width) // HEAD_DIM
    return jnp.asarray((i[:, None] == i[None, :]).astype(np.float32) / HEAD_DIM, BF16)


def _slot_select():
    col = np.arange(D_MODEL) // SUBLANES
    return jnp.asarray((col[:, None] == np.arange(PEER_SLOTS)[None, :]).astype(np.float32), BF16)


def _ret_log_decay():
    return jnp.log1p(-jnp.exp2(-5.0 - jnp.arange(H_RET, dtype=F32)))


def _alibi_slopes():
    return jnp.exp2(-8.0 * (jnp.arange(H_ATT, dtype=F32) + 1.0) / H_ATT)


def _pad_sub_keys(sub_keys):
    z = jnp.zeros_like(sub_keys[:, 0])
    return jnp.stack([jnp.concatenate([sub_keys[:, 0], z], axis=-1),
                      jnp.concatenate([z, sub_keys[:, 1]], axis=-1)], axis=1)


def _group_w_out(w):
    w4 = w.reshape(H_RET, 2, HEAD_DIM, D_MODEL)
    return jnp.concatenate([w4[:, 0].reshape(RET_W, D_MODEL), w4[:, 1].reshape(ATT_W, D_MODEL)], axis=0)


def _split_w(w):
    hi = w.astype(BF16)
    return hi, (w - hi.astype(F32)).astype(BF16)


def _token_tile(n, pref):
    return pref if n % pref == 0 else n


def _peer(h2, qp, x1, gt2, consts, weights):
    n = h2.shape[0]
    idx, gate = _peer_topk(qp, weights["sk_hi"], weights["sk_lo"], _token_tile(n, 128))
    tb = _token_tile(n, 256)
    w = _peer_u(idx, weights["u_tab"], h2, gate, consts["sel"], tb)
    return _peer_v(idx, weights["v_tab"], w, x1, gt2, consts["selt"], tb)


def _mix_and_peer(x2d, ret, att, mods, consts, weights):
    sh1, sc1, gt1, sh2, sc2, gt2 = mods
    n = x2d.shape[0]
    x1, h2, qp = _outproj(x2d, ret, att, weights["w_out"], gt1, weights["g_norm2"], sc2, sh2,
                          weights["wq_hi"], weights["wq_lo"], _token_tile(n, 512))
    return _peer(h2, qp, x1, gt2, consts, weights)


def _project(x2d, mods, consts, weights):
    sh1, sc1 = mods[0], mods[1]
    n = x2d.shape[0]
    return _inproj(x2d, sc1, sh1, weights["g_norm1"], weights["w_in"], weights["g_qn"], weights["g_kn"],
                   consts["avg_att"], _token_tile(n, 512))


def kernel(x_prompt, x_sample, c_prompt, c_sample, state_ret, cache_win_k, cache_win_v, w_ada, b_ada, g_norm1, w_in, g_qn, g_kn, g_ret, w_out, g_norm2, w_pq, sub_keys, peer_u, peer_v):
    b, s, d = x_prompt.shape
    db, ds_, _ = x_sample.shape
    depth = w_ada.shape[0]
    assert d == D_MODEL and ds_ == 1 and s % ATT_TILE == 0
    win = cache_win_k.shape[2]
    assert win == MAX_WINDOW

    consts = {
        "avg_att": _block_avg(ATT_W),
        "sel": _slot_select(),
        "selt": _slot_select().T,
        "lg": _ret_log_decay(),
        "slopes": _alibi_slopes(),
    }

    y_p = x_prompt.reshape(b * s, d)
    y_s = x_sample.reshape(db, d)
    outs = {k: [] for k in ("ret_p", "wk_p", "wv_p", "ret_s", "wk_s", "wv_s")}
    for l in range(depth):
        wq_hi, wq_lo = _split_w(w_pq[l])
        sk_hi, sk_lo = _split_w(_pad_sub_keys(sub_keys[l]))
        weights = {
            "g_norm1": g_norm1[l].reshape(1, d), "g_norm2": g_norm2[l].reshape(1, d),
            "w_in": w_in[l].astype(BF16), "w_out": _group_w_out(w_out[l]).astype(BF16),
            "g_qn": g_qn[l].reshape(1, ATT_W), "g_kn": g_kn[l].reshape(1, ATT_W),
            "wq_hi": wq_hi, "wq_lo": wq_lo, "sk_hi": sk_hi, "sk_lo": sk_lo,
            "u_tab": _pack_table(peer_u[l]), "v_tab": _pack_table(peer_v[l]),
        }
        gr = g_ret[l].reshape(1, RET_W)

        mod = _adaln(jnp.concatenate([c_prompt, c_sample], axis=0), w_ada[l], b_ada[l])
        mod6 = jnp.split(mod, 6, axis=-1)
        mods_p = [m[:b].reshape(b, 1, d) for m in mod6]
        mods_s = [m[b:].reshape(1, db, d) for m in mod6]

        rq, rk, rv, rg, aq, ak, av = _project(y_p, mods_p, consts, weights)
        seq = lambda a: a.reshape(b, s, a.shape[-1])
        ret, ret_state = _retention_prompt(consts["lg"], seq(rq), seq(rk), seq(rv), seq(rg), gr, 1024)
        att = _attention_prompt(consts["slopes"], seq(aq), seq(ak), seq(av))
        y_p = _mix_and_peer(y_p, ret.reshape(b * s, RET_W), att.reshape(b * s, ATT_W), mods_p, consts, weights)
        keep = min(MAX_WINDOW, s)
        outs["ret_p"].append(ret_state)
        outs["wk_p"].append(seq(ak)[:, s - keep:].reshape(b, keep, H_ATT, HEAD_DIM))
        outs["wv_p"].append(seq(av)[:, s - keep:].reshape(b, keep, H_ATT, HEAD_DIM))

        rq, rk, rv, rg, aq, ak, av = _project(y_s, mods_s, consts, weights)
        ret, new_state = _retention_sample(consts["lg"], rq, rk, rv, rg, gr, state_ret[l])
        att = _attention_sample(consts["slopes"], aq, ak, av, cache_win_k[l], cache_win_v[l])
        y_s = _mix_and_peer(y_s, ret.reshape(db, RET_W), att.reshape(db, ATT_W), mods_s, consts, weights)
        outs["ret_s"].append(new_state)
        outs["wk_s"].append(ak.reshape(db, 1, H_ATT, HEAD_DIM))
        outs["wv_s"].append(av.reshape(db, 1, H_ATT, HEAD_DIM))

    st = lambda k: jnp.stack(outs[k])
    return (y_p.reshape(b, s, d), y_s.reshape(db, 1, d), st("ret_p"), st("wk_p"), st("wv_p"),
            st("ret_s"), st("wk_s"), st("wv_s"))
```

```python
import functools
import math

import numpy as np
import jax
import jax.numpy as jnp
from jax import lax
from jax.experimental import pallas as pl
from jax.experimental.pallas import tpu as pltpu

F32 = jnp.float32
BF16 = jnp.bfloat16
I32 = jnp.int32

D_MODEL = 1024
HEAD_DIM = 64
RET_W = 512
ATT_W = 512
H_RET = 8
H_ATT = 8
N_PROJ = 4 * RET_W + 3 * ATT_W
RET_CHUNK = 128
DIL_BRANCHES = ((128, 1), (512, 4), (2048, 16))
NW = 128
MAX_WINDOW = 2048
N_KEYS = 128
PEER_HEADS = 8
PEER_TOPK = 16
PEER_SLOTS = PEER_HEADS * PEER_TOPK
EPS = 1e-6

LANES = 128
SUBLANES = 8
ROW_WORDS = D_MODEL // 2
ROW_SUB = ROW_WORDS // LANES
VMEM_LIMIT = 56 * 1024 * 1024
RET_UNROLL = 4
TOK_UNROLL = 32

SDS = jax.ShapeDtypeStruct


def _dot(a, b):
    return jnp.dot(a, b, preferred_element_type=F32)


def _dot_nt(a, b):
    return lax.dot_general(a, b, (((1,), (1,)), ((), ())), preferred_element_type=F32)


def _dot_tn(a, b):
    return lax.dot_general(a, b, (((0,), (0,)), ((), ())), preferred_element_type=F32)


def _split(x):
    hi = x.astype(BF16)
    lo = (x - hi.astype(F32)).astype(BF16)
    return hi, lo


def _silu(x):
    return x * jax.nn.sigmoid(x)


def _params(sem, vmem=VMEM_LIMIT):
    return pltpu.CompilerParams(dimension_semantics=sem, vmem_limit_bytes=vmem)


def _adaln_kernel(c_ref, w_ref, b_ref, o_ref):
    s = _silu(c_ref[...])
    o_ref[...] = jnp.dot(s, w_ref[...], preferred_element_type=F32,
                         precision=lax.Precision.HIGHEST) + b_ref[...]


def _adaln(c, w_ada, b_ada):
    rows = c.shape[0]
    tn = 512
    return pl.pallas_call(
        _adaln_kernel,
        out_shape=SDS((rows, 6 * D_MODEL), F32),
        grid=(6 * D_MODEL // tn,),
        in_specs=[pl.BlockSpec((rows, D_MODEL), lambda j: (0, 0)),
                  pl.BlockSpec((D_MODEL, tn), lambda j: (0, j)),
                  pl.BlockSpec((1, tn), lambda j: (0, j))],
        out_specs=pl.BlockSpec((rows, tn), lambda j: (0, j)),
        compiler_params=_params(("arbitrary",)),
        name="adaln",
    )(c, w_ada, b_ada.reshape(1, -1))


def _inproj_kernel(x_ref, sc_ref, sh_ref, g1_ref, w_ref, gq_ref, gk_ref, bd_ref,
                   rq_ref, rk_ref, rv_ref, rg_ref, aq_ref, ak_ref, av_ref):
    x = x_ref[...]
    ms = jnp.mean(x * x, axis=-1, keepdims=True)
    h = x * lax.rsqrt(ms + EPS) * g1_ref[...]
    h = h * (1.0 + sc_ref[...]) + sh_ref[...]
    p = _dot(h.astype(BF16), w_ref[...])
    scale = HEAD_DIM ** -0.5
    rq_ref[...] = p[:, 0:RET_W].astype(BF16)
    rk_ref[...] = (p[:, RET_W:2 * RET_W] * scale).astype(BF16)
    rv_ref[...] = p[:, 2 * RET_W:3 * RET_W].astype(BF16)
    rg_ref[...] = p[:, 3 * RET_W:4 * RET_W].astype(BF16)
    o = 4 * RET_W

    def head_norm(a, g):
        msq = _dot((a * a).astype(BF16), bd_ref[...])
        return a * lax.rsqrt(msq + EPS) * g

    aq_ref[...] = head_norm(p[:, o:o + ATT_W], gq_ref[...]) * scale
    ak_ref[...] = head_norm(p[:, o + ATT_W:o + 2 * ATT_W], gk_ref[...])
    av_ref[...] = p[:, o + 2 * ATT_W:o + 3 * ATT_W]


def _inproj(x2d, sc, sh, g1, w_in_bf, gq, gk, bd, tb):
    n = x2d.shape[0]
    groups, rows, _ = sc.shape
    tiles_per_group = n // tb // groups
    mod_spec = pl.BlockSpec((None, rows, D_MODEL), lambda i: (i // tiles_per_group, 0, 0))
    tok = lambda w: pl.BlockSpec((tb, w), lambda i: (i, 0))
    full = lambda a: pl.BlockSpec(a.shape, lambda i: (0,) * a.ndim)
    outs = [SDS((n, RET_W), BF16)] * 4 + [SDS((n, ATT_W), F32)] * 3
    return pl.pallas_call(
        _inproj_kernel,
        out_shape=outs,
        grid=(n // tb,),
        in_specs=[tok(D_MODEL), mod_spec, mod_spec, full(g1), full(w_in_bf), full(gq), full(gk), full(bd)],
        out_specs=[tok(RET_W)] * 4 + [tok(ATT_W)] * 3,
        compiler_params=_params(("arbitrary",)),
        name="inproj",
    )(x2d, sc, sh, g1, w_in_bf, gq, gk, bd)


def _ret_kernel(lg_ref, q_ref, k_ref, v_ref, rg_ref, gr_ref, o_ref, st_ref, s_scr, *, nchunk):
    hp = pl.program_id(1)
    t = pl.program_id(2)
    C = RET_CHUNK
    lane = lax.broadcasted_iota(I32, (1, LANES), 1)
    m0 = lane < HEAD_DIM
    lg0 = lg_ref[2 * hp]
    lg1 = lg_ref[2 * hp + 1]
    lgv = jnp.where(m0, lg0, lg1)

    @pl.when(t == 0)
    def _():
        s_scr[...] = jnp.zeros_like(s_scr)

    ii = lax.broadcasted_iota(I32, (C, C), 0)
    jj = lax.broadcasted_iota(I32, (C, C), 1)
    diff = ii - jj
    dpos = jnp.maximum(diff, 0).astype(F32)
    dec01 = jnp.concatenate([jnp.where(diff >= 0, jnp.exp(dpos * lg0), 0.0),
                             jnp.where(diff >= 0, jnp.exp(dpos * lg1), 0.0)], axis=0)

    def head_mean(x):
        s0 = jnp.sum(jnp.where(m0, x, 0.0), axis=-1, keepdims=True)
        s1 = jnp.sum(jnp.where(m0, 0.0, x), axis=-1, keepdims=True)
        return jnp.where(m0, s0, s1) * (1.0 / HEAD_DIM)
    ri = lax.broadcasted_iota(I32, (C, LANES), 0).astype(F32)
    w_start = jnp.exp((ri + 1.0) * lgv)
    w_end = jnp.exp((float(C) - 1.0 - ri) * lgv)
    chunk_decay = jnp.exp(float(C) * lgv)
    rr = lax.broadcasted_iota(I32, (LANES, LANES), 0)
    cc = lax.broadcasted_iota(I32, (LANES, LANES), 1)
    same_head = (rr < HEAD_DIM) == (cc < HEAD_DIM)
    gr = gr_ref[...]

    def chunk(c, state):
        rows = pl.ds(pl.multiple_of(c * C, C), C)
        q = q_ref[rows, :]
        k = k_ref[rows, :]
        v = v_ref[rows, :]
        qf = q.astype(F32)
        kf = k.astype(F32)
        q01 = jnp.concatenate([jnp.where(m0, qf, 0.0), jnp.where(m0, 0.0, qf)], axis=0).astype(BF16)
        p01 = (_dot_nt(q01, k) * dec01).astype(BF16)
        pv = _dot(p01, v)
        o_intra = jnp.where(m0, pv[0:C, :], pv[C:, :])
        o_inter = _dot((qf * w_start).astype(BF16), state.astype(BF16))
        o = o_intra + o_inter
        kv = _dot_tn((kf * w_end).astype(BF16), v)
        mu = head_mean(o)
        xc = o - mu
        var = head_mean(xc * xc)
        y = xc * lax.rsqrt(var + EPS) * gr
        y = y * _silu(rg_ref[rows, :].astype(F32))
        o_ref[rows, :] = y.astype(BF16)
        return chunk_decay * state + jnp.where(same_head, kv, 0.0)

    def chunk_group(i, state):
        for u in range(RET_UNROLL):
            state = chunk(i * RET_UNROLL + u, state)
        return state

    s_scr[...] = lax.fori_loop(0, nchunk // RET_UNROLL, chunk_group, s_scr[...])

    @pl.when(t == pl.num_programs(2) - 1)
    def _():
        state = s_scr[...]
        st_ref[0] = state[0:HEAD_DIM, 0:HEAD_DIM]
        st_ref[1] = state[HEAD_DIM:, HEAD_DIM:]


def _retention_prompt(lg, rq, rk, rv, rg, g_ret, cb):
    b, s, _ = rq.shape
    nchunk = cb // RET_CHUNK
    blk = pl.BlockSpec((None, cb, LANES), lambda bi, hp, t, lg_: (bi, t, hp))
    grid_spec = pltpu.PrefetchScalarGridSpec(
        num_scalar_prefetch=1,
        grid=(b, H_RET // 2, s // cb),
        in_specs=[blk, blk, blk, blk,
                  pl.BlockSpec((1, LANES), lambda bi, hp, t, lg_: (0, hp))],
        out_specs=[blk, pl.BlockSpec((None, 2, HEAD_DIM, HEAD_DIM), lambda bi, hp, t, lg_: (bi, hp, 0, 0))],
        scratch_shapes=[pltpu.VMEM((LANES, LANES), F32)],
    )
    return pl.pallas_call(
        functools.partial(_ret_kernel, nchunk=nchunk),
        out_shape=[SDS((b, s, RET_W), BF16), SDS((b, H_RET, HEAD_DIM, HEAD_DIM), F32)],
        grid_spec=grid_spec,
        compiler_params=_params(("arbitrary", "arbitrary", "arbitrary")),
        name="retention_prompt",
    )(lg, rq, rk, rv, rg, g_ret)


ATT_TILE = MAX_WINDOW


ATT_UNROLL = 8


def _att_kernel(sl_ref, q_ref, k_ref, v_ref, o_ref, kk, vv, *stats):
    hp = pl.program_id(1)
    t = pl.program_id(2)
    TS = ATT_TILE
    nbr = len(DIL_BRANCHES)
    m_br, l_br, acc_br = stats[:nbr], stats[nbr:2 * nbr], stats[2 * nbr:]
    cur = pl.multiple_of((t & 1) * TS, TS)

    @pl.when(t == 0)
    def _():
        kk[TS:2 * TS, :] = jnp.zeros((TS, LANES), F32)
        vv[TS:2 * TS, :] = jnp.zeros((TS, LANES), F32)

    kk[pl.ds(cur, TS), :] = k_ref[...]
    vv[pl.ds(cur, TS), :] = v_ref[...]

    lane = lax.broadcasted_iota(I32, (1, LANES), 1)
    m0 = lane < HEAD_DIM
    head_lanes = (m0, jnp.logical_not(m0))
    ii = lax.broadcasted_iota(I32, (NW, 2 * NW), 0)
    jj = lax.broadcasted_iota(I32, (NW, 2 * NW), 1)
    off = ii + NW - jj
    band = (off >= 0) & (off <= NW)
    off_f = off.astype(F32)
    ones_h = [jnp.broadcast_to(jnp.where(hl, 1.0, 0.0).astype(BF16), (2 * NW, LANES)) for hl in head_lanes]

    for br, (_, dil) in enumerate(DIL_BRANCHES):
        nblk = TS // (NW * dil)
        shift = int(math.log2(nblk))
        bias = [jnp.where(band, -(sl_ref[2 * hp + hh] * float(dil)) * off_f, -jnp.inf) for hh in range(2)]

        def one_block(bs, dil=dil, nblk=nblk, shift=shift, br=br, bias=bias):
            r = lax.shift_right_logical(bs, shift)
            blk = bs & (nblk - 1)
            qs = r + dil * NW * blk
            q_rows = pl.ds(qs, NW, stride=dil)
            own = pl.ds(cur + qs, NW, stride=dil)
            prev = pl.ds((cur + qs - dil * NW) & (2 * TS - 1), NW, stride=dil)
            qb = q_ref[q_rows, :]
            kb = jnp.concatenate([kk[prev, :], kk[own, :]], axis=0).astype(BF16)
            vb = jnp.concatenate([vv[prev, :], vv[own, :]], axis=0)
            first_key = jnp.where(jnp.logical_or(t > 0, blk > 0), 0, NW)
            started = jj >= first_key
            res = None
            mx = []
            for hh in range(2):
                qh = jnp.where(head_lanes[hh], qb, 0.0).astype(BF16)
                s = jnp.where(started, _dot_nt(qh, kb) + bias[hh], -jnp.inf)
                mh = jnp.max(s, axis=-1, keepdims=True)
                p = jnp.exp(s - mh).astype(BF16)
                vh = jnp.concatenate([jnp.where(head_lanes[hh], vb, 0.0).astype(BF16), ones_h[hh]], axis=1)
                part = _dot(p, vh)
                res = part if res is None else res + part
                mx.append(mh)
            m_br[br][q_rows, :] = jnp.where(m0, mx[0], mx[1])
            acc_br[br][q_rows, :] = res[:, :LANES]
            l_br[br][q_rows, :] = res[:, LANES:]

        def body(i, carry, one_block=one_block):
            for u in range(ATT_UNROLL):
                one_block(i * ATT_UNROLL + u)
            return carry

        lax.fori_loop(0, TS // NW // ATT_UNROLL, body, 0)

    rows_per_step = 256

    def merge(i, carry):
        rows = pl.ds(pl.multiple_of(i * rows_per_step, rows_per_step), rows_per_step)
        ms = [m[rows, :] for m in m_br]
        m_all = functools.reduce(jnp.maximum, ms)
        ws = [jnp.exp(m - m_all) for m in ms]
        num = sum(w * a[rows, :] for w, a in zip(ws, acc_br))
        den = sum(w * l[rows, :] for w, l in zip(ws, l_br))
        o_ref[rows, :] = (num / den).astype(BF16)
        return carry

    lax.fori_loop(0, TS // rows_per_step, merge, 0)


def _attention_prompt(slopes, aq, ak, av):
    b, s, _ = aq.shape
    ts = ATT_TILE
    blk = pl.BlockSpec((None, ts, LANES), lambda bi, hp, t, sl: (bi, t, hp))
    grid_spec = pltpu.PrefetchScalarGridSpec(
        num_scalar_prefetch=1,
        grid=(b, H_ATT // 2, s // ts),
        in_specs=[blk, blk, blk],
        out_specs=blk,
        scratch_shapes=[pltpu.VMEM((2 * ts, LANES), F32)] * 2
        + [pltpu.VMEM((ts, LANES), F32)] * (3 * len(DIL_BRANCHES)),
    )
    return pl.pallas_call(
        _att_kernel,
        out_shape=SDS((b, s, ATT_W), BF16),
        grid_spec=grid_spec,
        compiler_params=_params(("arbitrary", "arbitrary", "arbitrary")),
        name="attention_prompt",
    )(slopes, aq, ak, av)


def _sret_kernel(lg_ref, q_ref, k_ref, v_ref, rg_ref, gr_ref, st_ref, o_ref, sn_ref):
    q = q_ref[...].astype(F32)
    k = k_ref[...].astype(F32)
    v = v_ref[...].astype(F32)
    rg = rg_ref[...].astype(F32)
    gr = gr_ref[...]
    r_i = lax.broadcasted_iota(I32, (HEAD_DIM, HEAD_DIM), 0)
    c_i = lax.broadcasted_iota(I32, (HEAD_DIM, HEAD_DIM), 1)
    eye = r_i == c_i
    outs = []
    for h in range(H_RET):
        sl = slice(h * HEAD_DIM, (h + 1) * HEAD_DIM)
        qh, kh, vh = q[:, sl], k[:, sl], v[:, sl]
        gamma = jnp.exp(jnp.full((1, HEAD_DIM), lg_ref[h], F32))
        q_col = jnp.sum(jnp.where(eye, qh, 0.0), axis=1, keepdims=True)
        k_col = jnp.sum(jnp.where(eye, kh, 0.0), axis=1, keepdims=True)
        state = st_ref[h]
        o_inter = gamma * jnp.sum(q_col * state, axis=0, keepdims=True)
        o_intra = jnp.sum(qh * kh, axis=-1, keepdims=True) * vh
        sn_ref[h] = gamma * state + k_col * vh
        o = o_intra + o_inter
        mu = jnp.mean(o, axis=-1, keepdims=True)
        xc = o - mu
        var = jnp.mean(xc * xc, axis=-1, keepdims=True)
        outs.append(xc * lax.rsqrt(var + EPS) * gr[:, sl] * _silu(rg[:, sl]))
    o_ref[...] = jnp.concatenate(outs, axis=-1).astype(BF16)


def _retention_sample(lg, rq, rk, rv, rg, g_ret, state):
    db = rq.shape[0]
    row = pl.BlockSpec((None, 1, RET_W), lambda i: (i, 0, 0))
    st = pl.BlockSpec((None, H_RET, HEAD_DIM, HEAD_DIM), lambda i: (i, 0, 0, 0))
    r3 = lambda a: a.reshape(db, 1, RET_W)
    return pl.pallas_call(
        _sret_kernel,
        out_shape=[SDS((db, 1, RET_W), BF16), SDS(state.shape, F32)],
        grid=(db,),
        in_specs=[pl.BlockSpec(memory_space=pltpu.SMEM), row, row, row, row,
                  pl.BlockSpec((1, RET_W), lambda i: (0, 0)), st],
        out_specs=[row, st],
        compiler_params=_params(("arbitrary",)),
        name="retention_sample",
    )(lg, r3(rq), r3(rk), r3(rv), r3(rg), g_ret, state)


def _satt_kernel(sl_ref, qt_ref, knt_ref, vnt_ref, kt_ref, vt_ref, o_ref):
    win = kt_ref.shape[-1]
    dist = win - lax.broadcasted_iota(I32, (1, win), 1)
    count = jnp.zeros((1, win), F32)
    for window, dil in DIL_BRANCHES:
        reads = ((dist & (dil - 1)) == 0) & (dist <= window)
        count = count + jnp.where(reads, 1.0, 0.0)
    dist_f = dist.astype(F32)
    nbr = float(len(DIL_BRANCHES))
    for h in range(H_ATT):
        q = qt_ref[:, h:h + 1]
        s = jnp.sum(kt_ref[h] * q, axis=0, keepdims=True) - sl_ref[h] * dist_f
        s = jnp.where(count > 0.0, s, -jnp.inf)
        s_new = jnp.sum(q * knt_ref[:, h:h + 1], axis=0, keepdims=True)
        m = jnp.maximum(jnp.max(s, axis=-1, keepdims=True), s_new)
        p = count * jnp.exp(s - m)
        p_new = nbr * jnp.exp(s_new - m)
        den = jnp.sum(p, axis=-1, keepdims=True) + p_new
        num = jnp.sum(vt_ref[h] * p, axis=-1, keepdims=True) + p_new * vnt_ref[:, h:h + 1]
        o_ref[:, h:h + 1] = num / den


def _attention_sample(slopes, aq, ak, av, win_k, win_v):
    db = aq.shape[0]
    win = win_k.shape[1]
    col = pl.BlockSpec((None, HEAD_DIM, H_ATT), lambda i: (i, 0, 0))
    cache = pl.BlockSpec((None, H_ATT, HEAD_DIM, win), lambda i: (i, 0, 0, 0))
    cols = lambda a: jnp.transpose(a.reshape(db, H_ATT, HEAD_DIM), (0, 2, 1))
    keys_minor = lambda c: jnp.transpose(c, (0, 2, 3, 1))
    out = pl.pallas_call(
        _satt_kernel,
        out_shape=SDS((db, HEAD_DIM, H_ATT), F32),
        grid=(db,),
        in_specs=[pl.BlockSpec(memory_space=pltpu.SMEM), col, col, col, cache, cache],
        out_specs=col,
        compiler_params=_params(("arbitrary",)),
        name="attention_sample",
    )(slopes, cols(aq), cols(ak), cols(av), keys_minor(win_k), keys_minor(win_v))
    return jnp.transpose(out, (0, 2, 1)).reshape(db, ATT_W).astype(BF16)


def _outproj_kernel(x_ref, ret_ref, att_ref, wo_ref, gt1_ref, g2_ref, sc2_ref, sh2_ref, wqh_ref, wql_ref,
                    x1_ref, h2_ref, qp_ref):
    mix = _dot(ret_ref[...], wo_ref[0:RET_W, :]) + _dot(att_ref[...], wo_ref[RET_W:, :])
    x1 = x_ref[...] + gt1_ref[...] * mix
    x1_ref[...] = x1
    ms = jnp.mean(x1 * x1, axis=-1, keepdims=True)
    h2 = x1 * lax.rsqrt(ms + EPS) * g2_ref[...]
    h2 = h2 * (1.0 + sc2_ref[...]) + sh2_ref[...]
    h2_ref[...] = h2
    hi, lo = _split(h2)
    wqh = wqh_ref[...]
    qp_ref[...] = _dot(hi, wqh) + (_dot(lo, wqh) + _dot(hi, wql_ref[...]))


def _outproj(x2d, ret, att, wo_bf, gt1, g2, sc2, sh2, wq_hi, wq_lo, tb):
    n = x2d.shape[0]
    groups, rows, _ = gt1.shape
    tiles_per_group = n // tb // groups
    mod_spec = pl.BlockSpec((None, rows, D_MODEL), lambda i: (i // tiles_per_group, 0, 0))
    tok = lambda w: pl.BlockSpec((tb, w), lambda i: (i, 0))
    full = lambda a: pl.BlockSpec(a.shape, lambda i: (0,) * a.ndim)
    return pl.pallas_call(
        _outproj_kernel,
        out_shape=[SDS((n, D_MODEL), F32)] * 3,
        grid=(n // tb,),
        in_specs=[tok(D_MODEL), tok(RET_W), tok(ATT_W), full(wo_bf), mod_spec, full(g2), mod_spec, mod_spec,
                  full(wq_hi), full(wq_lo)],
        out_specs=[tok(D_MODEL)] * 3,
        compiler_params=_params(("arbitrary",)),
        name="outproj",
    )(x2d, ret, att, wo_bf, gt1, g2, sc2, sh2, wq_hi, wq_lo)


def _top16(sc, ids=None):
    if ids is None:
        ids = lax.broadcasted_iota(I32, sc.shape, 0)
    ids = ids.astype(F32)
    vals, idxs = [], []
    for _ in range(PEER_TOPK):
        m = jnp.max(sc, axis=0, keepdims=True)
        idx = jnp.min(jnp.where(sc == m, ids, jnp.inf), axis=0, keepdims=True)
        vals.append(m)
        idxs.append(idx)
        sc = jnp.where(ids == idx, -jnp.inf, sc)
    return jnp.concatenate(vals, axis=0), jnp.concatenate(idxs, axis=0).astype(I32)


def _pair_candidates(s1, s2):
    k = PEER_TOPK
    half = k // 2
    pieces = [s1[0:1, :] + s2]
    pieces += [s1[a:a + 1, :] + s2[0:half, :] for a in range(1, half)]
    pieces.append(s1[half:, :] + s2[0:1, :])
    cand = jnp.concatenate(pieces, axis=0)
    r = lax.broadcasted_iota(I32, cand.shape, 0)
    mid = r - k
    a_mid = 1 + lax.shift_right_logical(mid, 3)
    b_mid = mid & (half - 1)
    tail_start = k + half * (half - 1)
    pos = jnp.where(r < k, r, jnp.where(r < tail_start, a_mid * k + b_mid, (half + r - tail_start) * k))
    reachable = jnp.logical_or(jnp.logical_or(r < k, r >= tail_start), (a_mid + 1) * (b_mid + 1) <= k)
    return jnp.where(reachable, cand, -jnp.inf), pos


TOPK_HEADS = 8


def _topk_one_head(q, skh_ref, skl_ref, hh):
    qh, ql = _split(q)

    def scores(c):
        kh = skh_ref[hh, c]
        return _dot_nt(kh, qh) + (_dot_nt(kh, ql) + _dot_nt(skl_ref[hh, c], qh))

    s1, i1 = _top16(scores(0))
    s2, i2 = _top16(scores(1))
    top_s, pos = _top16(*_pair_candidates(s1, s2))
    a_sel = lax.shift_right_logical(pos, 4)
    b_sel = pos & (PEER_TOPK - 1)
    io = lax.broadcasted_iota(I32, i1.shape, 0)
    rows = []
    for r in range(PEER_TOPK):
        ia = jnp.sum(jnp.where(io == a_sel[r:r + 1, :], i1, 0), axis=0, keepdims=True)
        ib = jnp.sum(jnp.where(io == b_sel[r:r + 1, :], i2, 0), axis=0, keepdims=True)
        rows.append(ia * N_KEYS + ib)
    ex = jnp.exp(top_s - top_s[0:1, :])
    return jnp.concatenate(rows, axis=0) * ROW_SUB, ex / jnp.sum(ex, axis=0, keepdims=True)


def _topk_kernel(q_ref, skh_ref, skl_ref, e_ref, g_ref, e_scr, g_scr):
    step = pl.program_id(1)
    for hh in range(TOPK_HEADS):
        e, g = _topk_one_head(q_ref[:, LANES * hh:LANES * (hh + 1)], skh_ref, skl_ref, hh)
        slot_rows = pl.ds(pl.multiple_of((step * TOPK_HEADS + hh) * PEER_TOPK, PEER_TOPK), PEER_TOPK)
        e_scr[slot_rows, :] = e
        g_scr[slot_rows, :] = g

    @pl.when(step == pl.num_programs(1) - 1)
    def _():
        e_ref[...] = e_scr[...].T
        g_ref[...] = g_scr[...].T


def _peer_topk(qp, sk_hi, sk_lo, tt):
    n = qp.shape[0]
    out_blk = pl.BlockSpec((tt, PEER_SLOTS), lambda i, h: (i, 0))
    sk_blk = pl.BlockSpec((TOPK_HEADS, 2, N_KEYS, LANES), lambda i, h: (h, 0, 0, 0))
    return pl.pallas_call(
        _topk_kernel,
        out_shape=[SDS((n, PEER_SLOTS), I32), SDS((n, PEER_SLOTS), F32)],
        grid=(n // tt, PEER_HEADS // TOPK_HEADS),
        in_specs=[pl.BlockSpec((tt, TOPK_HEADS * LANES), lambda i, h: (i, h)), sk_blk, sk_blk],
        out_specs=[out_blk, out_blk],
        scratch_shapes=[pltpu.VMEM((PEER_SLOTS, tt), I32), pltpu.VMEM((PEER_SLOTS, tt), F32)],
        compiler_params=_params(("arbitrary", "arbitrary")),
        name="peer_topk",
    )(qp, sk_hi, sk_lo)


def _gather_rows(idx_ref, tab_ref, t):
    rows = []
    for k in range(PEER_SLOTS):
        e = pl.multiple_of(idx_ref[t, k], ROW_SUB)
        rows.append(tab_ref[pl.ds(e, ROW_SUB), :])
    return pltpu.bitcast(jnp.concatenate(rows, axis=0), BF16)


def _diag_mask():
    sub = lax.broadcasted_iota(I32, (SUBLANES, D_MODEL), 0)
    col = lax.broadcasted_iota(I32, (SUBLANES, D_MODEL), 1)
    return (col & (SUBLANES - 1)) == sub


def _for_each_token(tb, one_token):
    def group(i, carry):
        for u in range(TOK_UNROLL):
            one_token(i * TOK_UNROLL + u)
        return carry

    lax.fori_loop(0, tb // TOK_UNROLL, group, 0)


def _peer_u_kernel(idx_ref, tab_ref, h2_ref, g_ref, sel_ref, w_ref, xbuf, rall):
    tb = h2_ref.shape[0]
    for j in range(SUBLANES):
        xbuf[pl.ds(j, tb, stride=SUBLANES), :] = h2_ref[:, LANES * j:LANES * (j + 1)]
    diag = _diag_mask()

    def tok(t):
        rows = _gather_rows(idx_ref, tab_ref, t)
        xt = xbuf[pl.ds(pl.multiple_of(t * SUBLANES, SUBLANES), SUBLANES), :].astype(BF16)
        r = _dot_nt(xt, rows)
        rall[pl.ds(t, 1), :] = jnp.sum(jnp.where(diag, r, 0.0), axis=0, keepdims=True)

    _for_each_token(tb, tok)
    hi, lo = _split(rall[...])
    sel = sel_ref[...]
    a = _dot(hi, sel) + _dot(lo, sel)
    gelu = 0.5 * a * (1.0 + lax.erf(a * (2.0 ** -0.5)))
    w_ref[...] = g_ref[...] * gelu


def _peer_v_kernel(idx_ref, tab_ref, w_ref, x1_ref, gt2_ref, selt_ref, y_ref, wrep, ybuf):
    tb = x1_ref.shape[0]
    wrep[...] = _dot(w_ref[...].astype(BF16), selt_ref[...])
    diag = _diag_mask()

    def tok(t):
        rows = _gather_rows(idx_ref, tab_ref, t)
        wt = jnp.where(diag, wrep[pl.ds(t, 1), :], 0.0).astype(BF16)
        ybuf[pl.ds(pl.multiple_of(t * SUBLANES, SUBLANES), SUBLANES), :] = _dot(wt, rows)

    _for_each_token(tb, tok)
    for j in range(SUBLANES):
        cols = slice(LANES * j, LANES * (j + 1))
        y_ref[:, cols] = x1_ref[:, cols] + gt2_ref[:, cols] * ybuf[pl.ds(j, tb, stride=SUBLANES), :]


def _peer_u(idx, tab, h2, g, sel, tb):
    n = h2.shape[0]
    tok = lambda w: pl.BlockSpec((tb, w), lambda i: (i, 0))
    return pl.pallas_call(
        _peer_u_kernel,
        out_shape=SDS((n, PEER_SLOTS), F32),
        grid=(n // tb,),
        in_specs=[pl.BlockSpec((tb, PEER_SLOTS), lambda i: (i, 0), memory_space=pltpu.SMEM),
                  pl.BlockSpec(memory_space=pltpu.VMEM), tok(D_MODEL), tok(PEER_SLOTS),
                  pl.BlockSpec(sel.shape, lambda i: (0, 0))],
        out_specs=tok(PEER_SLOTS),
        scratch_shapes=[pltpu.VMEM((tb * SUBLANES, LANES), F32), pltpu.VMEM((tb, D_MODEL), F32)],
        compiler_params=_params(("arbitrary",)),
        name="peer_u",
    )(idx, tab, h2, g, sel)


def _peer_v(idx, tab, w, x1, gt2, selt, tb):
    n = x1.shape[0]
    groups, rows, _ = gt2.shape
    tiles_per_group = n // tb // groups
    tok = lambda wd: pl.BlockSpec((tb, wd), lambda i: (i, 0))
    return pl.pallas_call(
        _peer_v_kernel,
        out_shape=SDS((n, D_MODEL), F32),
        grid=(n // tb,),
        in_specs=[pl.BlockSpec((tb, PEER_SLOTS), lambda i: (i, 0), memory_space=pltpu.SMEM),
                  pl.BlockSpec(memory_space=pltpu.VMEM), tok(PEER_SLOTS), tok(D_MODEL),
                  pl.BlockSpec((None, rows, D_MODEL), lambda i: (i // tiles_per_group, 0, 0)),
                  pl.BlockSpec(selt.shape, lambda i: (0, 0))],
        out_specs=tok(D_MODEL),
        scratch_shapes=[pltpu.VMEM((tb, D_MODEL), F32), pltpu.VMEM((tb * SUBLANES, LANES), F32)],
        compiler_params=_params(("arbitrary",)),
        name="peer_v",
    )(idx, tab, w, x1, gt2, selt)


def _pack_table(tab):
    e = tab.shape[0]
    bits = lax.bitcast_convert_type(tab.astype(BF16), jnp.uint16).astype(jnp.uint32)
    bits = bits.reshape(e, ROW_SUB, 2, LANES)
    word = bits[:, :, 0, :] | (bits[:, :, 1, :] << 16)
    return lax.bitcast_convert_type(word, I32).reshape(e * ROW_SUB, LANES)


def _block_avg(width):
    i = np.arange(width) // HEAD_DIM
    return jnp.asarray((i[:, None] == i[None, :]).astype(np.float32) / HEAD_DIM, BF16)


def _slot_select():
    col = np.arange(D_MODEL) // SUBLANES
    return jnp.asarray((col[:, None] == np.arange(PEER_SLOTS)[None, :]).astype(np.float32), BF16)


def _ret_log_decay():
    return jnp.log1p(-jnp.exp2(-5.0 - jnp.arange(H_RET, dtype=F32)))


def _alibi_slopes():
    return jnp.exp2(-8.0 * (jnp.arange(H_ATT, dtype=F32) + 1.0) / H_ATT)


def _pad_sub_keys(sub_keys):
    z = jnp.zeros_like(sub_keys[:, 0])
    return jnp.stack([jnp.concatenate([sub_keys[:, 0], z], axis=-1),
                      jnp.concatenate([z, sub_keys[:, 1]], axis=-1)], axis=1)


def _group_w_out(w):
    w4 = w.reshape(H_RET, 2, HEAD_DIM, D_MODEL)
    return jnp.concatenate([w4[:, 0].reshape(RET_W, D_MODEL), w4[:, 1].reshape(ATT_W, D_MODEL)], axis=0)


def _split_w(w):
    hi = w.astype(BF16)
    return hi, (w - hi.astype(F32)).astype(BF16)


def _token_tile(n, pref):
    return pref if n % pref == 0 else n


def _peer(h2, qp, x1, gt2, consts, weights):
    n = h2.shape[0]
    idx, gate = _peer_topk(qp, weights["sk_hi"], weights["sk_lo"], _token_tile(n, 256))
    tb = _token_tile(n, 256)
    w = _peer_u(idx, weights["u_tab"], h2, gate, consts["sel"], tb)
    return _peer_v(idx, weights["v_tab"], w, x1, gt2, consts["selt"], tb)


def _mix_and_peer(x2d, ret, att, mods, consts, weights):
    sh1, sc1, gt1, sh2, sc2, gt2 = mods
    n = x2d.shape[0]
    x1, h2, qp = _outproj(x2d, ret, att, weights["w_out"], gt1, weights["g_norm2"], sc2, sh2,
                          weights["wq_hi"], weights["wq_lo"], _token_tile(n, 512))
    return _peer(h2, qp, x1, gt2, consts, weights)


def _project(x2d, mods, consts, weights):
    sh1, sc1 = mods[0], mods[1]
    n = x2d.shape[0]
    return _inproj(x2d, sc1, sh1, weights["g_norm1"], weights["w_in"], weights["g_qn"], weights["g_kn"],
                   consts["avg_att"], _token_tile(n, 512))


def kernel(x_prompt, x_sample, c_prompt, c_sample, state_ret, cache_win_k, cache_win_v, w_ada, b_ada, g_norm1, w_in, g_qn, g_kn, g_ret, w_out, g_norm2, w_pq, sub_keys, peer_u, peer_v):
    b, s, d = x_prompt.shape
    db, ds_, _ = x_sample.shape
    depth = w_ada.shape[0]
    assert d == D_MODEL and ds_ == 1 and s % ATT_TILE == 0
    win = cache_win_k.shape[2]
    assert win == MAX_WINDOW

    consts = {
        "avg_att": _block_avg(ATT_W),
        "sel": _slot_select(),
        "selt": _slot_select().T,
        "lg": _ret_log_decay(),
        "slopes": _alibi_slopes(),
    }

    y_p = x_prompt.reshape(b * s, d)
    y_s = x_sample.reshape(db, d)
    outs = {k: [] for k in ("ret_p", "wk_p", "wv_p", "ret_s", "wk_s", "wv_s")}
    for l in range(depth):
        wq_hi, wq_lo = _split_w(w_pq[l])
        sk_hi, sk_lo = _split_w(_pad_sub_keys(sub_keys[l]))
        weights = {
            "g_norm1": g_norm1[l].reshape(1, d), "g_norm2": g_norm2[l].reshape(1, d),
            "w_in": w_in[l].astype(BF16), "w_out": _group_w_out(w_out[l]).astype(BF16),
            "g_qn": g_qn[l].reshape(1, ATT_W), "g_kn": g_kn[l].reshape(1, ATT_W),
            "wq_hi": wq_hi, "wq_lo": wq_lo, "sk_hi": sk_hi, "sk_lo": sk_lo,
            "u_tab": _pack_table(peer_u[l]), "v_tab": _pack_table(peer_v[l]),
        }
        gr = g_ret[l].reshape(1, RET_W)

        mod = _adaln(jnp.concatenate([c_prompt, c_sample], axis=0), w_ada[l], b_ada[l])
        mod6 = jnp.split(mod, 6, axis=-1)
        mods_p = [m[:b].reshape(b, 1, d) for m in mod6]
        mods_s = [m[b:].reshape(1, db, d) for m in mod6]

        rq, rk, rv, rg, aq, ak, av = _project(y_p, mods_p, consts, weights)
        seq = lambda a: a.reshape(b, s, a.shape[-1])
        ret, ret_state = _retention_prompt(consts["lg"], seq(rq), seq(rk), seq(rv), seq(rg), gr, 1024)
        att = _attention_prompt(consts["slopes"], seq(aq), seq(ak), seq(av))
        y_p = _mix_and_peer(y_p, ret.reshape(b * s, RET_W), att.reshape(b * s, ATT_W), mods_p, consts, weights)
        keep = min(MAX_WINDOW, s)
        outs["ret_p"].append(ret_state)
        outs["wk_p"].append(seq(ak)[:, s - keep:].reshape(b, keep, H_ATT, HEAD_DIM))
        outs["wv_p"].append(seq(av)[:, s - keep:].reshape(b, keep, H_ATT, HEAD_DIM))

        rq, rk, rv, rg, aq, ak, av = _project(y_s, mods_s, consts, weights)
        ret, new_state = _retention_sample(consts["lg"], rq, rk, rv, rg, gr, state_ret[l])
        att = _attention_sample(consts["slopes"], aq, ak, av, cache_win_k[l], cache_win_v[l])
        y_s = _mix_and_peer(y_s, ret.reshape(db, RET_W), att.reshape(db, ATT_W), mods_s, consts, weights)
        outs["ret_s"].append(new_state)
        outs["wk_s"].append(ak.reshape(db, 1, H_ATT, HEAD_DIM))
        outs["wv_s"].append(av.reshape(db, 1, H_ATT, HEAD_DIM))

    st = lambda k: jnp.stack(outs[k])
    return (y_p.reshape(b, s, d), y_s.reshape(db, 1, d), st("ret_p"), st("wk_p"), st("wv_p"),
            st("ret_s"), st("wk_s"), st("wv_s"))
```

```python
import functools
import math

import numpy as np
import jax
import jax.numpy as jnp
from jax import lax
from jax.experimental import pallas as pl
from jax.experimental.pallas import tpu as pltpu

F32 = jnp.float32
BF16 = jnp.bfloat16
I32 = jnp.int32

D_MODEL = 1024
HEAD_DIM = 64
RET_W = 512
ATT_W = 512
H_RET = 8
H_ATT = 8
N_PROJ = 4 * RET_W + 3 * ATT_W
RET_CHUNK = 128
DIL_BRANCHES = ((128, 1), (512, 4), (2048, 16))
NW = 128
MAX_WINDOW = 2048
N_KEYS = 128
PEER_HEADS = 8
PEER_TOPK = 16
PEER_SLOTS = PEER_HEADS * PEER_TOPK
EPS = 1e-6

LANES = 128
SUBLANES = 8
ROW_WORDS = D_MODEL // 2
ROW_SUB = ROW_WORDS // LANES
VMEM_LIMIT = 56 * 1024 * 1024
RET_UNROLL = 4
TOK_UNROLL = 32
TOK_UNROLL_WITH_TOPK = 16

SDS = jax.ShapeDtypeStruct


def _dot(a, b):
    return jnp.dot(a, b, preferred_element_type=F32)


def _dot_nt(a, b):
    return lax.dot_general(a, b, (((1,), (1,)), ((), ())), preferred_element_type=F32)


def _dot_tn(a, b):
    return lax.dot_general(a, b, (((0,), (0,)), ((), ())), preferred_element_type=F32)


def _split(x):
    hi = x.astype(BF16)
    lo = (x - hi.astype(F32)).astype(BF16)
    return hi, lo


def _silu(x):
    return x * jax.nn.sigmoid(x)


def _params(sem, vmem=VMEM_LIMIT):
    return pltpu.CompilerParams(dimension_semantics=sem, vmem_limit_bytes=vmem)


def _adaln_kernel(c_ref, w_ref, b_ref, o_ref):
    s = _silu(c_ref[...])
    o_ref[...] = jnp.dot(s, w_ref[...], preferred_element_type=F32,
                         precision=lax.Precision.HIGHEST) + b_ref[...]


def _adaln(c, w_ada, b_ada):
    rows = c.shape[0]
    tn = 512
    return pl.pallas_call(
        _adaln_kernel,
        out_shape=SDS((rows, 6 * D_MODEL), F32),
        grid=(6 * D_MODEL // tn,),
        in_specs=[pl.BlockSpec((rows, D_MODEL), lambda j: (0, 0)),
                  pl.BlockSpec((D_MODEL, tn), lambda j: (0, j)),
                  pl.BlockSpec((1, tn), lambda j: (0, j))],
        out_specs=pl.BlockSpec((rows, tn), lambda j: (0, j)),
        compiler_params=_params(("arbitrary",)),
        name="adaln",
    )(c, w_ada, b_ada.reshape(1, -1))


def _inproj_kernel(x_ref, sc_ref, sh_ref, g1_ref, w_ref, gq_ref, gk_ref, bd_ref,
                   rq_ref, rk_ref, rv_ref, rg_ref, aq_ref, ak_ref, av_ref):
    x = x_ref[...]
    ms = jnp.mean(x * x, axis=-1, keepdims=True)
    h = x * lax.rsqrt(ms + EPS) * g1_ref[...]
    h = h * (1.0 + sc_ref[...]) + sh_ref[...]
    p = _dot(h.astype(BF16), w_ref[...])
    scale = HEAD_DIM ** -0.5
    rq_ref[...] = p[:, 0:RET_W].astype(BF16)
    rk_ref[...] = (p[:, RET_W:2 * RET_W] * scale).astype(BF16)
    rv_ref[...] = p[:, 2 * RET_W:3 * RET_W].astype(BF16)
    rg_ref[...] = p[:, 3 * RET_W:4 * RET_W].astype(BF16)
    o = 4 * RET_W

    def head_norm(a, g):
        msq = _dot((a * a).astype(BF16), bd_ref[...])
        return a * lax.rsqrt(msq + EPS) * g

    aq_ref[...] = head_norm(p[:, o:o + ATT_W], gq_ref[...]) * scale
    ak_ref[...] = head_norm(p[:, o + ATT_W:o + 2 * ATT_W], gk_ref[...])
    av_ref[...] = p[:, o + 2 * ATT_W:o + 3 * ATT_W]


def _inproj(x2d, sc, sh, g1, w_in_bf, gq, gk, bd, tb):
    n = x2d.shape[0]
    groups, rows, _ = sc.shape
    tiles_per_group = n // tb // groups
    mod_spec = pl.BlockSpec((None, rows, D_MODEL), lambda i: (i // tiles_per_group, 0, 0))
    tok = lambda w: pl.BlockSpec((tb, w), lambda i: (i, 0))
    full = lambda a: pl.BlockSpec(a.shape, lambda i: (0,) * a.ndim)
    outs = [SDS((n, RET_W), BF16)] * 4 + [SDS((n, ATT_W), F32)] * 3
    return pl.pallas_call(
        _inproj_kernel,
        out_shape=outs,
        grid=(n // tb,),
        in_specs=[tok(D_MODEL), mod_spec, mod_spec, full(g1), full(w_in_bf), full(gq), full(gk), full(bd)],
        out_specs=[tok(RET_W)] * 4 + [tok(ATT_W)] * 3,
        compiler_params=_params(("arbitrary",)),
        name="inproj",
    )(x2d, sc, sh, g1, w_in_bf, gq, gk, bd)


def _ret_kernel(lg_ref, q_ref, k_ref, v_ref, rg_ref, gr_ref, o_ref, st_ref, s_scr, *, nchunk):
    hp = pl.program_id(1)
    t = pl.program_id(2)
    C = RET_CHUNK
    lane = lax.broadcasted_iota(I32, (1, LANES), 1)
    m0 = lane < HEAD_DIM
    lg0 = lg_ref[2 * hp]
    lg1 = lg_ref[2 * hp + 1]
    lgv = jnp.where(m0, lg0, lg1)

    @pl.when(t == 0)
    def _():
        s_scr[...] = jnp.zeros_like(s_scr)

    ii = lax.broadcasted_iota(I32, (C, C), 0)
    jj = lax.broadcasted_iota(I32, (C, C), 1)
    diff = ii - jj
    dpos = jnp.maximum(diff, 0).astype(F32)
    dec01 = jnp.concatenate([jnp.where(diff >= 0, jnp.exp(dpos * lg0), 0.0),
                             jnp.where(diff >= 0, jnp.exp(dpos * lg1), 0.0)], axis=0)

    def head_mean(x):
        s0 = jnp.sum(jnp.where(m0, x, 0.0), axis=-1, keepdims=True)
        s1 = jnp.sum(jnp.where(m0, 0.0, x), axis=-1, keepdims=True)
        return jnp.where(m0, s0, s1) * (1.0 / HEAD_DIM)
    ri = lax.broadcasted_iota(I32, (C, LANES), 0).astype(F32)
    w_start = jnp.exp((ri + 1.0) * lgv)
    w_end = jnp.exp((float(C) - 1.0 - ri) * lgv)
    chunk_decay = jnp.exp(float(C) * lgv)
    rr = lax.broadcasted_iota(I32, (LANES, LANES), 0)
    cc = lax.broadcasted_iota(I32, (LANES, LANES), 1)
    same_head = (rr < HEAD_DIM) == (cc < HEAD_DIM)
    gr = gr_ref[...]

    def chunk(c, state):
        rows = pl.ds(pl.multiple_of(c * C, C), C)
        q = q_ref[rows, :]
        k = k_ref[rows, :]
        v = v_ref[rows, :]
        qf = q.astype(F32)
        kf = k.astype(F32)
        q01 = jnp.concatenate([jnp.where(m0, qf, 0.0), jnp.where(m0, 0.0, qf)], axis=0).astype(BF16)
        p01 = (_dot_nt(q01, k) * dec01).astype(BF16)
        pv = _dot(p01, v)
        o_intra = jnp.where(m0, pv[0:C, :], pv[C:, :])
        o_inter = _dot((qf * w_start).astype(BF16), state.astype(BF16))
        o = o_intra + o_inter
        kv = _dot_tn((kf * w_end).astype(BF16), v)
        mu = head_mean(o)
        xc = o - mu
        var = head_mean(xc * xc)
        y = xc * lax.rsqrt(var + EPS) * gr
        y = y * _silu(rg_ref[rows, :].astype(F32))
        o_ref[rows, :] = y.astype(BF16)
        return chunk_decay * state + jnp.where(same_head, kv, 0.0)

    def chunk_group(i, state):
        for u in range(RET_UNROLL):
            state = chunk(i * RET_UNROLL + u, state)
        return state

    s_scr[...] = lax.fori_loop(0, nchunk // RET_UNROLL, chunk_group, s_scr[...])

    @pl.when(t == pl.num_programs(2) - 1)
    def _():
        state = s_scr[...]
        st_ref[0] = state[0:HEAD_DIM, 0:HEAD_DIM]
        st_ref[1] = state[HEAD_DIM:, HEAD_DIM:]


def _retention_prompt(lg, rq, rk, rv, rg, g_ret, cb):
    b, s, _ = rq.shape
    nchunk = cb // RET_CHUNK
    blk = pl.BlockSpec((None, cb, LANES), lambda bi, hp, t, lg_: (bi, t, hp))
    grid_spec = pltpu.PrefetchScalarGridSpec(
        num_scalar_prefetch=1,
        grid=(b, H_RET // 2, s // cb),
        in_specs=[blk, blk, blk, blk,
                  pl.BlockSpec((1, LANES), lambda bi, hp, t, lg_: (0, hp))],
        out_specs=[blk, pl.BlockSpec((None, 2, HEAD_DIM, HEAD_DIM), lambda bi, hp, t, lg_: (bi, hp, 0, 0))],
        scratch_shapes=[pltpu.VMEM((LANES, LANES), F32)],
    )
    return pl.pallas_call(
        functools.partial(_ret_kernel, nchunk=nchunk),
        out_shape=[SDS((b, s, RET_W), BF16), SDS((b, H_RET, HEAD_DIM, HEAD_DIM), F32)],
        grid_spec=grid_spec,
        compiler_params=_params(("arbitrary", "arbitrary", "arbitrary")),
        name="retention_prompt",
    )(lg, rq, rk, rv, rg, g_ret)


ATT_TILE = MAX_WINDOW


ATT_UNROLL = 8


def _att_kernel(sl_ref, q_ref, k_ref, v_ref, o_ref, kk, vv, *stats):
    hp = pl.program_id(1)
    t = pl.program_id(2)
    TS = ATT_TILE
    nbr = len(DIL_BRANCHES)
    m_br, l_br, acc_br = stats[:nbr], stats[nbr:2 * nbr], stats[2 * nbr:]
    cur = pl.multiple_of((t & 1) * TS, TS)

    @pl.when(t == 0)
    def _():
        kk[TS:2 * TS, :] = jnp.zeros((TS, LANES), F32)
        vv[TS:2 * TS, :] = jnp.zeros((TS, LANES), F32)

    kk[pl.ds(cur, TS), :] = k_ref[...]
    vv[pl.ds(cur, TS), :] = v_ref[...]

    lane = lax.broadcasted_iota(I32, (1, LANES), 1)
    m0 = lane < HEAD_DIM
    head_lanes = (m0, jnp.logical_not(m0))
    ii = lax.broadcasted_iota(I32, (NW, 2 * NW), 0)
    jj = lax.broadcasted_iota(I32, (NW, 2 * NW), 1)
    off = ii + NW - jj
    band = (off >= 0) & (off <= NW)
    off_f = off.astype(F32)
    ones_h = [jnp.broadcast_to(jnp.where(hl, 1.0, 0.0).astype(BF16), (2 * NW, LANES)) for hl in head_lanes]

    for br, (_, dil) in enumerate(DIL_BRANCHES):
        nblk = TS // (NW * dil)
        shift = int(math.log2(nblk))
        bias = [jnp.where(band, -(sl_ref[2 * hp + hh] * float(dil)) * off_f, -jnp.inf) for hh in range(2)]

        def one_block(bs, dil=dil, nblk=nblk, shift=shift, br=br, bias=bias):
            r = lax.shift_right_logical(bs, shift)
            blk = bs & (nblk - 1)
            qs = r + dil * NW * blk
            q_rows = pl.ds(qs, NW, stride=dil)
            own = pl.ds(cur + qs, NW, stride=dil)
            prev = pl.ds((cur + qs - dil * NW) & (2 * TS - 1), NW, stride=dil)
            qb = q_ref[q_rows, :]
            kb = jnp.concatenate([kk[prev, :], kk[own, :]], axis=0).astype(BF16)
            vb = jnp.concatenate([vv[prev, :], vv[own, :]], axis=0)
            first_key = jnp.where(jnp.logical_or(t > 0, blk > 0), 0, NW)
            started = jj >= first_key
            res = None
            mx = []
            for hh in range(2):
                qh = jnp.where(head_lanes[hh], qb, 0.0).astype(BF16)
                s = jnp.where(started, _dot_nt(qh, kb) + bias[hh], -jnp.inf)
                mh = jnp.max(s, axis=-1, keepdims=True)
                p = jnp.exp(s - mh).astype(BF16)
                vh = jnp.concatenate([jnp.where(head_lanes[hh], vb, 0.0).astype(BF16), ones_h[hh]], axis=1)
                part = _dot(p, vh)
                res = part if res is None else res + part
                mx.append(mh)
            m_br[br][q_rows, :] = jnp.where(m0, mx[0], mx[1])
            acc_br[br][q_rows, :] = res[:, :LANES]
            l_br[br][q_rows, :] = res[:, LANES:]

        def body(i, carry, one_block=one_block):
            for u in range(ATT_UNROLL):
                one_block(i * ATT_UNROLL + u)
            return carry

        lax.fori_loop(0, TS // NW // ATT_UNROLL, body, 0)

    rows_per_step = 256

    def merge(i, carry):
        rows = pl.ds(pl.multiple_of(i * rows_per_step, rows_per_step), rows_per_step)
        ms = [m[rows, :] for m in m_br]
        m_all = functools.reduce(jnp.maximum, ms)
        ws = [jnp.exp(m - m_all) for m in ms]
        num = sum(w * a[rows, :] for w, a in zip(ws, acc_br))
        den = sum(w * l[rows, :] for w, l in zip(ws, l_br))
        o_ref[rows, :] = (num / den).astype(BF16)
        return carry

    lax.fori_loop(0, TS // rows_per_step, merge, 0)


def _attention_prompt(slopes, aq, ak, av):
    b, s, _ = aq.shape
    ts = ATT_TILE
    blk = pl.BlockSpec((None, ts, LANES), lambda bi, hp, t, sl: (bi, t, hp))
    grid_spec = pltpu.PrefetchScalarGridSpec(
        num_scalar_prefetch=1,
        grid=(b, H_ATT // 2, s // ts),
        in_specs=[blk, blk, blk],
        out_specs=blk,
        scratch_shapes=[pltpu.VMEM((2 * ts, LANES), F32)] * 2
        + [pltpu.VMEM((ts, LANES), F32)] * (3 * len(DIL_BRANCHES)),
    )
    return pl.pallas_call(
        _att_kernel,
        out_shape=SDS((b, s, ATT_W), BF16),
        grid_spec=grid_spec,
        compiler_params=_params(("arbitrary", "arbitrary", "arbitrary")),
        name="attention_prompt",
    )(slopes, aq, ak, av)


def _sret_kernel(lg_ref, q_ref, k_ref, v_ref, rg_ref, gr_ref, st_ref, o_ref, sn_ref):
    q = q_ref[...].astype(F32)
    k = k_ref[...].astype(F32)
    v = v_ref[...].astype(F32)
    rg = rg_ref[...].astype(F32)
    gr = gr_ref[...]
    r_i = lax.broadcasted_iota(I32, (HEAD_DIM, HEAD_DIM), 0)
    c_i = lax.broadcasted_iota(I32, (HEAD_DIM, HEAD_DIM), 1)
    eye = r_i == c_i
    outs = []
    for h in range(H_RET):
        sl = slice(h * HEAD_DIM, (h + 1) * HEAD_DIM)
        qh, kh, vh = q[:, sl], k[:, sl], v[:, sl]
        gamma = jnp.exp(jnp.full((1, HEAD_DIM), lg_ref[h], F32))
        q_col = jnp.sum(jnp.where(eye, qh, 0.0), axis=1, keepdims=True)
        k_col = jnp.sum(jnp.where(eye, kh, 0.0), axis=1, keepdims=True)
        state = st_ref[h]
        o_inter = gamma * jnp.sum(q_col * state, axis=0, keepdims=True)
        o_intra = jnp.sum(qh * kh, axis=-1, keepdims=True) * vh
        sn_ref[h] = gamma * state + k_col * vh
        o = o_intra + o_inter
        mu = jnp.mean(o, axis=-1, keepdims=True)
        xc = o - mu
        var = jnp.mean(xc * xc, axis=-1, keepdims=True)
        outs.append(xc * lax.rsqrt(var + EPS) * gr[:, sl] * _silu(rg[:, sl]))
    o_ref[...] = jnp.concatenate(outs, axis=-1).astype(BF16)


def _retention_sample(lg, rq, rk, rv, rg, g_ret, state):
    db = rq.shape[0]
    row = pl.BlockSpec((None, 1, RET_W), lambda i: (i, 0, 0))
    st = pl.BlockSpec((None, H_RET, HEAD_DIM, HEAD_DIM), lambda i: (i, 0, 0, 0))
    r3 = lambda a: a.reshape(db, 1, RET_W)
    return pl.pallas_call(
        _sret_kernel,
        out_shape=[SDS((db, 1, RET_W), BF16), SDS(state.shape, F32)],
        grid=(db,),
        in_specs=[pl.BlockSpec(memory_space=pltpu.SMEM), row, row, row, row,
                  pl.BlockSpec((1, RET_W), lambda i: (0, 0)), st],
        out_specs=[row, st],
        compiler_params=_params(("arbitrary",)),
        name="retention_sample",
    )(lg, r3(rq), r3(rk), r3(rv), r3(rg), g_ret, state)


def _satt_kernel(sl_ref, qt_ref, knt_ref, vnt_ref, kt_ref, vt_ref, o_ref):
    win = kt_ref.shape[-1]
    dist = win - lax.broadcasted_iota(I32, (1, win), 1)
    count = jnp.zeros((1, win), F32)
    for window, dil in DIL_BRANCHES:
        reads = ((dist & (dil - 1)) == 0) & (dist <= window)
        count = count + jnp.where(reads, 1.0, 0.0)
    dist_f = dist.astype(F32)
    nbr = float(len(DIL_BRANCHES))
    for h in range(H_ATT):
        q = qt_ref[:, h:h + 1]
        s = jnp.sum(kt_ref[h] * q, axis=0, keepdims=True) - sl_ref[h] * dist_f
        s = jnp.where(count > 0.0, s, -jnp.inf)
        s_new = jnp.sum(q * knt_ref[:, h:h + 1], axis=0, keepdims=True)
        m = jnp.maximum(jnp.max(s, axis=-1, keepdims=True), s_new)
        p = count * jnp.exp(s - m)
        p_new = nbr * jnp.exp(s_new - m)
        den = jnp.sum(p, axis=-1, keepdims=True) + p_new
        num = jnp.sum(vt_ref[h] * p, axis=-1, keepdims=True) + p_new * vnt_ref[:, h:h + 1]
        o_ref[:, h:h + 1] = num / den


def _attention_sample(slopes, aq, ak, av, win_k, win_v):
    db = aq.shape[0]
    win = win_k.shape[1]
    col = pl.BlockSpec((None, HEAD_DIM, H_ATT), lambda i: (i, 0, 0))
    cache = pl.BlockSpec((None, H_ATT, HEAD_DIM, win), lambda i: (i, 0, 0, 0))
    cols = lambda a: jnp.transpose(a.reshape(db, H_ATT, HEAD_DIM), (0, 2, 1))
    keys_minor = lambda c: jnp.transpose(c, (0, 2, 3, 1))
    out = pl.pallas_call(
        _satt_kernel,
        out_shape=SDS((db, HEAD_DIM, H_ATT), F32),
        grid=(db,),
        in_specs=[pl.BlockSpec(memory_space=pltpu.SMEM), col, col, col, cache, cache],
        out_specs=col,
        compiler_params=_params(("arbitrary",)),
        name="attention_sample",
    )(slopes, cols(aq), cols(ak), cols(av), keys_minor(win_k), keys_minor(win_v))
    return jnp.transpose(out, (0, 2, 1)).reshape(db, ATT_W).astype(BF16)


def _outproj_kernel(x_ref, ret_ref, att_ref, wo_ref, gt1_ref, g2_ref, sc2_ref, sh2_ref, wqh_ref, wql_ref,
                    x1_ref, h2_ref, qp_ref):
    mix = _dot(ret_ref[...], wo_ref[0:RET_W, :]) + _dot(att_ref[...], wo_ref[RET_W:, :])
    x1 = x_ref[...] + gt1_ref[...] * mix
    x1_ref[...] = x1
    ms = jnp.mean(x1 * x1, axis=-1, keepdims=True)
    h2 = x1 * lax.rsqrt(ms + EPS) * g2_ref[...]
    h2 = h2 * (1.0 + sc2_ref[...]) + sh2_ref[...]
    h2_ref[...] = h2
    hi, lo = _split(h2)
    wqh = wqh_ref[...]
    qp_ref[...] = _dot(hi, wqh) + (_dot(lo, wqh) + _dot(hi, wql_ref[...]))


def _outproj(x2d, ret, att, wo_bf, gt1, g2, sc2, sh2, wq_hi, wq_lo, tb):
    n = x2d.shape[0]
    groups, rows, _ = gt1.shape
    tiles_per_group = n // tb // groups
    mod_spec = pl.BlockSpec((None, rows, D_MODEL), lambda i: (i // tiles_per_group, 0, 0))
    tok = lambda w: pl.BlockSpec((tb, w), lambda i: (i, 0))
    full = lambda a: pl.BlockSpec(a.shape, lambda i: (0,) * a.ndim)
    return pl.pallas_call(
        _outproj_kernel,
        out_shape=[SDS((n, D_MODEL), F32)] * 3,
        grid=(n // tb,),
        in_specs=[tok(D_MODEL), tok(RET_W), tok(ATT_W), full(wo_bf), mod_spec, full(g2), mod_spec, mod_spec,
                  full(wq_hi), full(wq_lo)],
        out_specs=[tok(D_MODEL)] * 3,
        compiler_params=_params(("arbitrary",)),
        name="outproj",
    )(x2d, ret, att, wo_bf, gt1, g2, sc2, sh2, wq_hi, wq_lo)


def _top16(sc, ids=None):
    if ids is None:
        ids = lax.broadcasted_iota(I32, sc.shape, 0)
    ids = ids.astype(F32)
    vals, idxs = [], []
    for _ in range(PEER_TOPK):
        m = jnp.max(sc, axis=0, keepdims=True)
        idx = jnp.min(jnp.where(sc == m, ids, jnp.inf), axis=0, keepdims=True)
        vals.append(m)
        idxs.append(idx)
        sc = jnp.where(ids == idx, -jnp.inf, sc)
    return jnp.concatenate(vals, axis=0), jnp.concatenate(idxs, axis=0).astype(I32)


def _pair_candidates(s1, s2):
    k = PEER_TOPK
    half = k // 2
    pieces = [s1[0:1, :] + s2]
    pieces += [s1[a:a + 1, :] + s2[0:half, :] for a in range(1, half)]
    pieces.append(s1[half:, :] + s2[0:1, :])
    cand = jnp.concatenate(pieces, axis=0)
    r = lax.broadcasted_iota(I32, cand.shape, 0)
    mid = r - k
    a_mid = 1 + lax.shift_right_logical(mid, 3)
    b_mid = mid & (half - 1)
    tail_start = k + half * (half - 1)
    pos = jnp.where(r < k, r, jnp.where(r < tail_start, a_mid * k + b_mid, (half + r - tail_start) * k))
    reachable = jnp.logical_or(jnp.logical_or(r < k, r >= tail_start), (a_mid + 1) * (b_mid + 1) <= k)
    return jnp.where(reachable, cand, -jnp.inf), pos


TOPK_HEADS = 8


def _topk_one_head(q, skh_ref, skl_ref, hh):
    qh, ql = _split(q)

    def scores(c):
        kh = skh_ref[hh, c]
        return _dot_nt(kh, qh) + (_dot_nt(kh, ql) + _dot_nt(skl_ref[hh, c], qh))

    s1, i1 = _top16(scores(0))
    s2, i2 = _top16(scores(1))
    top_s, pos = _top16(*_pair_candidates(s1, s2))
    a_sel = lax.shift_right_logical(pos, 4)
    b_sel = pos & (PEER_TOPK - 1)
    io = lax.broadcasted_iota(I32, i1.shape, 0)
    rows = []
    for r in range(PEER_TOPK):
        ia = jnp.sum(jnp.where(io == a_sel[r:r + 1, :], i1, 0), axis=0, keepdims=True)
        ib = jnp.sum(jnp.where(io == b_sel[r:r + 1, :], i2, 0), axis=0, keepdims=True)
        rows.append(ia * N_KEYS + ib)
    ex = jnp.exp(top_s - top_s[0:1, :])
    return jnp.concatenate(rows, axis=0) * ROW_SUB, ex / jnp.sum(ex, axis=0, keepdims=True)


def _topk_kernel(q_ref, skh_ref, skl_ref, e_ref, g_ref, e_scr, g_scr):
    step = pl.program_id(1)
    for hh in range(TOPK_HEADS):
        e, g = _topk_one_head(q_ref[:, LANES * hh:LANES * (hh + 1)], skh_ref, skl_ref, hh)
        slot_rows = pl.ds(pl.multiple_of((step * TOPK_HEADS + hh) * PEER_TOPK, PEER_TOPK), PEER_TOPK)
        e_scr[slot_rows, :] = e
        g_scr[slot_rows, :] = g

    @pl.when(step == pl.num_programs(1) - 1)
    def _():
        e_ref[...] = e_scr[...].T
        g_ref[...] = g_scr[...].T


def _peer_topk(qp, sk_hi, sk_lo, tt, start, count):
    first = start // tt
    out_blk = pl.BlockSpec((tt, PEER_SLOTS), lambda i, h: (i, 0))
    sk_blk = pl.BlockSpec((TOPK_HEADS, 2, N_KEYS, LANES), lambda i, h: (h, 0, 0, 0))
    return pl.pallas_call(
        _topk_kernel,
        out_shape=[SDS((count, PEER_SLOTS), I32), SDS((count, PEER_SLOTS), F32)],
        grid=(count // tt, PEER_HEADS // TOPK_HEADS),
        in_specs=[pl.BlockSpec((tt, TOPK_HEADS * LANES), lambda i, h: (i + first, h)), sk_blk, sk_blk],
        out_specs=[out_blk, out_blk],
        scratch_shapes=[pltpu.VMEM((PEER_SLOTS, tt), I32), pltpu.VMEM((PEER_SLOTS, tt), F32)],
        compiler_params=_params(("arbitrary", "arbitrary")),
        name="peer_topk",
    )(qp, sk_hi, sk_lo)


def _gather_rows(idx_ref, tab_ref, t):
    rows = []
    for k in range(PEER_SLOTS):
        e = pl.multiple_of(idx_ref[t, k], ROW_SUB)
        rows.append(tab_ref[pl.ds(e, ROW_SUB), :])
    return pltpu.bitcast(jnp.concatenate(rows, axis=0), BF16)


def _diag_mask():
    sub = lax.broadcasted_iota(I32, (SUBLANES, D_MODEL), 0)
    col = lax.broadcasted_iota(I32, (SUBLANES, D_MODEL), 1)
    return (col & (SUBLANES - 1)) == sub


def _for_each_token(tb, one_token, unroll=TOK_UNROLL, per_step=None):
    def group(i, carry):
        if per_step is not None:
            per_step(i)
        for u in range(unroll):
            one_token(i * unroll + u)
        return carry

    lax.fori_loop(0, tb // unroll, group, 0)


def _peer_u_kernel(idx_ref, tab_ref, h2_ref, g_ref, sel_ref, *refs, with_topk):
    if with_topk:
        q_ref, skh_ref, skl_ref, w_ref, e_ref, gate_ref, xbuf, rall, e_scr, g_scr = refs
    else:
        w_ref, xbuf, rall = refs
    tb = h2_ref.shape[0]
    for j in range(SUBLANES):
        xbuf[pl.ds(j, tb, stride=SUBLANES), :] = h2_ref[:, LANES * j:LANES * (j + 1)]
    diag = _diag_mask()

    def tok(t):
        rows = _gather_rows(idx_ref, tab_ref, t)
        xt = xbuf[pl.ds(pl.multiple_of(t * SUBLANES, SUBLANES), SUBLANES), :].astype(BF16)
        r = _dot_nt(xt, rows)
        rall[pl.ds(t, 1), :] = jnp.sum(jnp.where(diag, r, 0.0), axis=0, keepdims=True)

    unroll = TOK_UNROLL_WITH_TOPK if with_topk else TOK_UNROLL
    halves = tb // LANES
    units_per_step = PEER_HEADS * halves * unroll // tb

    def topk_head(i):
        for j in range(units_per_step):
            unit = i * units_per_step + j
            h = lax.shift_right_logical(unit, int(math.log2(halves)))
            toks = pl.ds(pl.multiple_of((unit & (halves - 1)) * LANES, LANES), LANES)
            q = q_ref[toks, pl.ds(pl.multiple_of(h * LANES, LANES), LANES)]
            e, g = _topk_one_head(q, skh_ref, skl_ref, h)
            slot_rows = pl.ds(pl.multiple_of(h * PEER_TOPK, PEER_TOPK), PEER_TOPK)
            e_scr[slot_rows, toks] = e
            g_scr[slot_rows, toks] = g

    _for_each_token(tb, tok, unroll, topk_head if with_topk else None)
    hi, lo = _split(rall[...])
    sel = sel_ref[...]
    a = _dot(hi, sel) + _dot(lo, sel)
    gelu = 0.5 * a * (1.0 + lax.erf(a * (2.0 ** -0.5)))
    w_ref[...] = g_ref[...] * gelu
    if with_topk:
        e_ref[...] = e_scr[...].T
        gate_ref[...] = g_scr[...].T


def _peer_v_kernel(idx_ref, tab_ref, w_ref, x1_ref, gt2_ref, selt_ref, y_ref, wrep, ybuf):
    tb = x1_ref.shape[0]
    wrep[...] = _dot(w_ref[...].astype(BF16), selt_ref[...])
    diag = _diag_mask()

    def tok(t):
        rows = _gather_rows(idx_ref, tab_ref, t)
        wt = jnp.where(diag, wrep[pl.ds(t, 1), :], 0.0).astype(BF16)
        ybuf[pl.ds(pl.multiple_of(t * SUBLANES, SUBLANES), SUBLANES), :] = _dot(wt, rows)

    _for_each_token(tb, tok)
    for j in range(SUBLANES):
        cols = slice(LANES * j, LANES * (j + 1))
        y_ref[:, cols] = x1_ref[:, cols] + gt2_ref[:, cols] * ybuf[pl.ds(j, tb, stride=SUBLANES), :]


def _topk_units_fit(tb):
    units, steps = PEER_HEADS * (tb // LANES), tb // TOK_UNROLL_WITH_TOPK
    return tb % LANES == 0 and steps > 0 and units % steps == 0


def _peer_u(idx, tab, h2, g, sel, tb, start, topk_next=None):
    count = idx.shape[0]
    first = start // tb
    piece = lambda w: pl.BlockSpec((tb, w), lambda i: (i, 0))
    full = lambda a: pl.BlockSpec(a.shape, lambda i: (0,) * a.ndim)
    in_specs = [pl.BlockSpec((tb, PEER_SLOTS), lambda i: (i, 0), memory_space=pltpu.SMEM),
                pl.BlockSpec(memory_space=pltpu.VMEM),
                pl.BlockSpec((tb, D_MODEL), lambda i: (i + first, 0)), piece(PEER_SLOTS), full(sel)]
    args = [idx, tab, h2, g, sel]
    out_shape = [SDS((count, PEER_SLOTS), F32)]
    out_specs = [piece(PEER_SLOTS)]
    scratch = [pltpu.VMEM((tb * SUBLANES, LANES), F32), pltpu.VMEM((tb, D_MODEL), F32)]
    if topk_next is not None:
        qp, sk_hi, sk_lo, start_next = topk_next
        assert _topk_units_fit(tb)
        first_next = start_next // tb
        in_specs += [pl.BlockSpec((tb, D_MODEL), lambda i: (i + first_next, 0)), full(sk_hi), full(sk_lo)]
        args += [qp, sk_hi, sk_lo]
        out_shape += [SDS((count, PEER_SLOTS), I32), SDS((count, PEER_SLOTS), F32)]
        out_specs += [piece(PEER_SLOTS), piece(PEER_SLOTS)]
        scratch += [pltpu.VMEM((PEER_SLOTS, tb), I32), pltpu.VMEM((PEER_SLOTS, tb), F32)]
    return pl.pallas_call(
        functools.partial(_peer_u_kernel, with_topk=topk_next is not None),
        out_shape=out_shape,
        grid=(count // tb,),
        in_specs=in_specs,
        out_specs=out_specs,
        scratch_shapes=scratch,
        compiler_params=_params(("arbitrary",)),
        name="peer_u",
    )(*args)


def _peer_v(idx, tab, w, x1, gt2, selt, tb):
    n = x1.shape[0]
    groups, rows, _ = gt2.shape
    tiles_per_group = n // tb // groups
    tok = lambda wd: pl.BlockSpec((tb, wd), lambda i: (i, 0))
    return pl.pallas_call(
        _peer_v_kernel,
        out_shape=SDS((n, D_MODEL), F32),
        grid=(n // tb,),
        in_specs=[pl.BlockSpec((tb, PEER_SLOTS), lambda i: (i, 0), memory_space=pltpu.SMEM),
                  pl.BlockSpec(memory_space=pltpu.VMEM), tok(PEER_SLOTS), tok(D_MODEL),
                  pl.BlockSpec((None, rows, D_MODEL), lambda i: (i // tiles_per_group, 0, 0)),
                  pl.BlockSpec(selt.shape, lambda i: (0, 0))],
        out_specs=tok(D_MODEL),
        scratch_shapes=[pltpu.VMEM((tb, D_MODEL), F32), pltpu.VMEM((tb * SUBLANES, LANES), F32)],
        compiler_params=_params(("arbitrary",)),
        name="peer_v",
    )(idx, tab, w, x1, gt2, selt)


def _pack_table(tab):
    e = tab.shape[0]
    bits = lax.bitcast_convert_type(tab.astype(BF16), jnp.uint16).astype(jnp.uint32)
    bits = bits.reshape(e, ROW_SUB, 2, LANES)
    word = bits[:, :, 0, :] | (bits[:, :, 1, :] << 16)
    return lax.bitcast_convert_type(word, I32).reshape(e * ROW_SUB, LANES)


def _block_avg(width):
    i = np.arange(width) // HEAD_DIM
    return jnp.asarray((i[:, None] == i[None, :]).astype(np.float32) / HEAD_DIM, BF16)


def _slot_select():
    col = np.arange(D_MODEL) // SUBLANES
    return jnp.asarray((col[:, None] == np.arange(PEER_SLOTS)[None, :]).astype(np.float32), BF16)


def _ret_log_decay():
    return jnp.log1p(-jnp.exp2(-5.0 - jnp.arange(H_RET, dtype=F32)))


def _alibi_slopes():
    return jnp.exp2(-8.0 * (jnp.arange(H_ATT, dtype=F32) + 1.0) / H_ATT)


def _pad_sub_keys(sub_keys):
    z = jnp.zeros_like(sub_keys[:, 0])
    return jnp.stack([jnp.concatenate([sub_keys[:, 0], z], axis=-1),
                      jnp.concatenate([z, sub_keys[:, 1]], axis=-1)], axis=1)


def _group_w_out(w):
    w4 = w.reshape(H_RET, 2, HEAD_DIM, D_MODEL)
    return jnp.concatenate([w4[:, 0].reshape(RET_W, D_MODEL), w4[:, 1].reshape(ATT_W, D_MODEL)], axis=0)


def _split_w(w):
    hi = w.astype(BF16)
    return hi, (w - hi.astype(F32)).astype(BF16)


def _token_tile(n, pref):
    return pref if n % pref == 0 else n


PEER_PIECES = 8


def _peer(h2, qp, x1, gt2, consts, weights):
    n = h2.shape[0]
    tb = _token_tile(n, 256)
    sk = (weights["sk_hi"], weights["sk_lo"])
    pieces = PEER_PIECES if (_topk_units_fit(tb) and n % (PEER_PIECES * tb) == 0) else 1
    count = n // pieces
    idx, gate = _peer_topk(qp, *sk, tb, 0, count)
    idxs, ws = [], []
    for p in range(pieces):
        idxs.append(idx)
        if p + 1 < pieces:
            w, idx, gate = _peer_u(idx, weights["u_tab"], h2, gate, consts["sel"], tb, p * count,
                                   topk_next=(qp, *sk, (p + 1) * count))
        else:
            w, = _peer_u(idx, weights["u_tab"], h2, gate, consts["sel"], tb, p * count)
        ws.append(w)
    idx_all = jnp.concatenate(idxs, axis=0) if pieces > 1 else idxs[0]
    w_all = jnp.concatenate(ws, axis=0) if pieces > 1 else ws[0]
    return _peer_v(idx_all, weights["v_tab"], w_all, x1, gt2, consts["selt"], tb)


def _mix_and_peer(x2d, ret, att, mods, consts, weights):
    sh1, sc1, gt1, sh2, sc2, gt2 = mods
    n = x2d.shape[0]
    x1, h2, qp = _outproj(x2d, ret, att, weights["w_out"], gt1, weights["g_norm2"], sc2, sh2,
                          weights["wq_hi"], weights["wq_lo"], _token_tile(n, 512))
    return _peer(h2, qp, x1, gt2, consts, weights)


def _project(x2d, mods, consts, weights):
    sh1, sc1 = mods[0], mods[1]
    n = x2d.shape[0]
    return _inproj(x2d, sc1, sh1, weights["g_norm1"], weights["w_in"], weights["g_qn"], weights["g_kn"],
                   consts["avg_att"], _token_tile(n, 512))


def kernel(x_prompt, x_sample, c_prompt, c_sample, state_ret, cache_win_k, cache_win_v, w_ada, b_ada, g_norm1, w_in, g_qn, g_kn, g_ret, w_out, g_norm2, w_pq, sub_keys, peer_u, peer_v):
    b, s, d = x_prompt.shape
    db, ds_, _ = x_sample.shape
    depth = w_ada.shape[0]
    assert d == D_MODEL and ds_ == 1 and s % ATT_TILE == 0
    win = cache_win_k.shape[2]
    assert win == MAX_WINDOW

    consts = {
        "avg_att": _block_avg(ATT_W),
        "sel": _slot_select(),
        "selt": _slot_select().T,
        "lg": _ret_log_decay(),
        "slopes": _alibi_slopes(),
    }

    y_p = x_prompt.reshape(b * s, d)
    y_s = x_sample.reshape(db, d)
    outs = {k: [] for k in ("ret_p", "wk_p", "wv_p", "ret_s", "wk_s", "wv_s")}
    for l in range(depth):
        wq_hi, wq_lo = _split_w(w_pq[l])
        sk_hi, sk_lo = _split_w(_pad_sub_keys(sub_keys[l]))
        weights = {
            "g_norm1": g_norm1[l].reshape(1, d), "g_norm2": g_norm2[l].reshape(1, d),
            "w_in": w_in[l].astype(BF16), "w_out": _group_w_out(w_out[l]).astype(BF16),
            "g_qn": g_qn[l].reshape(1, ATT_W), "g_kn": g_kn[l].reshape(1, ATT_W),
            "wq_hi": wq_hi, "wq_lo": wq_lo, "sk_hi": sk_hi, "sk_lo": sk_lo,
            "u_tab": _pack_table(peer_u[l]), "v_tab": _pack_table(peer_v[l]),
        }
        gr = g_ret[l].reshape(1, RET_W)

        mod = _adaln(jnp.concatenate([c_prompt, c_sample], axis=0), w_ada[l], b_ada[l])
        mod6 = jnp.split(mod, 6, axis=-1)
        mods_p = [m[:b].reshape(b, 1, d) for m in mod6]
        mods_s = [m[b:].reshape(1, db, d) for m in mod6]

        rq, rk, rv, rg, aq, ak, av = _project(y_p, mods_p, consts, weights)
        seq = lambda a: a.reshape(b, s, a.shape[-1])
        ret, ret_state = _retention_prompt(consts["lg"], seq(rq), seq(rk), seq(rv), seq(rg), gr, 1024)
        att = _attention_prompt(consts["slopes"], seq(aq), seq(ak), seq(av))
        y_p = _mix_and_peer(y_p, ret.reshape(b * s, RET_W), att.reshape(b * s, ATT_W), mods_p, consts, weights)
        keep = min(MAX_WINDOW, s)
        outs["ret_p"].append(ret_state)
        outs["wk_p"].append(seq(ak)[:, s - keep:].reshape(b, keep, H_ATT, HEAD_DIM))
        outs["wv_p"].append(seq(av)[:, s - keep:].reshape(b, keep, H_ATT, HEAD_DIM))

        rq, rk, rv, rg, aq, ak, av = _project(y_s, mods_s, consts, weights)
        ret, new_state = _retention_sample(consts["lg"], rq, rk, rv, rg, gr, state_ret[l])
        att = _attention_sample(consts["slopes"], aq, ak, av, cache_win_k[l], cache_win_v[l])
        y_s = _mix_and_peer(y_s, ret.reshape(db, RET_W), att.reshape(db, ATT_W), mods_s, consts, weights)
        outs["ret_s"].append(new_state)
        outs["wk_s"].append(ak.reshape(db, 1, H_ATT, HEAD_DIM))
        outs["wv_s"].append(av.reshape(db, 1, H_ATT, HEAD_DIM))

    st = lambda k: jnp.stack(outs[k])
    return (y_p.reshape(b, s, d), y_s.reshape(db, 1, d), st("ret_p"), st("wk_p"), st("wv_p"),
            st("ret_s"), st("wk_s"), st("wv_s"))
```

```python
import functools
import math

import numpy as np
import jax
import jax.numpy as jnp
from jax import lax
from jax.experimental import pallas as pl
from jax.experimental.pallas import tpu as pltpu

F32 = jnp.float32
BF16 = jnp.bfloat16
I32 = jnp.int32

D_MODEL = 1024
HEAD_DIM = 64
RET_W = 512
ATT_W = 512
H_RET = 8
H_ATT = 8
N_PROJ = 4 * RET_W + 3 * ATT_W
RET_CHUNK = 128
DIL_BRANCHES = ((128, 1), (512, 4), (2048, 16))
NW = 128
MAX_WINDOW = 2048
N_KEYS = 128
PEER_HEADS = 8
PEER_TOPK = 16
PEER_SLOTS = PEER_HEADS * PEER_TOPK
EPS = 1e-6

LANES = 128
SUBLANES = 8
ROW_WORDS = D_MODEL // 2
ROW_SUB = ROW_WORDS // LANES
VMEM_LIMIT = 56 * 1024 * 1024
RET_UNROLL = 4
TOK_UNROLL = 32
TOK_UNROLL_WITH_TOPK = 16

SDS = jax.ShapeDtypeStruct


def _dot(a, b):
    return jnp.dot(a, b, preferred_element_type=F32)


def _dot_nt(a, b):
    return lax.dot_general(a, b, (((1,), (1,)), ((), ())), preferred_element_type=F32)


def _dot_tn(a, b):
    return lax.dot_general(a, b, (((0,), (0,)), ((), ())), preferred_element_type=F32)


def _split(x):
    hi = x.astype(BF16)
    lo = (x - hi.astype(F32)).astype(BF16)
    return hi, lo


def _silu(x):
    return x * jax.nn.sigmoid(x)


def _params(sem, vmem=VMEM_LIMIT):
    return pltpu.CompilerParams(dimension_semantics=sem, vmem_limit_bytes=vmem)


def _adaln_kernel(c_ref, w_ref, b_ref, o_ref):
    s = _silu(c_ref[...])
    o_ref[...] = jnp.dot(s, w_ref[...], preferred_element_type=F32,
                         precision=lax.Precision.HIGHEST) + b_ref[...]


def _adaln(c, w_ada, b_ada):
    rows = c.shape[0]
    tn = 512
    return pl.pallas_call(
        _adaln_kernel,
        out_shape=SDS((rows, 6 * D_MODEL), F32),
        grid=(6 * D_MODEL // tn,),
        in_specs=[pl.BlockSpec((rows, D_MODEL), lambda j: (0, 0)),
                  pl.BlockSpec((D_MODEL, tn), lambda j: (0, j)),
                  pl.BlockSpec((1, tn), lambda j: (0, j))],
        out_specs=pl.BlockSpec((rows, tn), lambda j: (0, j)),
        compiler_params=_params(("arbitrary",)),
        name="adaln",
    )(c, w_ada, b_ada.reshape(1, -1))


def _inproj_kernel(x_ref, sc_ref, sh_ref, g1_ref, w_ref, gq_ref, gk_ref, bd_ref,
                   rq_ref, rk_ref, rv_ref, rg_ref, aq_ref, ak_ref, av_ref):
    x = x_ref[...]
    ms = jnp.mean(x * x, axis=-1, keepdims=True)
    h = x * lax.rsqrt(ms + EPS) * g1_ref[...]
    h = h * (1.0 + sc_ref[...]) + sh_ref[...]
    p = _dot(h.astype(BF16), w_ref[...])
    scale = HEAD_DIM ** -0.5
    rq_ref[...] = p[:, 0:RET_W].astype(BF16)
    rk_ref[...] = (p[:, RET_W:2 * RET_W] * scale).astype(BF16)
    rv_ref[...] = p[:, 2 * RET_W:3 * RET_W].astype(BF16)
    rg_ref[...] = p[:, 3 * RET_W:4 * RET_W].astype(BF16)
    o = 4 * RET_W

    def head_norm(a, g):
        msq = _dot((a * a).astype(BF16), bd_ref[...])
        return a * lax.rsqrt(msq + EPS) * g

    aq_ref[...] = head_norm(p[:, o:o + ATT_W], gq_ref[...]) * scale
    ak_ref[...] = head_norm(p[:, o + ATT_W:o + 2 * ATT_W], gk_ref[...])
    av_ref[...] = p[:, o + 2 * ATT_W:o + 3 * ATT_W]


def _inproj(x2d, sc, sh, g1, w_in_bf, gq, gk, bd, tb):
    n = x2d.shape[0]
    groups, rows, _ = sc.shape
    tiles_per_group = n // tb // groups
    mod_spec = pl.BlockSpec((None, rows, D_MODEL), lambda i: (i // tiles_per_group, 0, 0))
    tok = lambda w: pl.BlockSpec((tb, w), lambda i: (i, 0))
    full = lambda a: pl.BlockSpec(a.shape, lambda i: (0,) * a.ndim)
    outs = [SDS((n, RET_W), BF16)] * 4 + [SDS((n, ATT_W), F32)] * 3
    return pl.pallas_call(
        _inproj_kernel,
        out_shape=outs,
        grid=(n // tb,),
        in_specs=[tok(D_MODEL), mod_spec, mod_spec, full(g1), full(w_in_bf), full(gq), full(gk), full(bd)],
        out_specs=[tok(RET_W)] * 4 + [tok(ATT_W)] * 3,
        compiler_params=_params(("arbitrary",)),
        name="inproj",
    )(x2d, sc, sh, g1, w_in_bf, gq, gk, bd)


def _ret_kernel(lg_ref, q_ref, k_ref, v_ref, rg_ref, gr_ref, o_ref, st_ref, s_scr, *, nchunk):
    hp = pl.program_id(1)
    t = pl.program_id(2)
    C = RET_CHUNK
    lane = lax.broadcasted_iota(I32, (1, LANES), 1)
    m0 = lane < HEAD_DIM
    lg0 = lg_ref[2 * hp]
    lg1 = lg_ref[2 * hp + 1]
    lgv = jnp.where(m0, lg0, lg1)

    @pl.when(t == 0)
    def _():
        s_scr[...] = jnp.zeros_like(s_scr)

    ii = lax.broadcasted_iota(I32, (C, C), 0)
    jj = lax.broadcasted_iota(I32, (C, C), 1)
    diff = ii - jj
    dpos = jnp.maximum(diff, 0).astype(F32)
    dec01 = jnp.concatenate([jnp.where(diff >= 0, jnp.exp(dpos * lg0), 0.0),
                             jnp.where(diff >= 0, jnp.exp(dpos * lg1), 0.0)], axis=0)

    def head_mean(x):
        s0 = jnp.sum(jnp.where(m0, x, 0.0), axis=-1, keepdims=True)
        s1 = jnp.sum(jnp.where(m0, 0.0, x), axis=-1, keepdims=True)
        return jnp.where(m0, s0, s1) * (1.0 / HEAD_DIM)
    ri = lax.broadcasted_iota(I32, (C, LANES), 0).astype(F32)
    w_start = jnp.exp((ri + 1.0) * lgv)
    w_end = jnp.exp((float(C) - 1.0 - ri) * lgv)
    chunk_decay = jnp.exp(float(C) * lgv)
    rr = lax.broadcasted_iota(I32, (LANES, LANES), 0)
    cc = lax.broadcasted_iota(I32, (LANES, LANES), 1)
    same_head = (rr < HEAD_DIM) == (cc < HEAD_DIM)
    gr = gr_ref[...]

    def chunk(c, state):
        rows = pl.ds(pl.multiple_of(c * C, C), C)
        q = q_ref[rows, :]
        k = k_ref[rows, :]
        v = v_ref[rows, :]
        qf = q.astype(F32)
        kf = k.astype(F32)
        q01 = jnp.concatenate([jnp.where(m0, qf, 0.0), jnp.where(m0, 0.0, qf)], axis=0).astype(BF16)
        p01 = (_dot_nt(q01, k) * dec01).astype(BF16)
        pv = _dot(p01, v)
        o_intra = jnp.where(m0, pv[0:C, :], pv[C:, :])
        o_inter = _dot((qf * w_start).astype(BF16), state.astype(BF16))
        o = o_intra + o_inter
        kv = _dot_tn((kf * w_end).astype(BF16), v)
        mu = head_mean(o)
        xc = o - mu
        var = head_mean(xc * xc)
        y = xc * lax.rsqrt(var + EPS) * gr
        y = y * _silu(rg_ref[rows, :].astype(F32))
        o_ref[rows, :] = y.astype(BF16)
        return chunk_decay * state + jnp.where(same_head, kv, 0.0)

    def chunk_group(i, state):
        for u in range(RET_UNROLL):
            state = chunk(i * RET_UNROLL + u, state)
        return state

    s_scr[...] = lax.fori_loop(0, nchunk // RET_UNROLL, chunk_group, s_scr[...])

    @pl.when(t == pl.num_programs(2) - 1)
    def _():
        state = s_scr[...]
        st_ref[0] = state[0:HEAD_DIM, 0:HEAD_DIM]
        st_ref[1] = state[HEAD_DIM:, HEAD_DIM:]


def _retention_prompt(lg, rq, rk, rv, rg, g_ret, cb):
    b, s, _ = rq.shape
    nchunk = cb // RET_CHUNK
    blk = pl.BlockSpec((None, cb, LANES), lambda bi, hp, t, lg_: (bi, t, hp))
    grid_spec = pltpu.PrefetchScalarGridSpec(
        num_scalar_prefetch=1,
        grid=(b, H_RET // 2, s // cb),
        in_specs=[blk, blk, blk, blk,
                  pl.BlockSpec((1, LANES), lambda bi, hp, t, lg_: (0, hp))],
        out_specs=[blk, pl.BlockSpec((None, 2, HEAD_DIM, HEAD_DIM), lambda bi, hp, t, lg_: (bi, hp, 0, 0))],
        scratch_shapes=[pltpu.VMEM((LANES, LANES), F32)],
    )
    return pl.pallas_call(
        functools.partial(_ret_kernel, nchunk=nchunk),
        out_shape=[SDS((b, s, RET_W), BF16), SDS((b, H_RET, HEAD_DIM, HEAD_DIM), F32)],
        grid_spec=grid_spec,
        compiler_params=_params(("arbitrary", "arbitrary", "arbitrary")),
        name="retention_prompt",
    )(lg, rq, rk, rv, rg, g_ret)


ATT_TILE = MAX_WINDOW


ATT_UNROLL = 8


def _att_kernel(sl_ref, q_ref, k_ref, v_ref, o_ref, kk, vv, *stats):
    hp = pl.program_id(1)
    t = pl.program_id(2)
    TS = ATT_TILE
    nbr = len(DIL_BRANCHES)
    m_br, l_br, acc_br = stats[:nbr], stats[nbr:2 * nbr], stats[2 * nbr:]
    cur = pl.multiple_of((t & 1) * TS, TS)

    @pl.when(t == 0)
    def _():
        kk[TS:2 * TS, :] = jnp.zeros((TS, LANES), F32)
        vv[TS:2 * TS, :] = jnp.zeros((TS, LANES), F32)

    kk[pl.ds(cur, TS), :] = k_ref[...]
    vv[pl.ds(cur, TS), :] = v_ref[...]

    lane = lax.broadcasted_iota(I32, (1, LANES), 1)
    m0 = lane < HEAD_DIM
    head_lanes = (m0, jnp.logical_not(m0))
    ii = lax.broadcasted_iota(I32, (NW, 2 * NW), 0)
    jj = lax.broadcasted_iota(I32, (NW, 2 * NW), 1)
    off = ii + NW - jj
    band = (off >= 0) & (off <= NW)
    off_f = off.astype(F32)
    ones_h = [jnp.broadcast_to(jnp.where(hl, 1.0, 0.0).astype(BF16), (2 * NW, LANES)) for hl in head_lanes]

    for br, (_, dil) in enumerate(DIL_BRANCHES):
        nblk = TS // (NW * dil)
        shift = int(math.log2(nblk))
        bias = [jnp.where(band, -(sl_ref[2 * hp + hh] * float(dil)) * off_f, -jnp.inf) for hh in range(2)]

        def one_block(bs, dil=dil, nblk=nblk, shift=shift, br=br, bias=bias):
            r = lax.shift_right_logical(bs, shift)
            blk = bs & (nblk - 1)
            qs = r + dil * NW * blk
            q_rows = pl.ds(qs, NW, stride=dil)
            own = pl.ds(cur + qs, NW, stride=dil)
            prev = pl.ds((cur + qs - dil * NW) & (2 * TS - 1), NW, stride=dil)
            qb = q_ref[q_rows, :]
            kb = jnp.concatenate([kk[prev, :], kk[own, :]], axis=0).astype(BF16)
            vb = jnp.concatenate([vv[prev, :], vv[own, :]], axis=0)
            first_key = jnp.where(jnp.logical_or(t > 0, blk > 0), 0, NW)
            started = jj >= first_key
            res = None
            mx = []
            for hh in range(2):
                qh = jnp.where(head_lanes[hh], qb, 0.0).astype(BF16)
                s = jnp.where(started, _dot_nt(qh, kb) + bias[hh], -jnp.inf)
                mh = jnp.max(s, axis=-1, keepdims=True)
                p = jnp.exp(s - mh).astype(BF16)
                vh = jnp.concatenate([jnp.where(head_lanes[hh], vb, 0.0).astype(BF16), ones_h[hh]], axis=1)
                part = _dot(p, vh)
                res = part if res is None else res + part
                mx.append(mh)
            m_br[br][q_rows, :] = jnp.where(m0, mx[0], mx[1])
            acc_br[br][q_rows, :] = res[:, :LANES]
            l_br[br][q_rows, :] = res[:, LANES:]

        def body(i, carry, one_block=one_block):
            for u in range(ATT_UNROLL):
                one_block(i * ATT_UNROLL + u)
            return carry

        lax.fori_loop(0, TS // NW // ATT_UNROLL, body, 0)

    rows_per_step = 256

    def merge(i, carry):
        rows = pl.ds(pl.multiple_of(i * rows_per_step, rows_per_step), rows_per_step)
        ms = [m[rows, :] for m in m_br]
        m_all = functools.reduce(jnp.maximum, ms)
        ws = [jnp.exp(m - m_all) for m in ms]
        num = sum(w * a[rows, :] for w, a in zip(ws, acc_br))
        den = sum(w * l[rows, :] for w, l in zip(ws, l_br))
        o_ref[rows, :] = (num / den).astype(BF16)
        return carry

    lax.fori_loop(0, TS // rows_per_step, merge, 0)


def _attention_prompt(slopes, aq, ak, av):
    b, s, _ = aq.shape
    ts = ATT_TILE
    blk = pl.BlockSpec((None, ts, LANES), lambda bi, hp, t, sl: (bi, t, hp))
    grid_spec = pltpu.PrefetchScalarGridSpec(
        num_scalar_prefetch=1,
        grid=(b, H_ATT // 2, s // ts),
        in_specs=[blk, blk, blk],
        out_specs=blk,
        scratch_shapes=[pltpu.VMEM((2 * ts, LANES), F32)] * 2
        + [pltpu.VMEM((ts, LANES), F32)] * (3 * len(DIL_BRANCHES)),
    )
    return pl.pallas_call(
        _att_kernel,
        out_shape=SDS((b, s, ATT_W), BF16),
        grid_spec=grid_spec,
        compiler_params=_params(("arbitrary", "arbitrary", "arbitrary")),
        name="attention_prompt",
    )(slopes, aq, ak, av)


def _sret_kernel(lg_ref, q_ref, k_ref, v_ref, rg_ref, gr_ref, st_ref, o_ref, sn_ref):
    q = q_ref[...].astype(F32)
    k = k_ref[...].astype(F32)
    v = v_ref[...].astype(F32)
    rg = rg_ref[...].astype(F32)
    gr = gr_ref[...]
    r_i = lax.broadcasted_iota(I32, (HEAD_DIM, HEAD_DIM), 0)
    c_i = lax.broadcasted_iota(I32, (HEAD_DIM, HEAD_DIM), 1)
    eye = r_i == c_i
    outs = []
    for h in range(H_RET):
        sl = slice(h * HEAD_DIM, (h + 1) * HEAD_DIM)
        qh, kh, vh = q[:, sl], k[:, sl], v[:, sl]
        gamma = jnp.exp(jnp.full((1, HEAD_DIM), lg_ref[h], F32))
        q_col = jnp.sum(jnp.where(eye, qh, 0.0), axis=1, keepdims=True)
        k_col = jnp.sum(jnp.where(eye, kh, 0.0), axis=1, keepdims=True)
        state = st_ref[h]
        o_inter = gamma * jnp.sum(q_col * state, axis=0, keepdims=True)
        o_intra = jnp.sum(qh * kh, axis=-1, keepdims=True) * vh
        sn_ref[h] = gamma * state + k_col * vh
        o = o_intra + o_inter
        mu = jnp.mean(o, axis=-1, keepdims=True)
        xc = o - mu
        var = jnp.mean(xc * xc, axis=-1, keepdims=True)
        outs.append(xc * lax.rsqrt(var + EPS) * gr[:, sl] * _silu(rg[:, sl]))
    o_ref[...] = jnp.concatenate(outs, axis=-1).astype(BF16)


def _retention_sample(lg, rq, rk, rv, rg, g_ret, state):
    db = rq.shape[0]
    row = pl.BlockSpec((None, 1, RET_W), lambda i: (i, 0, 0))
    st = pl.BlockSpec((None, H_RET, HEAD_DIM, HEAD_DIM), lambda i: (i, 0, 0, 0))
    r3 = lambda a: a.reshape(db, 1, RET_W)
    return pl.pallas_call(
        _sret_kernel,
        out_shape=[SDS((db, 1, RET_W), BF16), SDS(state.shape, F32)],
        grid=(db,),
        in_specs=[pl.BlockSpec(memory_space=pltpu.SMEM), row, row, row, row,
                  pl.BlockSpec((1, RET_W), lambda i: (0, 0)), st],
        out_specs=[row, st],
        compiler_params=_params(("arbitrary",)),
        name="retention_sample",
    )(lg, r3(rq), r3(rk), r3(rv), r3(rg), g_ret, state)


def _satt_kernel(sl_ref, qt_ref, knt_ref, vnt_ref, kt_ref, vt_ref, o_ref):
    win = kt_ref.shape[-1]
    dist = win - lax.broadcasted_iota(I32, (1, win), 1)
    count = jnp.zeros((1, win), F32)
    for window, dil in DIL_BRANCHES:
        reads = ((dist & (dil - 1)) == 0) & (dist <= window)
        count = count + jnp.where(reads, 1.0, 0.0)
    dist_f = dist.astype(F32)
    nbr = float(len(DIL_BRANCHES))
    for h in range(H_ATT):
        q = qt_ref[:, h:h + 1]
        s = jnp.sum(kt_ref[h] * q, axis=0, keepdims=True) - sl_ref[h] * dist_f
        s = jnp.where(count > 0.0, s, -jnp.inf)
        s_new = jnp.sum(q * knt_ref[:, h:h + 1], axis=0, keepdims=True)
        m = jnp.maximum(jnp.max(s, axis=-1, keepdims=True), s_new)
        p = count * jnp.exp(s - m)
        p_new = nbr * jnp.exp(s_new - m)
        den = jnp.sum(p, axis=-1, keepdims=True) + p_new
        num = jnp.sum(vt_ref[h] * p, axis=-1, keepdims=True) + p_new * vnt_ref[:, h:h + 1]
        o_ref[:, h:h + 1] = num / den


def _attention_sample(slopes, aq, ak, av, win_k, win_v):
    db = aq.shape[0]
    win = win_k.shape[1]
    col = pl.BlockSpec((None, HEAD_DIM, H_ATT), lambda i: (i, 0, 0))
    cache = pl.BlockSpec((None, H_ATT, HEAD_DIM, win), lambda i: (i, 0, 0, 0))
    cols = lambda a: jnp.transpose(a.reshape(db, H_ATT, HEAD_DIM), (0, 2, 1))
    keys_minor = lambda c: jnp.transpose(c, (0, 2, 3, 1))
    out = pl.pallas_call(
        _satt_kernel,
        out_shape=SDS((db, HEAD_DIM, H_ATT), F32),
        grid=(db,),
        in_specs=[pl.BlockSpec(memory_space=pltpu.SMEM), col, col, col, cache, cache],
        out_specs=col,
        compiler_params=_params(("arbitrary",)),
        name="attention_sample",
    )(slopes, cols(aq), cols(ak), cols(av), keys_minor(win_k), keys_minor(win_v))
    return jnp.transpose(out, (0, 2, 1)).reshape(db, ATT_W).astype(BF16)


def _outproj_kernel(x_ref, ret_ref, att_ref, wo_ref, gt1_ref, g2_ref, sc2_ref, sh2_ref, wqh_ref, wql_ref,
                    x1_ref, h2_ref, qp_ref):
    mix = _dot(ret_ref[...], wo_ref[0:RET_W, :]) + _dot(att_ref[...], wo_ref[RET_W:, :])
    x1 = x_ref[...] + gt1_ref[...] * mix
    x1_ref[...] = x1
    ms = jnp.mean(x1 * x1, axis=-1, keepdims=True)
    h2 = x1 * lax.rsqrt(ms + EPS) * g2_ref[...]
    h2 = h2 * (1.0 + sc2_ref[...]) + sh2_ref[...]
    h2_ref[...] = h2
    hi, lo = _split(h2)
    wqh = wqh_ref[...]
    qp_ref[...] = _dot(hi, wqh) + (_dot(lo, wqh) + _dot(hi, wql_ref[...]))


def _outproj(x2d, ret, att, wo_bf, gt1, g2, sc2, sh2, wq_hi, wq_lo, tb):
    n = x2d.shape[0]
    groups, rows, _ = gt1.shape
    tiles_per_group = n // tb // groups
    mod_spec = pl.BlockSpec((None, rows, D_MODEL), lambda i: (i // tiles_per_group, 0, 0))
    tok = lambda w: pl.BlockSpec((tb, w), lambda i: (i, 0))
    full = lambda a: pl.BlockSpec(a.shape, lambda i: (0,) * a.ndim)
    return pl.pallas_call(
        _outproj_kernel,
        out_shape=[SDS((n, D_MODEL), F32)] * 3,
        grid=(n // tb,),
        in_specs=[tok(D_MODEL), tok(RET_W), tok(ATT_W), full(wo_bf), mod_spec, full(g2), mod_spec, mod_spec,
                  full(wq_hi), full(wq_lo)],
        out_specs=[tok(D_MODEL)] * 3,
        compiler_params=_params(("arbitrary",)),
        name="outproj",
    )(x2d, ret, att, wo_bf, gt1, g2, sc2, sh2, wq_hi, wq_lo)


def _top16(sc, ids=None):
    if ids is None:
        ids = lax.broadcasted_iota(I32, sc.shape, 0)
    ids = ids.astype(F32)
    vals, idxs = [], []
    for _ in range(PEER_TOPK):
        m = jnp.max(sc, axis=0, keepdims=True)
        idx = jnp.min(jnp.where(sc == m, ids, jnp.inf), axis=0, keepdims=True)
        vals.append(m)
        idxs.append(idx)
        sc = jnp.where(ids == idx, -jnp.inf, sc)
    return jnp.concatenate(vals, axis=0), jnp.concatenate(idxs, axis=0).astype(I32)


def _pair_candidates(s1, s2):
    k = PEER_TOPK
    half = k // 2
    pieces = [s1[0:1, :] + s2]
    pieces += [s1[a:a + 1, :] + s2[0:half, :] for a in range(1, half)]
    pieces.append(s1[half:, :] + s2[0:1, :])
    cand = jnp.concatenate(pieces, axis=0)
    r = lax.broadcasted_iota(I32, cand.shape, 0)
    mid = r - k
    a_mid = 1 + lax.shift_right_logical(mid, 3)
    b_mid = mid & (half - 1)
    tail_start = k + half * (half - 1)
    pos = jnp.where(r < k, r, jnp.where(r < tail_start, a_mid * k + b_mid, (half + r - tail_start) * k))
    reachable = jnp.logical_or(jnp.logical_or(r < k, r >= tail_start), (a_mid + 1) * (b_mid + 1) <= k)
    return jnp.where(reachable, cand, -jnp.inf), pos


TOPK_HEADS = 8


def _topk_one_head(q, skh_ref, skl_ref, hh):
    qh, ql = _split(q)

    def scores(c):
        kh = skh_ref[hh, c]
        return _dot_nt(kh, qh) + (_dot_nt(kh, ql) + _dot_nt(skl_ref[hh, c], qh))

    s1, i1 = _top16(scores(0))
    s2, i2 = _top16(scores(1))
    top_s, pos = _top16(*_pair_candidates(s1, s2))
    a_sel = lax.shift_right_logical(pos, 4)
    b_sel = pos & (PEER_TOPK - 1)
    io = lax.broadcasted_iota(I32, i1.shape, 0)
    rows = []
    for r in range(PEER_TOPK):
        ia = jnp.sum(jnp.where(io == a_sel[r:r + 1, :], i1, 0), axis=0, keepdims=True)
        ib = jnp.sum(jnp.where(io == b_sel[r:r + 1, :], i2, 0), axis=0, keepdims=True)
        rows.append(ia * N_KEYS + ib)
    ex = jnp.exp(top_s - top_s[0:1, :])
    return jnp.concatenate(rows, axis=0) * ROW_SUB, ex / jnp.sum(ex, axis=0, keepdims=True)


def _topk_kernel(q_ref, skh_ref, skl_ref, e_ref, g_ref, e_scr, g_scr):
    step = pl.program_id(1)
    for hh in range(TOPK_HEADS):
        e, g = _topk_one_head(q_ref[:, LANES * hh:LANES * (hh + 1)], skh_ref, skl_ref, hh)
        slot_rows = pl.ds(pl.multiple_of((step * TOPK_HEADS + hh) * PEER_TOPK, PEER_TOPK), PEER_TOPK)
        e_scr[slot_rows, :] = e
        g_scr[slot_rows, :] = g

    @pl.when(step == pl.num_programs(1) - 1)
    def _():
        e_ref[...] = e_scr[...].T
        g_ref[...] = g_scr[...].T


def _peer_topk(qp, sk_hi, sk_lo, tt, start, count):
    first = start // tt
    out_blk = pl.BlockSpec((tt, PEER_SLOTS), lambda i, h: (i, 0))
    sk_blk = pl.BlockSpec((TOPK_HEADS, 2, N_KEYS, LANES), lambda i, h: (h, 0, 0, 0))
    return pl.pallas_call(
        _topk_kernel,
        out_shape=[SDS((count, PEER_SLOTS), I32), SDS((count, PEER_SLOTS), F32)],
        grid=(count // tt, PEER_HEADS // TOPK_HEADS),
        in_specs=[pl.BlockSpec((tt, TOPK_HEADS * LANES), lambda i, h: (i + first, h)), sk_blk, sk_blk],
        out_specs=[out_blk, out_blk],
        scratch_shapes=[pltpu.VMEM((PEER_SLOTS, tt), I32), pltpu.VMEM((PEER_SLOTS, tt), F32)],
        compiler_params=_params(("arbitrary", "arbitrary")),
        name="peer_topk",
    )(qp, sk_hi, sk_lo)


def _gather_rows(idx_ref, tab_ref, t):
    rows = []
    for k in range(PEER_SLOTS):
        e = pl.multiple_of(idx_ref[t, k], ROW_SUB)
        rows.append(tab_ref[pl.ds(e, ROW_SUB), :])
    return pltpu.bitcast(jnp.concatenate(rows, axis=0), BF16)


def _diag_mask():
    sub = lax.broadcasted_iota(I32, (SUBLANES, D_MODEL), 0)
    col = lax.broadcasted_iota(I32, (SUBLANES, D_MODEL), 1)
    return (col & (SUBLANES - 1)) == sub


def _for_each_token(tb, one_token, unroll=TOK_UNROLL, per_step=None):
    def group(i, carry):
        if per_step is not None:
            per_step(i)
        for u in range(unroll):
            one_token(i * unroll + u)
        return carry

    lax.fori_loop(0, tb // unroll, group, 0)


def _peer_u_kernel(idx_ref, tab_ref, h2_ref, g_ref, sel_ref, *refs, with_topk):
    if with_topk:
        q_ref, skh_ref, skl_ref, w_ref, e_ref, gate_ref, xbuf, rall, e_scr, g_scr = refs
    else:
        w_ref, xbuf, rall = refs
    tb = h2_ref.shape[0]
    for j in range(SUBLANES):
        xbuf[pl.ds(j, tb, stride=SUBLANES), :] = h2_ref[:, LANES * j:LANES * (j + 1)]
    diag = _diag_mask()

    def tok(t):
        rows = _gather_rows(idx_ref, tab_ref, t)
        xt = xbuf[pl.ds(pl.multiple_of(t * SUBLANES, SUBLANES), SUBLANES), :].astype(BF16)
        r = _dot_nt(xt, rows)
        rall[pl.ds(t, 1), :] = jnp.sum(jnp.where(diag, r, 0.0), axis=0, keepdims=True)

    unroll = TOK_UNROLL_WITH_TOPK if with_topk else TOK_UNROLL
    halves = tb // LANES
    units_per_step = PEER_HEADS * halves * unroll // tb

    def topk_head(i):
        for j in range(units_per_step):
            unit = i * units_per_step + j
            h = lax.shift_right_logical(unit, int(math.log2(halves)))
            toks = pl.ds(pl.multiple_of((unit & (halves - 1)) * LANES, LANES), LANES)
            q = q_ref[toks, pl.ds(pl.multiple_of(h * LANES, LANES), LANES)]
            e, g = _topk_one_head(q, skh_ref, skl_ref, h)
            slot_rows = pl.ds(pl.multiple_of(h * PEER_TOPK, PEER_TOPK), PEER_TOPK)
            e_scr[slot_rows, toks] = e
            g_scr[slot_rows, toks] = g

    _for_each_token(tb, tok, unroll, topk_head if with_topk else None)
    hi, lo = _split(rall[...])
    sel = sel_ref[...]
    a = _dot(hi, sel) + _dot(lo, sel)
    gelu = 0.5 * a * (1.0 + lax.erf(a * (2.0 ** -0.5)))
    w_ref[...] = g_ref[...] * gelu
    if with_topk:
        e_ref[...] = e_scr[...].T
        gate_ref[...] = g_scr[...].T


def _peer_v_kernel(idx_ref, tab_ref, w_ref, x1_ref, gt2_ref, selt_ref, y_ref, wrep, ybuf):
    tb = x1_ref.shape[0]
    wrep[...] = _dot(w_ref[...].astype(BF16), selt_ref[...])
    diag = _diag_mask()

    def tok(t):
        rows = _gather_rows(idx_ref, tab_ref, t)
        wt = jnp.where(diag, wrep[pl.ds(t, 1), :], 0.0).astype(BF16)
        ybuf[pl.ds(pl.multiple_of(t * SUBLANES, SUBLANES), SUBLANES), :] = _dot(wt, rows)

    _for_each_token(tb, tok)
    for j in range(SUBLANES):
        cols = slice(LANES * j, LANES * (j + 1))
        y_ref[:, cols] = x1_ref[:, cols] + gt2_ref[:, cols] * ybuf[pl.ds(j, tb, stride=SUBLANES), :]


def _topk_units_fit(tb):
    units, steps = PEER_HEADS * (tb // LANES), tb // TOK_UNROLL_WITH_TOPK
    return tb % LANES == 0 and steps > 0 and units % steps == 0


def _peer_u(idx, tab, h2, g, sel, tb, start, topk_next=None):
    count = idx.shape[0]
    first = start // tb
    piece = lambda w: pl.BlockSpec((tb, w), lambda i: (i, 0))
    full = lambda a: pl.BlockSpec(a.shape, lambda i: (0,) * a.ndim)
    in_specs = [pl.BlockSpec((tb, PEER_SLOTS), lambda i: (i, 0), memory_space=pltpu.SMEM),
                pl.BlockSpec(memory_space=pltpu.VMEM),
                pl.BlockSpec((tb, D_MODEL), lambda i: (i + first, 0)), piece(PEER_SLOTS), full(sel)]
    args = [idx, tab, h2, g, sel]
    out_shape = [SDS((count, PEER_SLOTS), F32)]
    out_specs = [piece(PEER_SLOTS)]
    scratch = [pltpu.VMEM((tb * SUBLANES, LANES), F32), pltpu.VMEM((tb, D_MODEL), F32)]
    if topk_next is not None:
        qp, sk_hi, sk_lo, start_next = topk_next
        assert _topk_units_fit(tb)
        first_next = start_next // tb
        in_specs += [pl.BlockSpec((tb, D_MODEL), lambda i: (i + first_next, 0)), full(sk_hi), full(sk_lo)]
        args += [qp, sk_hi, sk_lo]
        out_shape += [SDS((count, PEER_SLOTS), I32), SDS((count, PEER_SLOTS), F32)]
        out_specs += [piece(PEER_SLOTS), piece(PEER_SLOTS)]
        scratch += [pltpu.VMEM((PEER_SLOTS, tb), I32), pltpu.VMEM((PEER_SLOTS, tb), F32)]
    return pl.pallas_call(
        functools.partial(_peer_u_kernel, with_topk=topk_next is not None),
        out_shape=out_shape,
        grid=(count // tb,),
        in_specs=in_specs,
        out_specs=out_specs,
        scratch_shapes=scratch,
        compiler_params=_params(("arbitrary",)),
        name="peer_u",
    )(*args)


def _peer_v(idx, tab, w, x1, gt2, selt, tb):
    n = x1.shape[0]
    groups, rows, _ = gt2.shape
    tiles_per_group = n // tb // groups
    tok = lambda wd: pl.BlockSpec((tb, wd), lambda i: (i, 0))
    return pl.pallas_call(
        _peer_v_kernel,
        out_shape=SDS((n, D_MODEL), F32),
        grid=(n // tb,),
        in_specs=[pl.BlockSpec((tb, PEER_SLOTS), lambda i: (i, 0), memory_space=pltpu.SMEM),
                  pl.BlockSpec(memory_space=pltpu.VMEM), tok(PEER_SLOTS), tok(D_MODEL),
                  pl.BlockSpec((None, rows, D_MODEL), lambda i: (i // tiles_per_group, 0, 0)),
                  pl.BlockSpec(selt.shape, lambda i: (0, 0))],
        out_specs=tok(D_MODEL),
        scratch_shapes=[pltpu.VMEM((tb, D_MODEL), F32), pltpu.VMEM((tb * SUBLANES, LANES), F32)],
        compiler_params=_params(("arbitrary",)),
        name="peer_v",
    )(idx, tab, w, x1, gt2, selt)


def _pack_table(tab):
    e = tab.shape[0]
    bits = lax.bitcast_convert_type(tab.astype(BF16), jnp.uint16).astype(jnp.uint32)
    bits = bits.reshape(e, ROW_SUB, 2, LANES)
    word = bits[:, :, 0, :] | (bits[:, :, 1, :] << 16)
    return lax.bitcast_convert_type(word, I32).reshape(e * ROW_SUB, LANES)


def _block_avg(width):
    i = np.arange(width) // HEAD_DIM
    return jnp.asarray((i[:, None] == i[None, :]).astype(np.float32) / HEAD_DIM, BF16)


def _slot_select():
    col = np.arange(D_MODEL) // SUBLANES
    return jnp.asarray((col[:, None] == np.arange(PEER_SLOTS)[None, :]).astype(np.float32), BF16)


def _ret_log_decay():
    return jnp.log1p(-jnp.exp2(-5.0 - jnp.arange(H_RET, dtype=F32)))


def _alibi_slopes():
    return jnp.exp2(-8.0 * (jnp.arange(H_ATT, dtype=F32) + 1.0) / H_ATT)


def _pad_sub_keys(sub_keys):
    z = jnp.zeros_like(sub_keys[:, 0])
    return jnp.stack([jnp.concatenate([sub_keys[:, 0], z], axis=-1),
                      jnp.concatenate([z, sub_keys[:, 1]], axis=-1)], axis=1)


def _group_w_out(w):
    w4 = w.reshape(H_RET, 2, HEAD_DIM, D_MODEL)
    return jnp.concatenate([w4[:, 0].reshape(RET_W, D_MODEL), w4[:, 1].reshape(ATT_W, D_MODEL)], axis=0)


def _split_w(w):
    hi = w.astype(BF16)
    return hi, (w - hi.astype(F32)).astype(BF16)


def _token_tile(n, pref):
    return pref if n % pref == 0 else n


PEER_PIECES = 16


def _peer(h2, qp, x1, gt2, consts, weights):
    n = h2.shape[0]
    tb = _token_tile(n, 256)
    sk = (weights["sk_hi"], weights["sk_lo"])
    pieces = PEER_PIECES if (_topk_units_fit(tb) and n % (PEER_PIECES * tb) == 0) else 1
    count = n // pieces
    idx, gate = _peer_topk(qp, *sk, tb, 0, count)
    idxs, ws = [], []
    for p in range(pieces):
        idxs.append(idx)
        if p + 1 < pieces:
            w, idx, gate = _peer_u(idx, weights["u_tab"], h2, gate, consts["sel"], tb, p * count,
                                   topk_next=(qp, *sk, (p + 1) * count))
        else:
            w, = _peer_u(idx, weights["u_tab"], h2, gate, consts["sel"], tb, p * count)
        ws.append(w)
    idx_all = jnp.concatenate(idxs, axis=0) if pieces > 1 else idxs[0]
    w_all = jnp.concatenate(ws, axis=0) if pieces > 1 else ws[0]
    return _peer_v(idx_all, weights["v_tab"], w_all, x1, gt2, consts["selt"], tb)


def _mix_and_peer(x2d, ret, att, mods, consts, weights):
    sh1, sc1, gt1, sh2, sc2, gt2 = mods
    n = x2d.shape[0]
    x1, h2, qp = _outproj(x2d, ret, att, weights["w_out"], gt1, weights["g_norm2"], sc2, sh2,
                          weights["wq_hi"], weights["wq_lo"], _token_tile(n, 512))
    return _peer(h2, qp, x1, gt2, consts, weights)


def _project(x2d, mods, consts, weights):
    sh1, sc1 = mods[0], mods[1]
    n = x2d.shape[0]
    return _inproj(x2d, sc1, sh1, weights["g_norm1"], weights["w_in"], weights["g_qn"], weights["g_kn"],
                   consts["avg_att"], _token_tile(n, 512))


def kernel(x_prompt, x_sample, c_prompt, c_sample, state_ret, cache_win_k, cache_win_v, w_ada, b_ada, g_norm1, w_in, g_qn, g_kn, g_ret, w_out, g_norm2, w_pq, sub_keys, peer_u, peer_v):
    b, s, d = x_prompt.shape
    db, ds_, _ = x_sample.shape
    depth = w_ada.shape[0]
    assert d == D_MODEL and ds_ == 1 and s % ATT_TILE == 0
    win = cache_win_k.shape[2]
    assert win == MAX_WINDOW

    consts = {
        "avg_att": _block_avg(ATT_W),
        "sel": _slot_select(),
        "selt": _slot_select().T,
        "lg": _ret_log_decay(),
        "slopes": _alibi_slopes(),
    }

    y_p = x_prompt.reshape(b * s, d)
    y_s = x_sample.reshape(db, d)
    outs = {k: [] for k in ("ret_p", "wk_p", "wv_p", "ret_s", "wk_s", "wv_s")}
    for l in range(depth):
        wq_hi, wq_lo = _split_w(w_pq[l])
        sk_hi, sk_lo = _split_w(_pad_sub_keys(sub_keys[l]))
        weights = {
            "g_norm1": g_norm1[l].reshape(1, d), "g_norm2": g_norm2[l].reshape(1, d),
            "w_in": w_in[l].astype(BF16), "w_out": _group_w_out(w_out[l]).astype(BF16),
            "g_qn": g_qn[l].reshape(1, ATT_W), "g_kn": g_kn[l].reshape(1, ATT_W),
            "wq_hi": wq_hi, "wq_lo": wq_lo, "sk_hi": sk_hi, "sk_lo": sk_lo,
            "u_tab": _pack_table(peer_u[l]), "v_tab": _pack_table(peer_v[l]),
        }
        gr = g_ret[l].reshape(1, RET_W)

        mod = _adaln(jnp.concatenate([c_prompt, c_sample], axis=0), w_ada[l], b_ada[l])
        mod6 = jnp.split(mod, 6, axis=-1)
        mods_p = [m[:b].reshape(b, 1, d) for m in mod6]
        mods_s = [m[b:].reshape(1, db, d) for m in mod6]

        rq, rk, rv, rg, aq, ak, av = _project(y_p, mods_p, consts, weights)
        seq = lambda a: a.reshape(b, s, a.shape[-1])
        ret, ret_state = _retention_prompt(consts["lg"], seq(rq), seq(rk), seq(rv), seq(rg), gr, 1024)
        att = _attention_prompt(consts["slopes"], seq(aq), seq(ak), seq(av))
        y_p = _mix_and_peer(y_p, ret.reshape(b * s, RET_W), att.reshape(b * s, ATT_W), mods_p, consts, weights)
        keep = min(MAX_WINDOW, s)
        outs["ret_p"].append(ret_state)
        outs["wk_p"].append(seq(ak)[:, s - keep:].reshape(b, keep, H_ATT, HEAD_DIM))
        outs["wv_p"].append(seq(av)[:, s - keep:].reshape(b, keep, H_ATT, HEAD_DIM))

        rq, rk, rv, rg, aq, ak, av = _project(y_s, mods_s, consts, weights)
        ret, new_state = _retention_sample(consts["lg"], rq, rk, rv, rg, gr, state_ret[l])
        att = _attention_sample(consts["slopes"], aq, ak, av, cache_win_k[l], cache_win_v[l])
        y_s = _mix_and_peer(y_s, ret.reshape(db, RET_W), att.reshape(db, ATT_W), mods_s, consts, weights)
        outs["ret_s"].append(new_state)
        outs["wk_s"].append(ak.reshape(db, 1, H_ATT, HEAD_DIM))
        outs["wv_s"].append(av.reshape(db, 1, H_ATT, HEAD_DIM))

    st = lambda k: jnp.stack(outs[k])
    return (y_p.reshape(b, s, d), y_s.reshape(db, 1, d), st("ret_p"), st("wk_p"), st("wv_p"),
            st("ret_s"), st("wk_s"), st("wv_s"))
```

```python
import functools
import math

import numpy as np
import jax
import jax.numpy as jnp
from jax import lax
from jax.experimental import pallas as pl
from jax.experimental.pallas import tpu as pltpu

F32 = jnp.float32
BF16 = jnp.bfloat16
I32 = jnp.int32

D_MODEL = 1024
HEAD_DIM = 64
RET_W = 512
ATT_W = 512
H_RET = 8
H_ATT = 8
N_PROJ = 4 * RET_W + 3 * ATT_W
RET_CHUNK = 128
DIL_BRANCHES = ((128, 1), (512, 4), (2048, 16))
NW = 128
MAX_WINDOW = 2048
N_KEYS = 128
PEER_HEADS = 8
PEER_TOPK = 16
PEER_SLOTS = PEER_HEADS * PEER_TOPK
EPS = 1e-6

LANES = 128
SUBLANES = 8
PAIR_ROWS = 2 * (D_MODEL // LANES)
PAIR_COLS = PAIR_ROWS * PEER_SLOTS
VMEM_LIMIT = 56 * 1024 * 1024
RET_UNROLL = 4
TOK_UNROLL = 32
TOK_UNROLL_WITH_TOPK = 16

SDS = jax.ShapeDtypeStruct


def _dot(a, b):
    return jnp.dot(a, b, preferred_element_type=F32)


def _dot_nt(a, b):
    return lax.dot_general(a, b, (((1,), (1,)), ((), ())), preferred_element_type=F32)


def _dot_tn(a, b):
    return lax.dot_general(a, b, (((0,), (0,)), ((), ())), preferred_element_type=F32)


def _split(x):
    hi = x.astype(BF16)
    lo = (x - hi.astype(F32)).astype(BF16)
    return hi, lo


def _silu(x):
    return x * jax.nn.sigmoid(x)


def _params(sem, vmem=VMEM_LIMIT):
    return pltpu.CompilerParams(dimension_semantics=sem, vmem_limit_bytes=vmem)


def _adaln_kernel(c_ref, w_ref, b_ref, o_ref):
    s = _silu(c_ref[...])
    o_ref[...] = jnp.dot(s, w_ref[...], preferred_element_type=F32,
                         precision=lax.Precision.HIGHEST) + b_ref[...]


def _adaln(c, w_ada, b_ada):
    rows = c.shape[0]
    tn = 512
    return pl.pallas_call(
        _adaln_kernel,
        out_shape=SDS((rows, 6 * D_MODEL), F32),
        grid=(6 * D_MODEL // tn,),
        in_specs=[pl.BlockSpec((rows, D_MODEL), lambda j: (0, 0)),
                  pl.BlockSpec((D_MODEL, tn), lambda j: (0, j)),
                  pl.BlockSpec((1, tn), lambda j: (0, j))],
        out_specs=pl.BlockSpec((rows, tn), lambda j: (0, j)),
        compiler_params=_params(("arbitrary",)),
        name="adaln",
    )(c, w_ada, b_ada.reshape(1, -1))


def _inproj_kernel(x_ref, sc_ref, sh_ref, g1_ref, w_ref, gq_ref, gk_ref, bd_ref,
                   rq_ref, rk_ref, rv_ref, rg_ref, aq_ref, ak_ref, av_ref):
    x = x_ref[...]
    ms = jnp.mean(x * x, axis=-1, keepdims=True)
    h = x * lax.rsqrt(ms + EPS) * g1_ref[...]
    h = h * (1.0 + sc_ref[...]) + sh_ref[...]
    p = _dot(h.astype(BF16), w_ref[...])
    scale = HEAD_DIM ** -0.5
    rq_ref[...] = p[:, 0:RET_W].astype(BF16)
    rk_ref[...] = (p[:, RET_W:2 * RET_W] * scale).astype(BF16)
    rv_ref[...] = p[:, 2 * RET_W:3 * RET_W].astype(BF16)
    rg_ref[...] = p[:, 3 * RET_W:4 * RET_W].astype(BF16)
    o = 4 * RET_W

    def head_norm(a, g):
        msq = _dot((a * a).astype(BF16), bd_ref[...])
        return a * lax.rsqrt(msq + EPS) * g

    aq_ref[...] = head_norm(p[:, o:o + ATT_W], gq_ref[...]) * scale
    ak_ref[...] = head_norm(p[:, o + ATT_W:o + 2 * ATT_W], gk_ref[...])
    av_ref[...] = p[:, o + 2 * ATT_W:o + 3 * ATT_W]


def _inproj(x2d, sc, sh, g1, w_in_bf, gq, gk, bd, tb):
    n = x2d.shape[0]
    groups, rows, _ = sc.shape
    tiles_per_group = n // tb // groups
    mod_spec = pl.BlockSpec((None, rows, D_MODEL), lambda i: (i // tiles_per_group, 0, 0))
    tok = lambda w: pl.BlockSpec((tb, w), lambda i: (i, 0))
    full = lambda a: pl.BlockSpec(a.shape, lambda i: (0,) * a.ndim)
    outs = [SDS((n, RET_W), BF16)] * 4 + [SDS((n, ATT_W), F32)] * 3
    return pl.pallas_call(
        _inproj_kernel,
        out_shape=outs,
        grid=(n // tb,),
        in_specs=[tok(D_MODEL), mod_spec, mod_spec, full(g1), full(w_in_bf), full(gq), full(gk), full(bd)],
        out_specs=[tok(RET_W)] * 4 + [tok(ATT_W)] * 3,
        compiler_params=_params(("arbitrary",)),
        name="inproj",
    )(x2d, sc, sh, g1, w_in_bf, gq, gk, bd)


def _ret_kernel(lg_ref, q_ref, k_ref, v_ref, rg_ref, gr_ref, o_ref, st_ref, s_scr, *, nchunk):
    hp = pl.program_id(1)
    t = pl.program_id(2)
    C = RET_CHUNK
    lane = lax.broadcasted_iota(I32, (1, LANES), 1)
    m0 = lane < HEAD_DIM
    lg0 = lg_ref[2 * hp]
    lg1 = lg_ref[2 * hp + 1]
    lgv = jnp.where(m0, lg0, lg1)

    @pl.when(t == 0)
    def _():
        s_scr[...] = jnp.zeros_like(s_scr)

    ii = lax.broadcasted_iota(I32, (C, C), 0)
    jj = lax.broadcasted_iota(I32, (C, C), 1)
    diff = ii - jj
    dpos = jnp.maximum(diff, 0).astype(F32)
    dec01 = jnp.concatenate([jnp.where(diff >= 0, jnp.exp(dpos * lg0), 0.0),
                             jnp.where(diff >= 0, jnp.exp(dpos * lg1), 0.0)], axis=0)

    def head_mean(x):
        s0 = jnp.sum(jnp.where(m0, x, 0.0), axis=-1, keepdims=True)
        s1 = jnp.sum(jnp.where(m0, 0.0, x), axis=-1, keepdims=True)
        return jnp.where(m0, s0, s1) * (1.0 / HEAD_DIM)
    ri = lax.broadcasted_iota(I32, (C, LANES), 0).astype(F32)
    w_start = jnp.exp((ri + 1.0) * lgv)
    w_end = jnp.exp((float(C) - 1.0 - ri) * lgv)
    chunk_decay = jnp.exp(float(C) * lgv)
    rr = lax.broadcasted_iota(I32, (LANES, LANES), 0)
    cc = lax.broadcasted_iota(I32, (LANES, LANES), 1)
    same_head = (rr < HEAD_DIM) == (cc < HEAD_DIM)
    gr = gr_ref[...]

    def chunk(c, state):
        rows = pl.ds(pl.multiple_of(c * C, C), C)
        q = q_ref[rows, :]
        k = k_ref[rows, :]
        v = v_ref[rows, :]
        qf = q.astype(F32)
        kf = k.astype(F32)
        q01 = jnp.concatenate([jnp.where(m0, qf, 0.0), jnp.where(m0, 0.0, qf)], axis=0).astype(BF16)
        p01 = (_dot_nt(q01, k) * dec01).astype(BF16)
        pv = _dot(p01, v)
        o_intra = jnp.where(m0, pv[0:C, :], pv[C:, :])
        o_inter = _dot((qf * w_start).astype(BF16), state.astype(BF16))
        o = o_intra + o_inter
        kv = _dot_tn((kf * w_end).astype(BF16), v)
        mu = head_mean(o)
        xc = o - mu
        var = head_mean(xc * xc)
        y = xc * lax.rsqrt(var + EPS) * gr
        y = y * _silu(rg_ref[rows, :].astype(F32))
        o_ref[rows, :] = y.astype(BF16)
        return chunk_decay * state + jnp.where(same_head, kv, 0.0)

    def chunk_group(i, state):
        for u in range(RET_UNROLL):
            state = chunk(i * RET_UNROLL + u, state)
        return state

    s_scr[...] = lax.fori_loop(0, nchunk // RET_UNROLL, chunk_group, s_scr[...])

    @pl.when(t == pl.num_programs(2) - 1)
    def _():
        state = s_scr[...]
        st_ref[0] = state[0:HEAD_DIM, 0:HEAD_DIM]
        st_ref[1] = state[HEAD_DIM:, HEAD_DIM:]


def _retention_prompt(lg, rq, rk, rv, rg, g_ret, cb):
    b, s, _ = rq.shape
    nchunk = cb // RET_CHUNK
    blk = pl.BlockSpec((None, cb, LANES), lambda bi, hp, t, lg_: (bi, t, hp))
    grid_spec = pltpu.PrefetchScalarGridSpec(
        num_scalar_prefetch=1,
        grid=(b, H_RET // 2, s // cb),
        in_specs=[blk, blk, blk, blk,
                  pl.BlockSpec((1, LANES), lambda bi, hp, t, lg_: (0, hp))],
        out_specs=[blk, pl.BlockSpec((None, 2, HEAD_DIM, HEAD_DIM), lambda bi, hp, t, lg_: (bi, hp, 0, 0))],
        scratch_shapes=[pltpu.VMEM((LANES, LANES), F32)],
    )
    return pl.pallas_call(
        functools.partial(_ret_kernel, nchunk=nchunk),
        out_shape=[SDS((b, s, RET_W), BF16), SDS((b, H_RET, HEAD_DIM, HEAD_DIM), F32)],
        grid_spec=grid_spec,
        compiler_params=_params(("arbitrary", "arbitrary", "arbitrary")),
        name="retention_prompt",
    )(lg, rq, rk, rv, rg, g_ret)


ATT_TILE = MAX_WINDOW


ATT_UNROLL = 8


def _att_kernel(sl_ref, q_ref, k_ref, v_ref, o_ref, kk, vv, *stats):
    hp = pl.program_id(1)
    t = pl.program_id(2)
    TS = ATT_TILE
    nbr = len(DIL_BRANCHES)
    m_br, l_br, acc_br = stats[:nbr], stats[nbr:2 * nbr], stats[2 * nbr:]
    cur = pl.multiple_of((t & 1) * TS, TS)

    @pl.when(t == 0)
    def _():
        kk[TS:2 * TS, :] = jnp.zeros((TS, LANES), F32)
        vv[TS:2 * TS, :] = jnp.zeros((TS, LANES), F32)

    kk[pl.ds(cur, TS), :] = k_ref[...]
    vv[pl.ds(cur, TS), :] = v_ref[...]

    lane = lax.broadcasted_iota(I32, (1, LANES), 1)
    m0 = lane < HEAD_DIM
    head_lanes = (m0, jnp.logical_not(m0))
    ii = lax.broadcasted_iota(I32, (NW, 2 * NW), 0)
    jj = lax.broadcasted_iota(I32, (NW, 2 * NW), 1)
    off = ii + NW - jj
    band = (off >= 0) & (off <= NW)
    off_f = off.astype(F32)
    ones_h = [jnp.broadcast_to(jnp.where(hl, 1.0, 0.0).astype(BF16), (2 * NW, LANES)) for hl in head_lanes]

    for br, (_, dil) in enumerate(DIL_BRANCHES):
        nblk = TS // (NW * dil)
        shift = int(math.log2(nblk))
        bias = [jnp.where(band, -(sl_ref[2 * hp + hh] * float(dil)) * off_f, -jnp.inf) for hh in range(2)]

        def one_block(bs, dil=dil, nblk=nblk, shift=shift, br=br, bias=bias):
            r = lax.shift_right_logical(bs, shift)
            blk = bs & (nblk - 1)
            qs = r + dil * NW * blk
            q_rows = pl.ds(qs, NW, stride=dil)
            own = pl.ds(cur + qs, NW, stride=dil)
            prev = pl.ds((cur + qs - dil * NW) & (2 * TS - 1), NW, stride=dil)
            qb = q_ref[q_rows, :]
            kb = jnp.concatenate([kk[prev, :], kk[own, :]], axis=0).astype(BF16)
            vb = jnp.concatenate([vv[prev, :], vv[own, :]], axis=0)
            first_key = jnp.where(jnp.logical_or(t > 0, blk > 0), 0, NW)
            started = jj >= first_key
            res = None
            mx = []
            for hh in range(2):
                qh = jnp.where(head_lanes[hh], qb, 0.0).astype(BF16)
                s = jnp.where(started, _dot_nt(qh, kb) + bias[hh], -jnp.inf)
                mh = jnp.max(s, axis=-1, keepdims=True)
                p = jnp.exp(s - mh).astype(BF16)
                vh = jnp.concatenate([jnp.where(head_lanes[hh], vb, 0.0).astype(BF16), ones_h[hh]], axis=1)
                part = _dot(p, vh)
                res = part if res is None else res + part
                mx.append(mh)
            m_br[br][q_rows, :] = jnp.where(m0, mx[0], mx[1])
            acc_br[br][q_rows, :] = res[:, :LANES]
            l_br[br][q_rows, :] = res[:, LANES:]

        def body(i, carry, one_block=one_block):
            for u in range(ATT_UNROLL):
                one_block(i * ATT_UNROLL + u)
            return carry

        lax.fori_loop(0, TS // NW // ATT_UNROLL, body, 0)

    rows_per_step = 256

    def merge(i, carry):
        rows = pl.ds(pl.multiple_of(i * rows_per_step, rows_per_step), rows_per_step)
        ms = [m[rows, :] for m in m_br]
        m_all = functools.reduce(jnp.maximum, ms)
        ws = [jnp.exp(m - m_all) for m in ms]
        num = sum(w * a[rows, :] for w, a in zip(ws, acc_br))
        den = sum(w * l[rows, :] for w, l in zip(ws, l_br))
        o_ref[rows, :] = (num / den).astype(BF16)
        return carry

    lax.fori_loop(0, TS // rows_per_step, merge, 0)


def _attention_prompt(slopes, aq, ak, av):
    b, s, _ = aq.shape
    ts = ATT_TILE
    blk = pl.BlockSpec((None, ts, LANES), lambda bi, hp, t, sl: (bi, t, hp))
    grid_spec = pltpu.PrefetchScalarGridSpec(
        num_scalar_prefetch=1,
        grid=(b, H_ATT // 2, s // ts),
        in_specs=[blk, blk, blk],
        out_specs=blk,
        scratch_shapes=[pltpu.VMEM((2 * ts, LANES), F32)] * 2
        + [pltpu.VMEM((ts, LANES), F32)] * (3 * len(DIL_BRANCHES)),
    )
    return pl.pallas_call(
        _att_kernel,
        out_shape=SDS((b, s, ATT_W), BF16),
        grid_spec=grid_spec,
        compiler_params=_params(("arbitrary", "arbitrary", "arbitrary")),
        name="attention_prompt",
    )(slopes, aq, ak, av)


def _sret_kernel(lg_ref, q_ref, k_ref, v_ref, rg_ref, gr_ref, st_ref, o_ref, sn_ref):
    q = q_ref[...].astype(F32)
    k = k_ref[...].astype(F32)
    v = v_ref[...].astype(F32)
    rg = rg_ref[...].astype(F32)
    gr = gr_ref[...]
    r_i = lax.broadcasted_iota(I32, (HEAD_DIM, HEAD_DIM), 0)
    c_i = lax.broadcasted_iota(I32, (HEAD_DIM, HEAD_DIM), 1)
    eye = r_i == c_i
    outs = []
    for h in range(H_RET):
        sl = slice(h * HEAD_DIM, (h + 1) * HEAD_DIM)
        qh, kh, vh = q[:, sl], k[:, sl], v[:, sl]
        gamma = jnp.exp(jnp.full((1, HEAD_DIM), lg_ref[h], F32))
        q_col = jnp.sum(jnp.where(eye, qh, 0.0), axis=1, keepdims=True)
        k_col = jnp.sum(jnp.where(eye, kh, 0.0), axis=1, keepdims=True)
        state = st_ref[h]
        o_inter = gamma * jnp.sum(q_col * state, axis=0, keepdims=True)
        o_intra = jnp.sum(qh * kh, axis=-1, keepdims=True) * vh
        sn_ref[h] = gamma * state + k_col * vh
        o = o_intra + o_inter
        mu = jnp.mean(o, axis=-1, keepdims=True)
        xc = o - mu
        var = jnp.mean(xc * xc, axis=-1, keepdims=True)
        outs.append(xc * lax.rsqrt(var + EPS) * gr[:, sl] * _silu(rg[:, sl]))
    o_ref[...] = jnp.concatenate(outs, axis=-1).astype(BF16)


def _retention_sample(lg, rq, rk, rv, rg, g_ret, state):
    db = rq.shape[0]
    row = pl.BlockSpec((None, 1, RET_W), lambda i: (i, 0, 0))
    st = pl.BlockSpec((None, H_RET, HEAD_DIM, HEAD_DIM), lambda i: (i, 0, 0, 0))
    r3 = lambda a: a.reshape(db, 1, RET_W)
    return pl.pallas_call(
        _sret_kernel,
        out_shape=[SDS((db, 1, RET_W), BF16), SDS(state.shape, F32)],
        grid=(db,),
        in_specs=[pl.BlockSpec(memory_space=pltpu.SMEM), row, row, row, row,
                  pl.BlockSpec((1, RET_W), lambda i: (0, 0)), st],
        out_specs=[row, st],
        compiler_params=_params(("arbitrary",)),
        name="retention_sample",
    )(lg, r3(rq), r3(rk), r3(rv), r3(rg), g_ret, state)


def _satt_kernel(sl_ref, qt_ref, knt_ref, vnt_ref, kt_ref, vt_ref, o_ref):
    win = kt_ref.shape[-1]
    dist = win - lax.broadcasted_iota(I32, (1, win), 1)
    count = jnp.zeros((1, win), F32)
    for window, dil in DIL_BRANCHES:
        reads = ((dist & (dil - 1)) == 0) & (dist <= window)
        count = count + jnp.where(reads, 1.0, 0.0)
    dist_f = dist.astype(F32)
    nbr = float(len(DIL_BRANCHES))
    for h in range(H_ATT):
        q = qt_ref[:, h:h + 1]
        s = jnp.sum(kt_ref[h] * q, axis=0, keepdims=True) - sl_ref[h] * dist_f
        s = jnp.where(count > 0.0, s, -jnp.inf)
        s_new = jnp.sum(q * knt_ref[:, h:h + 1], axis=0, keepdims=True)
        m = jnp.maximum(jnp.max(s, axis=-1, keepdims=True), s_new)
        p = count * jnp.exp(s - m)
        p_new = nbr * jnp.exp(s_new - m)
        den = jnp.sum(p, axis=-1, keepdims=True) + p_new
        num = jnp.sum(vt_ref[h] * p, axis=-1, keepdims=True) + p_new * vnt_ref[:, h:h + 1]
        o_ref[:, h:h + 1] = num / den


def _attention_sample(slopes, aq, ak, av, win_k, win_v):
    db = aq.shape[0]
    win = win_k.shape[1]
    col = pl.BlockSpec((None, HEAD_DIM, H_ATT), lambda i: (i, 0, 0))
    cache = pl.BlockSpec((None, H_ATT, HEAD_DIM, win), lambda i: (i, 0, 0, 0))
    cols = lambda a: jnp.transpose(a.reshape(db, H_ATT, HEAD_DIM), (0, 2, 1))
    keys_minor = lambda c: jnp.transpose(c, (0, 2, 3, 1))
    out = pl.pallas_call(
        _satt_kernel,
        out_shape=SDS((db, HEAD_DIM, H_ATT), F32),
        grid=(db,),
        in_specs=[pl.BlockSpec(memory_space=pltpu.SMEM), col, col, col, cache, cache],
        out_specs=col,
        compiler_params=_params(("arbitrary",)),
        name="attention_sample",
    )(slopes, cols(aq), cols(ak), cols(av), keys_minor(win_k), keys_minor(win_v))
    return jnp.transpose(out, (0, 2, 1)).reshape(db, ATT_W).astype(BF16)


def _outproj_kernel(x_ref, ret_ref, att_ref, wo_ref, gt1_ref, g2_ref, sc2_ref, sh2_ref, wqh_ref, wql_ref,
                    x1_ref, h2_ref, qp_ref):
    mix = _dot(ret_ref[...], wo_ref[0:RET_W, :]) + _dot(att_ref[...], wo_ref[RET_W:, :])
    x1 = x_ref[...] + gt1_ref[...] * mix
    x1_ref[...] = x1
    ms = jnp.mean(x1 * x1, axis=-1, keepdims=True)
    h2 = x1 * lax.rsqrt(ms + EPS) * g2_ref[...]
    h2 = h2 * (1.0 + sc2_ref[...]) + sh2_ref[...]
    h2_ref[...] = h2
    hi, lo = _split(h2)
    wqh = wqh_ref[...]
    qp_ref[...] = _dot(hi, wqh) + (_dot(lo, wqh) + _dot(hi, wql_ref[...]))


def _outproj(x2d, ret, att, wo_bf, gt1, g2, sc2, sh2, wq_hi, wq_lo, tb):
    n = x2d.shape[0]
    groups, rows, _ = gt1.shape
    tiles_per_group = n // tb // groups
    mod_spec = pl.BlockSpec((None, rows, D_MODEL), lambda i: (i // tiles_per_group, 0, 0))
    tok = lambda w: pl.BlockSpec((tb, w), lambda i: (i, 0))
    full = lambda a: pl.BlockSpec(a.shape, lambda i: (0,) * a.ndim)
    return pl.pallas_call(
        _outproj_kernel,
        out_shape=[SDS((n, D_MODEL), F32)] * 3,
        grid=(n // tb,),
        in_specs=[tok(D_MODEL), tok(RET_W), tok(ATT_W), full(wo_bf), mod_spec, full(g2), mod_spec, mod_spec,
                  full(wq_hi), full(wq_lo)],
        out_specs=[tok(D_MODEL)] * 3,
        compiler_params=_params(("arbitrary",)),
        name="outproj",
    )(x2d, ret, att, wo_bf, gt1, g2, sc2, sh2, wq_hi, wq_lo)


def _top16(sc, ids=None):
    if ids is None:
        ids = lax.broadcasted_iota(I32, sc.shape, 0)
    ids = ids.astype(F32)
    vals, idxs = [], []
    for _ in range(PEER_TOPK):
        m = jnp.max(sc, axis=0, keepdims=True)
        idx = jnp.min(jnp.where(sc == m, ids, jnp.inf), axis=0, keepdims=True)
        vals.append(m)
        idxs.append(idx)
        sc = jnp.where(ids == idx, -jnp.inf, sc)
    return jnp.concatenate(vals, axis=0), jnp.concatenate(idxs, axis=0).astype(I32)


def _pair_candidates(s1, s2):
    k = PEER_TOPK
    half = k // 2
    pieces = [s1[0:1, :] + s2]
    pieces += [s1[a:a + 1, :] + s2[0:half, :] for a in range(1, half)]
    pieces.append(s1[half:, :] + s2[0:1, :])
    cand = jnp.concatenate(pieces, axis=0)
    r = lax.broadcasted_iota(I32, cand.shape, 0)
    mid = r - k
    a_mid = 1 + lax.shift_right_logical(mid, 3)
    b_mid = mid & (half - 1)
    tail_start = k + half * (half - 1)
    pos = jnp.where(r < k, r, jnp.where(r < tail_start, a_mid * k + b_mid, (half + r - tail_start) * k))
    reachable = jnp.logical_or(jnp.logical_or(r < k, r >= tail_start), (a_mid + 1) * (b_mid + 1) <= k)
    return jnp.where(reachable, cand, -jnp.inf), pos


TOPK_HEADS = 8


def _topk_one_head(q, skh_ref, skl_ref, hh):
    qh, ql = _split(q)

    def scores(c):
        kh = skh_ref[hh, c]
        return _dot_nt(kh, qh) + (_dot_nt(kh, ql) + _dot_nt(skl_ref[hh, c], qh))

    s1, i1 = _top16(scores(0))
    s2, i2 = _top16(scores(1))
    top_s, pos = _top16(*_pair_candidates(s1, s2))
    a_sel = lax.shift_right_logical(pos, 4)
    b_sel = pos & (PEER_TOPK - 1)
    io = lax.broadcasted_iota(I32, i1.shape, 0)
    rows = []
    for r in range(PEER_TOPK):
        ia = jnp.sum(jnp.where(io == a_sel[r:r + 1, :], i1, 0), axis=0, keepdims=True)
        ib = jnp.sum(jnp.where(io == b_sel[r:r + 1, :], i2, 0), axis=0, keepdims=True)
        rows.append(ia * N_KEYS + ib)
    ex = jnp.exp(top_s - top_s[0:1, :])
    return jnp.concatenate(rows, axis=0), ex / jnp.sum(ex, axis=0, keepdims=True)


def _topk_kernel(q_ref, skh_ref, skl_ref, e_ref, g_ref, e_scr, g_scr):
    step = pl.program_id(1)
    for hh in range(TOPK_HEADS):
        e, g = _topk_one_head(q_ref[:, LANES * hh:LANES * (hh + 1)], skh_ref, skl_ref, hh)
        slot_rows = pl.ds(pl.multiple_of((step * TOPK_HEADS + hh) * PEER_TOPK, PEER_TOPK), PEER_TOPK)
        e_scr[slot_rows, :] = e
        g_scr[slot_rows, :] = g

    @pl.when(step == pl.num_programs(1) - 1)
    def _():
        e_ref[...] = e_scr[...].T
        g_ref[...] = g_scr[...].T


def _peer_topk(qp, sk_hi, sk_lo, tt, start, count):
    first = start // tt
    out_blk = pl.BlockSpec((tt, PEER_SLOTS), lambda i, h: (i, 0))
    sk_blk = pl.BlockSpec((TOPK_HEADS, 2, N_KEYS, LANES), lambda i, h: (h, 0, 0, 0))
    return pl.pallas_call(
        _topk_kernel,
        out_shape=[SDS((count, PEER_SLOTS), I32), SDS((count, PEER_SLOTS), F32)],
        grid=(count // tt, PEER_HEADS // TOPK_HEADS),
        in_specs=[pl.BlockSpec((tt, TOPK_HEADS * LANES), lambda i, h: (i + first, h)), sk_blk, sk_blk],
        out_specs=[out_blk, out_blk],
        scratch_shapes=[pltpu.VMEM((PEER_SLOTS, tt), I32), pltpu.VMEM((PEER_SLOTS, tt), F32)],
        compiler_params=_params(("arbitrary", "arbitrary")),
        name="peer_topk",
    )(qp, sk_hi, sk_lo)


def _gather_rows(idx_ref, tab_ref, t):
    rows = []
    for k in range(PEER_SLOTS):
        e = pl.multiple_of(idx_ref[t, k], PAIR_ROWS)
        rows.append(tab_ref[pl.ds(e, PAIR_ROWS), :])
    return jnp.concatenate(rows, axis=0)


def _diag_mask():
    sub = lax.broadcasted_iota(I32, (SUBLANES, PAIR_COLS), 0)
    col = lax.broadcasted_iota(I32, (SUBLANES, PAIR_COLS), 1)
    return (col & (SUBLANES - 1)) == sub


def _own_half(par, spread):
    member = (lax.broadcasted_iota(I32, (1, PAIR_COLS), 1) >> 3) & 1
    return 1.0 - jnp.abs(member.astype(F32) - _dot(par.astype(BF16), spread))


def _for_each_token(tb, one_token, unroll=TOK_UNROLL, per_step=None):
    def group(i, carry):
        if per_step is not None:
            per_step(i)
        for u in range(unroll):
            one_token(i * unroll + u)
        return carry

    lax.fori_loop(0, tb // unroll, group, 0)


def _peer_u_kernel(idx_ref, tab_ref, h2_ref, g_ref, par_ref, sel_ref, spread_ref, *refs, with_topk):
    if with_topk:
        q_ref, skh_ref, skl_ref, w_ref, e_ref, gate_ref, xbuf, rall, own, e_scr, g_scr = refs
    else:
        w_ref, xbuf, rall, own = refs
    tb = h2_ref.shape[0]
    for j in range(SUBLANES):
        xbuf[pl.ds(j, tb, stride=SUBLANES), :] = h2_ref[:, LANES * j:LANES * (j + 1)]
    own[...] = _own_half(par_ref[...], spread_ref[...])
    diag = _diag_mask()

    def tok(t):
        rows = _gather_rows(idx_ref, tab_ref, t)
        xt = xbuf[pl.ds(pl.multiple_of(t * SUBLANES, SUBLANES), SUBLANES), :].astype(BF16)
        r = _dot_nt(xt, rows)
        keep = jnp.where(diag, r * own[pl.ds(t, 1), :], 0.0)
        rall[pl.ds(t, 1), :] = jnp.sum(keep, axis=0, keepdims=True)

    unroll = TOK_UNROLL_WITH_TOPK if with_topk else TOK_UNROLL
    halves = tb // LANES
    units_per_step = PEER_HEADS * halves * unroll // tb

    def topk_head(i):
        for j in range(units_per_step):
            unit = i * units_per_step + j
            h = lax.shift_right_logical(unit, int(math.log2(halves)))
            toks = pl.ds(pl.multiple_of((unit & (halves - 1)) * LANES, LANES), LANES)
            q = q_ref[toks, pl.ds(pl.multiple_of(h * LANES, LANES), LANES)]
            e, g = _topk_one_head(q, skh_ref, skl_ref, h)
            slot_rows = pl.ds(pl.multiple_of(h * PEER_TOPK, PEER_TOPK), PEER_TOPK)
            e_scr[slot_rows, toks] = e
            g_scr[slot_rows, toks] = g

    _for_each_token(tb, tok, unroll, topk_head if with_topk else None)
    hi, lo = _split(rall[...])
    sel = sel_ref[...]
    a = _dot(hi, sel) + _dot(lo, sel)
    gelu = 0.5 * a * (1.0 + lax.erf(a * (2.0 ** -0.5)))
    w_ref[...] = g_ref[...] * gelu
    if with_topk:
        e_ref[...] = e_scr[...].T
        gate_ref[...] = g_scr[...].T


def _peer_v_kernel(idx_ref, tab_ref, w_ref, par_ref, x1_ref, gt2_ref, spread_ref, y_ref, wrep, ybuf):
    tb = x1_ref.shape[0]
    spread = spread_ref[...]
    wrep[...] = _dot(w_ref[...].astype(BF16), spread) * _own_half(par_ref[...], spread)
    diag = _diag_mask()

    def tok(t):
        rows = _gather_rows(idx_ref, tab_ref, t)
        wt = jnp.where(diag, wrep[pl.ds(t, 1), :], 0.0).astype(BF16)
        ybuf[pl.ds(pl.multiple_of(t * SUBLANES, SUBLANES), SUBLANES), :] = _dot(wt, rows)

    _for_each_token(tb, tok)
    for j in range(SUBLANES):
        cols = slice(LANES * j, LANES * (j + 1))
        y_ref[:, cols] = x1_ref[:, cols] + gt2_ref[:, cols] * ybuf[pl.ds(j, tb, stride=SUBLANES), :]


def _topk_units_fit(tb):
    units, steps = PEER_HEADS * (tb // LANES), tb // TOK_UNROLL_WITH_TOPK
    return tb % LANES == 0 and steps > 0 and units % steps == 0


def _pair_index(eid):
    return lax.shift_right_logical(eid, 1) * PAIR_ROWS, (eid & 1).astype(F32)


def _peer_u(eid, tab, h2, g, sel, spread, tb, start, topk_next=None):
    count = eid.shape[0]
    first = start // tb
    idx, par = _pair_index(eid)
    piece = lambda w: pl.BlockSpec((tb, w), lambda i: (i, 0))
    full = lambda a: pl.BlockSpec(a.shape, lambda i: (0,) * a.ndim)
    in_specs = [pl.BlockSpec((tb, PEER_SLOTS), lambda i: (i, 0), memory_space=pltpu.SMEM),
                pl.BlockSpec(memory_space=pltpu.VMEM),
                pl.BlockSpec((tb, D_MODEL), lambda i: (i + first, 0)), piece(PEER_SLOTS), piece(PEER_SLOTS),
                full(sel), full(spread)]
    args = [idx, tab, h2, g, par, sel, spread]
    out_shape = [SDS((count, PEER_SLOTS), F32)]
    out_specs = [piece(PEER_SLOTS)]
    scratch = [pltpu.VMEM((tb * SUBLANES, LANES), F32), pltpu.VMEM((tb, PAIR_COLS), F32),
               pltpu.VMEM((tb, PAIR_COLS), F32)]
    if topk_next is not None:
        qp, sk_hi, sk_lo, start_next = topk_next
        assert _topk_units_fit(tb)
        first_next = start_next // tb
        in_specs += [pl.BlockSpec((tb, D_MODEL), lambda i: (i + first_next, 0)), full(sk_hi), full(sk_lo)]
        args += [qp, sk_hi, sk_lo]
        out_shape += [SDS((count, PEER_SLOTS), I32), SDS((count, PEER_SLOTS), F32)]
        out_specs += [piece(PEER_SLOTS), piece(PEER_SLOTS)]
        scratch += [pltpu.VMEM((PEER_SLOTS, tb), I32), pltpu.VMEM((PEER_SLOTS, tb), F32)]
    return pl.pallas_call(
        functools.partial(_peer_u_kernel, with_topk=topk_next is not None),
        out_shape=out_shape,
        grid=(count // tb,),
        in_specs=in_specs,
        out_specs=out_specs,
        scratch_shapes=scratch,
        compiler_params=_params(("arbitrary",)),
        name="peer_u",
    )(*args)


def _peer_v(eid, tab, w, x1, gt2, spread, tb):
    n = x1.shape[0]
    groups, rows, _ = gt2.shape
    tiles_per_group = n // tb // groups
    idx, par = _pair_index(eid)
    tok = lambda wd: pl.BlockSpec((tb, wd), lambda i: (i, 0))
    return pl.pallas_call(
        _peer_v_kernel,
        out_shape=SDS((n, D_MODEL), F32),
        grid=(n // tb,),
        in_specs=[pl.BlockSpec((tb, PEER_SLOTS), lambda i: (i, 0), memory_space=pltpu.SMEM),
                  pl.BlockSpec(memory_space=pltpu.VMEM), tok(PEER_SLOTS), tok(PEER_SLOTS), tok(D_MODEL),
                  pl.BlockSpec((None, rows, D_MODEL), lambda i: (i // tiles_per_group, 0, 0)),
                  pl.BlockSpec(spread.shape, lambda i: (0, 0))],
        out_specs=tok(D_MODEL),
        scratch_shapes=[pltpu.VMEM((tb, PAIR_COLS), F32), pltpu.VMEM((tb * SUBLANES, LANES), F32)],
        compiler_params=_params(("arbitrary",)),
        name="peer_v",
    )(idx, tab, w, par, x1, gt2, spread)


def _pack_table(tab):
    return tab.astype(BF16).reshape(tab.shape[0] * SUBLANES, LANES)


def _block_avg(width):
    i = np.arange(width) // HEAD_DIM
    return jnp.asarray((i[:, None] == i[None, :]).astype(np.float32) / HEAD_DIM, BF16)


def _slot_select():
    col = np.arange(PAIR_COLS) // PAIR_ROWS
    return jnp.asarray((col[:, None] == np.arange(PEER_SLOTS)[None, :]).astype(np.float32), BF16)


def _ret_log_decay():
    return jnp.log1p(-jnp.exp2(-5.0 - jnp.arange(H_RET, dtype=F32)))


def _alibi_slopes():
    return jnp.exp2(-8.0 * (jnp.arange(H_ATT, dtype=F32) + 1.0) / H_ATT)


def _pad_sub_keys(sub_keys):
    z = jnp.zeros_like(sub_keys[:, 0])
    return jnp.stack([jnp.concatenate([sub_keys[:, 0], z], axis=-1),
                      jnp.concatenate([z, sub_keys[:, 1]], axis=-1)], axis=1)


def _group_w_out(w):
    w4 = w.reshape(H_RET, 2, HEAD_DIM, D_MODEL)
    return jnp.concatenate([w4[:, 0].reshape(RET_W, D_MODEL), w4[:, 1].reshape(ATT_W, D_MODEL)], axis=0)


def _split_w(w):
    hi = w.astype(BF16)
    return hi, (w - hi.astype(F32)).astype(BF16)


def _token_tile(n, pref):
    return pref if n % pref == 0 else n


PEER_PIECES = 8


def _peer(h2, qp, x1, gt2, consts, weights):
    n = h2.shape[0]
    tb = _token_tile(n, 256)
    sk = (weights["sk_hi"], weights["sk_lo"])
    pieces = PEER_PIECES if (_topk_units_fit(tb) and n % (PEER_PIECES * tb) == 0) else 1
    count = n // pieces
    idx, gate = _peer_topk(qp, *sk, tb, 0, count)
    idxs, ws = [], []
    for p in range(pieces):
        idxs.append(idx)
        if p + 1 < pieces:
            w, idx, gate = _peer_u(idx, weights["u_tab"], h2, gate, consts["sel"], consts["selt"], tb, p * count,
                                   topk_next=(qp, *sk, (p + 1) * count))
        else:
            w, = _peer_u(idx, weights["u_tab"], h2, gate, consts["sel"], consts["selt"], tb, p * count)
        ws.append(w)
    idx_all = jnp.concatenate(idxs, axis=0) if pieces > 1 else idxs[0]
    w_all = jnp.concatenate(ws, axis=0) if pieces > 1 else ws[0]
    return _peer_v(idx_all, weights["v_tab"], w_all, x1, gt2, consts["selt"], tb)


def _mix_and_peer(x2d, ret, att, mods, consts, weights):
    sh1, sc1, gt1, sh2, sc2, gt2 = mods
    n = x2d.shape[0]
    x1, h2, qp = _outproj(x2d, ret, att, weights["w_out"], gt1, weights["g_norm2"], sc2, sh2,
                          weights["wq_hi"], weights["wq_lo"], _token_tile(n, 512))
    return _peer(h2, qp, x1, gt2, consts, weights)


def _project(x2d, mods, consts, weights):
    sh1, sc1 = mods[0], mods[1]
    n = x2d.shape[0]
    return _inproj(x2d, sc1, sh1, weights["g_norm1"], weights["w_in"], weights["g_qn"], weights["g_kn"],
                   consts["avg_att"], _token_tile(n, 512))


def kernel(x_prompt, x_sample, c_prompt, c_sample, state_ret, cache_win_k, cache_win_v, w_ada, b_ada, g_norm1, w_in, g_qn, g_kn, g_ret, w_out, g_norm2, w_pq, sub_keys, peer_u, peer_v):
    b, s, d = x_prompt.shape
    db, ds_, _ = x_sample.shape
    depth = w_ada.shape[0]
    assert d == D_MODEL and ds_ == 1 and s % ATT_TILE == 0
    win = cache_win_k.shape[2]
    assert win == MAX_WINDOW

    consts = {
        "avg_att": _block_avg(ATT_W),
        "sel": _slot_select(),
        "selt": _slot_select().T,
        "lg": _ret_log_decay(),
        "slopes": _alibi_slopes(),
    }

    y_p = x_prompt.reshape(b * s, d)
    y_s = x_sample.reshape(db, d)
    outs = {k: [] for k in ("ret_p", "wk_p", "wv_p", "ret_s", "wk_s", "wv_s")}
    for l in range(depth):
        wq_hi, wq_lo = _split_w(w_pq[l])
        sk_hi, sk_lo = _split_w(_pad_sub_keys(sub_keys[l]))
        weights = {
            "g_norm1": g_norm1[l].reshape(1, d), "g_norm2": g_norm2[l].reshape(1, d),
            "w_in": w_in[l].astype(BF16), "w_out": _group_w_out(w_out[l]).astype(BF16),
            "g_qn": g_qn[l].reshape(1, ATT_W), "g_kn": g_kn[l].reshape(1, ATT_W),
            "wq_hi": wq_hi, "wq_lo": wq_lo, "sk_hi": sk_hi, "sk_lo": sk_lo,
            "u_tab": _pack_table(peer_u[l]), "v_tab": _pack_table(peer_v[l]),
        }
        gr = g_ret[l].reshape(1, RET_W)

        mod = _adaln(jnp.concatenate([c_prompt, c_sample], axis=0), w_ada[l], b_ada[l])
        mod6 = jnp.split(mod, 6, axis=-1)
        mods_p = [m[:b].reshape(b, 1, d) for m in mod6]
        mods_s = [m[b:].reshape(1, db, d) for m in mod6]

        rq, rk, rv, rg, aq, ak, av = _project(y_p, mods_p, consts, weights)
        seq = lambda a: a.reshape(b, s, a.shape[-1])
        ret, ret_state = _retention_prompt(consts["lg"], seq(rq), seq(rk), seq(rv), seq(rg), gr, 1024)
        att = _attention_prompt(consts["slopes"], seq(aq), seq(ak), seq(av))
        y_p = _mix_and_peer(y_p, ret.reshape(b * s, RET_W), att.reshape(b * s, ATT_W), mods_p, consts, weights)
        keep = min(MAX_WINDOW, s)
        outs["ret_p"].append(ret_state)
        outs["wk_p"].append(seq(ak)[:, s - keep:].reshape(b, keep, H_ATT, HEAD_DIM))
        outs["wv_p"].append(seq(av)[:, s - keep:].reshape(b, keep, H_ATT, HEAD_DIM))

        rq, rk, rv, rg, aq, ak, av = _project(y_s, mods_s, consts, weights)
        ret, new_state = _retention_sample(consts["lg"], rq, rk, rv, rg, gr, state_ret[l])
        att = _attention_sample(consts["slopes"], aq, ak, av, cache_win_k[l], cache_win_v[l])
        y_s = _mix_and_peer(y_s, ret.reshape(db, RET_W), att.reshape(db, ATT_W), mods_s, consts, weights)
        outs["ret_s"].append(new_state)
        outs["wk_s"].append(ak.reshape(db, 1, H_ATT, HEAD_DIM))
        outs["wv_s"].append(av.reshape(db, 1, H_ATT, HEAD_DIM))

    st = lambda k: jnp.stack(outs[k])
    return (y_p.reshape(b, s, d), y_s.reshape(db, 1, d), st("ret_p"), st("wk_p"), st("wv_p"),
            st("ret_s"), st("wk_s"), st("wv_s"))
```
